```python
import functools
import jax, jax.numpy as jnp
from jax import lax
import numpy as np

D_MODEL = 1024
BATCH = 16
SEQ = 2048
DEPTH = 1
DEC_BATCH = 32
DEC_SEQ = 4
PAST_LEN = 16384
PAGE_SIZE = 128

HEAD_DIM = 64
NSA_HEADS = 8
NSA_KV_HEADS = 2
NSA_GROUP = NSA_HEADS // NSA_KV_HEADS
FOX_HEADS = 8
CMP_BLOCK = 32
CMP_STRIDE = 16
CMP_HIDDEN = 2 * HEAD_DIM
SEL_BLOCK = 64
N_SEL = 16
WINDOW = 512
Q_BLOCK = 128
D_FF = 2816
ALPHA = float((2.0 * DEPTH) ** 0.25)
BETA = float((8.0 * DEPTH) ** -0.25)
LN_EPS = 1e-5
NEG_INF = -1e30
FORCE = 1e9
FORGET_BIAS = 3.0
SCALE = HEAD_DIM ** -0.5
NSA_WIDTH = NSA_HEADS * HEAD_DIM
FOX_WIDTH = FOX_HEADS * HEAD_DIM
IN_SPLITS = (NSA_WIDTH, 6 * NSA_KV_HEADS * HEAD_DIM, 3 * NSA_HEADS, 3 * FOX_WIDTH, FOX_HEADS, 2 * D_MODEL)
IN_OFFSETS = tuple(sum(IN_SPLITS[:i + 1]) for i in range(len(IN_SPLITS) - 1))
IN_WIDTH = sum(IN_SPLITS)
FORGET_COL = IN_OFFSETS[4]

kernel_name = 'nsa_fox_macaron_deepnorm_step'


def layer_norm(x, g, b):
    xf = x.astype(jnp.float32)
    mu = jnp.mean(xf, axis=-1, keepdims=True)
    var = jnp.mean(jnp.square(xf - mu), axis=-1, keepdims=True)
    return ((xf - mu) * lax.rsqrt(var + LN_EPS) * g + b).astype(x.dtype)


def swiglu(x, w_gate, w_up, w_down):
    return (jax.nn.silu(x @ w_gate) * (x @ w_up)) @ w_down


def masked_softmax(logits, mask):
    logits = jnp.where(mask, logits.astype(jnp.float32), NEG_INF)
    return jax.nn.softmax(logits, axis=-1) * jnp.any(mask, axis=-1, keepdims=True)


def alibi_slopes():
    m = 2.0 ** (-8.0 * (np.arange(NSA_HEADS) + 1) / NSA_HEADS)
    return jnp.asarray(m.astype(np.float32).reshape(NSA_KV_HEADS, NSA_GROUP))


def gather_pages(pool, page_table):
    g = pool[page_table]
    return g.reshape((g.shape[0], g.shape[1] * g.shape[2]) + g.shape[3:])


def compress(rows, pe, w1, w2):
    B, L = rows.shape[:2]
    n_cmp = (L - CMP_BLOCK) // CMP_STRIDE + 1
    n_sub = CMP_BLOCK // CMP_STRIDE
    n_chunk = n_cmp + n_sub - 1
    chunks = rows[:, :n_chunk * CMP_STRIDE].reshape(B, n_chunk, CMP_STRIDE, NSA_KV_HEADS, HEAD_DIM)
    w1_sub = w1.reshape(n_sub, CMP_STRIDE, HEAD_DIM, CMP_HIDDEN)
    part = jnp.einsum('bcrgd,mrdh->bmcgh', chunks, w1_sub)
    hid = pe.reshape(-1) @ w1
    for m in range(n_sub):
        hid = hid + part[:, m, m:m + n_cmp]
    return jax.nn.gelu(hid) @ w2


def compressed_attention(q, kc, vc, q_off, slopes):
    Tq, n_cmp = q.shape[1], kc.shape[1]
    tpos = q_off + np.arange(Tq)
    end = np.arange(n_cmp) * CMP_STRIDE + CMP_BLOCK - 1
    dist = jnp.asarray((tpos[:, None] - end[None, :]).astype(np.float32))
    logits = jnp.einsum('btgrd,bngd->bgrtn', q, kc) * SCALE - slopes[:, :, None, None] * dist
    p = masked_softmax(logits, dist >= 0)
    o = jnp.einsum('bgrtn,bngd->btgrd', p.astype(vc.dtype), vc)
    return o, jnp.sum(p, axis=2)


def select_blocks(imp, seq_len, q_off):
    Tq, n_cmp = imp.shape[2], imp.shape[3]
    n_slc = -(-seq_len // SEL_BLOCK)
    cmp_start = np.arange(n_cmp)[:, None] * CMP_STRIDE
    slc_start = np.arange(n_slc)[None, :] * SEL_BLOCK
    overlap = ((cmp_start < slc_start + SEL_BLOCK) & (cmp_start + CMP_BLOCK > slc_start)).astype(np.float32)
    imp_slc = jnp.einsum('bgtn,ns->bgts', imp, jnp.asarray(overlap))
    tpos = q_off + np.arange(Tq)
    blk = np.arange(n_slc)[None, :]
    cur = (tpos // SEL_BLOCK)[:, None]
    valid = jnp.asarray(blk * SEL_BLOCK <= tpos[:, None])
    forced = jnp.asarray((blk == 0) | (blk == cur) | (blk == cur - 1))
    score = jnp.where(forced, FORCE, jnp.where(valid, imp_slc, -FORCE))
    _, idx = lax.top_k(score, min(N_SEL, n_slc))
    return idx


def selected_attention(q, ks, vs, idx, q_off, slopes):
    B, Tq = q.shape[:2]
    L = ks.shape[1]
    n_slc = -(-L // SEL_BLOCK)
    pad = ((0, 0), (0, n_slc * SEL_BLOCK - L), (0, 0), (0, 0))

    def to_blocks(a):
        a = jnp.pad(a, pad).reshape(B, n_slc, SEL_BLOCK, NSA_KV_HEADS, HEAD_DIM)
        return a.transpose(0, 3, 1, 2, 4)

    kb, vb = to_blocks(ks), to_blocks(vs)
    qb = min(Q_BLOCK, Tq)
    nqb = Tq // qb
    tpos = (q_off + jnp.arange(Tq)).reshape(nqb, qb)
    g_ix = jnp.arange(NSA_KV_HEADS)[:, None, None]
    offs = jnp.arange(SEL_BLOCK)
    n_keys = idx.shape[-1] * SEL_BLOCK

    def one_seq(args):
        q1, i1, k1, v1 = args

        def one_block(bargs):
            qx, ix, tx = bargs
            kg = k1[g_ix, ix].reshape(NSA_KV_HEADS, qb, n_keys, HEAD_DIM)
            vg = v1[g_ix, ix].reshape(NSA_KV_HEADS, qb, n_keys, HEAD_DIM)
            kpos = (ix[..., None] * SEL_BLOCK + offs).reshape(NSA_KV_HEADS, qb, n_keys)
            dist = (tx[None, :, None] - kpos).astype(jnp.float32)
            logits = jnp.einsum('tgrd,gtnd->grtn', qx, kg) * SCALE - slopes[:, :, None, None] * dist[:, None]
            p = masked_softmax(logits, dist[:, None] >= 0)
            return jnp.einsum('grtn,gtnd->tgrd', p.astype(vg.dtype), vg)

        o = lax.map(one_block, (q1.reshape(nqb, qb, NSA_KV_HEADS, NSA_GROUP, HEAD_DIM),
                                i1.reshape(NSA_KV_HEADS, nqb, qb, -1).transpose(1, 0, 2, 3), tpos))
        return o.reshape(Tq, NSA_KV_HEADS, NSA_GROUP, HEAD_DIM)

    return lax.map(one_seq, (q, idx, kb, vb))


def window_attention(q, kw, vw, q_off, slopes):
    B, Tq = q.shape[:2]
    qb = min(Q_BLOCK, Tq)
    nqb = Tq // qb
    span = WINDOW + qb

    def one_block(args):
        i, qx = args
        kx = lax.dynamic_slice_in_dim(kw, i * qb, span, axis=1)
        vx = lax.dynamic_slice_in_dim(vw, i * qb, span, axis=1)
        tpos = q_off + i * qb + jnp.arange(qb)
        kpos = q_off - WINDOW + i * qb + jnp.arange(span)
        d = tpos[:, None] - kpos[None, :]
        mask = (d >= 0) & (d < WINDOW) & (kpos[None, :] >= 0)
        logits = jnp.einsum('btgrd,bsgd->bgrts', qx, kx) * SCALE - slopes[:, :, None, None] * d.astype(jnp.float32)
        p = masked_softmax(logits, mask)
        return jnp.einsum('bgrts,bsgd->btgrd', p.astype(vx.dtype), vx)

    qs = jnp.moveaxis(q.reshape(B, nqb, qb, NSA_KV_HEADS, NSA_GROUP, HEAD_DIM), 1, 0)
    o = lax.map(one_block, (jnp.arange(nqb), qs))
    return jnp.moveaxis(o, 0, 1).reshape(B, Tq, NSA_KV_HEADS, NSA_GROUP, HEAD_DIM)


def forgetting_attention(q, k, v, logf, q_off):
    B, Tq = q.shape[:2]
    L = k.shape[1]
    c_k = jnp.cumsum(logf.astype(jnp.float32), axis=1).transpose(0, 2, 1)
    c_q = c_k[:, :, q_off:]
    qb = min(Q_BLOCK, Tq)
    nqb = Tq // qb
    kpos = jnp.arange(L)

    def one_block(args):
        i, qx, cx = args
        tpos = q_off + i * qb + jnp.arange(qb)
        logits = jnp.einsum('bthd,bshd->bhts', qx, k) * SCALE + (cx[..., None] - c_k[:, :, None, :])
        p = masked_softmax(logits, kpos[None, :] <= tpos[:, None])
        return jnp.einsum('bhts,bshd->bthd', p.astype(v.dtype), v)

    qs = jnp.moveaxis(q.reshape(B, nqb, qb, FOX_HEADS, HEAD_DIM), 1, 0)
    cs = jnp.moveaxis(c_q.reshape(B, FOX_HEADS, nqb, qb), 2, 0)
    o = lax.map(one_block, (jnp.arange(nqb), qs, cs))
    return jnp.moveaxis(o, 0, 1).reshape(B, Tq, FOX_HEADS, HEAD_DIM)


def prompt_context(nsa_rows, win_rows, fox_rows, logf_rows):
    B, T = nsa_rows.shape[:2]
    pad = jnp.zeros((B, WINDOW) + win_rows.shape[2:], win_rows.dtype)
    band = jnp.concatenate([pad, win_rows], axis=1)
    state = (nsa_rows, fox_rows, logf_rows, win_rows[:, T - min(WINDOW, T):])
    return (nsa_rows, band, fox_rows, logf_rows, 0), state


def sample_context(nsa_rows, win_rows, fox_rows, logf_rows, nsa_pool, fox_pool, logf_pool, win_buf, page_table):
    B, T = nsa_rows.shape[:2]
    nsa_full = jnp.concatenate([gather_pages(nsa_pool, page_table), nsa_rows], axis=1)
    fox_full = jnp.concatenate([gather_pages(fox_pool, page_table), fox_rows], axis=1)
    logf_full = jnp.concatenate([gather_pages(logf_pool, page_table).astype(logf_rows.dtype), logf_rows], axis=1)
    wb = win_buf.shape[1]
    win_all = jnp.concatenate([win_buf, win_rows], axis=1)
    pad = jnp.zeros((B, WINDOW - wb) + win_rows.shape[2:], win_rows.dtype)
    band = jnp.concatenate([pad, win_all], axis=1)
    q_off = nsa_full.shape[1] - T
    state = (nsa_rows, fox_rows, logf_rows, win_all[:, T:])
    return (nsa_full, band, fox_full, logf_full, q_off), state


def token_mixer(h, w, context_fn):
    B, T, _ = h.shape
    z = h @ w['w_in'] + w['b_in']
    q_a, kv_a, g_a, qkv_b, f_b, g_m = jnp.split(z, IN_OFFSETS, axis=-1)
    q_a = q_a.reshape(B, T, NSA_KV_HEADS, NSA_GROUP, HEAD_DIM)
    kv_a = kv_a.reshape(B, T, 6, NSA_KV_HEADS, HEAD_DIM)
    g_a = jax.nn.sigmoid(g_a).reshape(B, T, 3, NSA_KV_HEADS, NSA_GROUP, 1)
    qkv_b = qkv_b.reshape(B, T, 3, FOX_HEADS, HEAD_DIM)
    logf = jax.nn.log_sigmoid(f_b.astype(jnp.float32))
    g_m = jax.nn.sigmoid(g_m).reshape(B, T, 2, D_MODEL)
    (nsa_full, band, fox_full, logf_full, q_off), state = context_fn(
        kv_a[:, :, :4], kv_a[:, :, 4:], qkv_b[:, :, 1:], logf)
    slopes = alibi_slopes()
    kc = compress(nsa_full[:, :, 0], w['cmp_pe_k'], w['cmp_w1_k'], w['cmp_w2_k'])
    vc = compress(nsa_full[:, :, 1], w['cmp_pe_v'], w['cmp_w1_v'], w['cmp_w2_v'])
    o_cmp, imp = compressed_attention(q_a, kc, vc, q_off, slopes)
    idx = select_blocks(imp, nsa_full.shape[1], q_off)
    o_sel = selected_attention(q_a, nsa_full[:, :, 2], nsa_full[:, :, 3], idx, q_off, slopes)
    o_win = window_attention(q_a, band[:, :, 0], band[:, :, 1], q_off, slopes)
    o_a = g_a[:, :, 0] * o_cmp + g_a[:, :, 1] * o_sel + g_a[:, :, 2] * o_win
    o_b = forgetting_attention(qkv_b[:, :, 0], fox_full[:, :, 0], fox_full[:, :, 1], logf_full, q_off)
    merged = (g_m[:, :, 0] * (o_a.reshape(B, T, NSA_WIDTH) @ w['w_proj_a'])
              + g_m[:, :, 1] * (o_b.reshape(B, T, FOX_WIDTH) @ w['w_proj_b']))
    return merged @ w['w_out'], state


def decoder_layer(x, w, context_fn):
    h = layer_norm(ALPHA * x + 0.5 * swiglu(x, w['ffn1_w_gate'], w['ffn1_w_up'], w['ffn1_w_down']),
                   w['ln1_g'], w['ln1_b'])
    m, state = token_mixer(h, w, context_fn)
    h = layer_norm(ALPHA * h + m, w['ln2_g'], w['ln2_b'])
    h = layer_norm(ALPHA * h + 0.5 * swiglu(h, w['ffn2_w_gate'], w['ffn2_w_up'], w['ffn2_w_down']),
                   w['ln3_g'], w['ln3_b'])
    return h, state


def setup_inputs(seed: int = 0) -> dict:
    key = jax.random.key(seed)
    ks = jax.random.split(key, 40)
    n_pages = PAST_LEN // PAGE_SIZE
    n_used = DEC_BATCH * n_pages
    n_pool = n_used + n_used // 4
    win_len = min(WINDOW, PAST_LEN)

    def nrm(k, shape, scale=1.0):
        return scale * jax.random.normal(k, shape, jnp.float32)

    def gain(k):
        return 1.0 + nrm(k, (DEPTH, D_MODEL), 0.02)

    def bias(k, n):
        return nrm(k, (DEPTH, n), 0.02)

    b_in = bias(ks[13], IN_WIDTH).at[:, FORGET_COL:FORGET_COL + FOX_HEADS].add(FORGET_BIAS)
    return {
        'x_prompt': nrm(ks[0], (BATCH, SEQ, D_MODEL)),
        'x_sample': nrm(ks[1], (DEC_BATCH, DEC_SEQ, D_MODEL)),
        'cache_nsa_kv': nrm(ks[2], (DEPTH, n_pool, PAGE_SIZE, 4, NSA_KV_HEADS, HEAD_DIM)),
        'cache_fox_kv': nrm(ks[3], (DEPTH, n_pool, PAGE_SIZE, 2, FOX_HEADS, HEAD_DIM)),
        'cache_fox_logf': jax.nn.log_sigmoid(FORGET_BIAS + nrm(ks[4], (DEPTH, n_pool, PAGE_SIZE, FOX_HEADS), 0.5)),
        'state_win_kv': nrm(ks[5], (DEPTH, DEC_BATCH, win_len, 2, NSA_KV_HEADS, HEAD_DIM)),
        'page_table': jax.random.permutation(ks[6], n_pool)[:n_used].reshape(DEC_BATCH, n_pages).astype(jnp.int32),
        'ln1_g': gain(ks[7]),
        'ln1_b': bias(ks[8], D_MODEL),
        'ffn1_w_gate': nrm(ks[9], (DEPTH, D_MODEL, D_FF), D_MODEL ** -0.5),
        'ffn1_w_up': nrm(ks[10], (DEPTH, D_MODEL, D_FF), D_MODEL ** -0.5),
        'ffn1_w_down': nrm(ks[11], (DEPTH, D_FF, D_MODEL), BETA * D_FF ** -0.5),
        'w_in': nrm(ks[12], (DEPTH, D_MODEL, IN_WIDTH), D_MODEL ** -0.5),
        'b_in': b_in,
        'cmp_pe_k': nrm(ks[14], (DEPTH, CMP_BLOCK, HEAD_DIM), 0.1),
        'cmp_w1_k': nrm(ks[15], (DEPTH, CMP_BLOCK * HEAD_DIM, CMP_HIDDEN), (CMP_BLOCK * HEAD_DIM) ** -0.5),
        'cmp_w2_k': nrm(ks[16], (DEPTH, CMP_HIDDEN, HEAD_DIM), 1.5 * CMP_HIDDEN ** -0.5),
        'cmp_pe_v': nrm(ks[17], (DEPTH, CMP_BLOCK, HEAD_DIM), 0.1),
        'cmp_w1_v': nrm(ks[18], (DEPTH, CMP_BLOCK * HEAD_DIM, CMP_HIDDEN), (CMP_BLOCK * HEAD_DIM) ** -0.5),
        'cmp_w2_v': nrm(ks[19], (DEPTH, CMP_HIDDEN, HEAD_DIM), 1.5 * CMP_HIDDEN ** -0.5),
        'w_proj_a': nrm(ks[20], (DEPTH, NSA_WIDTH, D_MODEL), NSA_WIDTH ** -0.5),
        'w_proj_b': nrm(ks[21], (DEPTH, FOX_WIDTH, D_MODEL), FOX_WIDTH ** -0.5),
        'w_out': nrm(ks[22], (DEPTH, D_MODEL, D_MODEL), BETA * D_MODEL ** -0.5),
        'ln2_g': gain(ks[23]),
        'ln2_b': bias(ks[24], D_MODEL),
        'ffn2_w_gate': nrm(ks[25], (DEPTH, D_MODEL, D_FF), D_MODEL ** -0.5),
        'ffn2_w_up': nrm(ks[26], (DEPTH, D_MODEL, D_FF), D_MODEL ** -0.5),
        'ffn2_w_down': nrm(ks[27], (DEPTH, D_FF, D_MODEL), BETA * D_FF ** -0.5),
        'ln3_g': gain(ks[28]),
        'ln3_b': bias(ks[29], D_MODEL),
    }


def reference(x_prompt, x_sample, cache_nsa_kv, cache_fox_kv, cache_fox_logf, state_win_kv, page_table,
              ln1_g, ln1_b, ffn1_w_gate, ffn1_w_up, ffn1_w_down, w_in, b_in,
              cmp_pe_k, cmp_w1_k, cmp_w2_k, cmp_pe_v, cmp_w1_v, cmp_w2_v,
              w_proj_a, w_proj_b, w_out, ln2_g, ln2_b,
              ffn2_w_gate, ffn2_w_up, ffn2_w_down, ln3_g, ln3_b):
    h_p, h_s = x_prompt, x_sample
    st_p, st_s = [], []
    for l in range(DEPTH):
        w = dict(ln1_g=ln1_g[l], ln1_b=ln1_b[l], ffn1_w_gate=ffn1_w_gate[l], ffn1_w_up=ffn1_w_up[l],
                 ffn1_w_down=ffn1_w_down[l], w_in=w_in[l], b_in=b_in[l],
                 cmp_pe_k=cmp_pe_k[l], cmp_w1_k=cmp_w1_k[l], cmp_w2_k=cmp_w2_k[l],
                 cmp_pe_v=cmp_pe_v[l], cmp_w1_v=cmp_w1_v[l], cmp_w2_v=cmp_w2_v[l],
                 w_proj_a=w_proj_a[l], w_proj_b=w_proj_b[l], w_out=w_out[l], ln2_g=ln2_g[l], ln2_b=ln2_b[l],
                 ffn2_w_gate=ffn2_w_gate[l], ffn2_w_up=ffn2_w_up[l], ffn2_w_down=ffn2_w_down[l],
                 ln3_g=ln3_g[l], ln3_b=ln3_b[l])
        h_p, s_p = decoder_layer(h_p, w, prompt_context)
        ctx = functools.partial(sample_context, nsa_pool=cache_nsa_kv[l], fox_pool=cache_fox_kv[l],
                                logf_pool=cache_fox_logf[l], win_buf=state_win_kv[l], page_table=page_table)
        h_s, s_s = decoder_layer(h_s, w, ctx)
        st_p.append(s_p)
        st_s.append(s_s)
    nsa_kv_prompt = jnp.stack([s[0] for s in st_p])
    fox_kv_prompt = jnp.stack([s[1] for s in st_p])
    fox_logf_prompt = jnp.stack([s[2] for s in st_p])
    win_kv_prompt = jnp.stack([s[3] for s in st_p])
    nsa_kv_sample = jnp.stack([s[0] for s in st_s])
    fox_kv_sample = jnp.stack([s[1] for s in st_s])
    fox_logf_sample = jnp.stack([s[2] for s in st_s])
    win_kv_sample = jnp.stack([s[3] for s in st_s])
    return (h_p, h_s, nsa_kv_prompt, fox_kv_prompt, fox_logf_prompt, win_kv_prompt,
            nsa_kv_sample, fox_kv_sample, fox_logf_sample, win_kv_sample)
```

```python
import functools

import numpy as np
import jax
import jax.numpy as jnp
from jax import lax
from jax.experimental import pallas as pl
from jax.experimental.pallas import tpu as pltpu

F32 = jnp.float32
BF16 = jnp.bfloat16

LANES = 128
VMEM_LIMIT = 56 * 1024 * 1024

D_MODEL = 1024
HEAD_DIM = 64
NSA_HEADS = 8
NSA_KV_HEADS = 2
NSA_GROUP = NSA_HEADS // NSA_KV_HEADS
FOX_HEADS = 8
CMP_BLOCK = 32
CMP_STRIDE = 16
CMP_HIDDEN = 2 * HEAD_DIM
SEL_BLOCK = 64
N_SEL = 16
WINDOW = 512
PAGE = 128
LN_EPS = 1e-5
NEG = -1e30
FORCE = 1e9
SCALE = HEAD_DIM ** -0.5
MASK_BIG = 30000.0

SELBIT0 = 64
ALIBI0 = 96


def _dot(a, b):
    return jnp.dot(a, b, preferred_element_type=F32)


def _dot_nt(a, b):
    return lax.dot_general(a, b, (((1,), (1,)), ((), ())), preferred_element_type=F32)


def _split3(x):
    hi = x.astype(BF16)
    r1 = x - hi.astype(F32)
    mid = r1.astype(BF16)
    lo = (r1 - mid.astype(F32)).astype(BF16)
    return hi, mid, lo


def _dot3(x, w):
    hi, mid, lo = _split3(x)
    return _dot(hi, w) + _dot(mid, w) + _dot(lo, w)


def _dot3_nt(a, x):
    hi, mid, lo = _split3(x)
    return _dot_nt(a, hi) + _dot_nt(a, mid) + _dot_nt(a, lo)


def _sigmoid(x):
    return 1.0 / (1.0 + jnp.exp(-x))


def _log_sigmoid(x):
    return -(jnp.maximum(-x, 0.0) + jnp.log(1.0 + jnp.exp(-jnp.abs(x))))


def _gelu_tanh(x):
    c = np.float32(np.sqrt(2.0 / np.pi))
    return 0.5 * x * (1.0 + jnp.tanh(c * (x + np.float32(0.044715) * (x * x * x))))


def _cparams(sem):
    return pltpu.CompilerParams(dimension_semantics=sem, vmem_limit_bytes=VMEM_LIMIT)


def _ffn_ln_kernel(x_ref, wg_ref, wu_ref, wd_ref, g_ref, b_ref, o_ref, acc_ref, *, alpha, n_ff):
    j = pl.program_id(1)

    @pl.when(j == 0)
    def _():
        acc_ref[...] = jnp.zeros_like(acc_ref)

    xb = x_ref[...].astype(BF16)
    gate = _dot(xb, wg_ref[...])
    up = _dot(xb, wu_ref[...])
    mid = (gate * _sigmoid(gate) * up).astype(BF16)
    acc_ref[...] += _dot(mid, wd_ref[...])

    @pl.when(j == n_ff - 1)
    def _():
        y = alpha * x_ref[...] + 0.5 * acc_ref[...]
        mu = jnp.mean(y, axis=-1, keepdims=True)
        d = y - mu
        var = jnp.mean(d * d, axis=-1, keepdims=True)
        o_ref[...] = d * lax.rsqrt(var + LN_EPS) * g_ref[...] + b_ref[...]


def _ffn_ln(x, wg, wu, wd, g, b, alpha, tm):
    n, d = x.shape
    dff = wg.shape[1]
    tf = dff // 2
    n_ff = dff // tf
    return pl.pallas_call(
        functools.partial(_ffn_ln_kernel, alpha=alpha, n_ff=n_ff),
        grid=(n // tm, n_ff),
        in_specs=[
            pl.BlockSpec((tm, d), lambda i, j: (i, 0)),
            pl.BlockSpec((d, tf), lambda i, j: (0, j)),
            pl.BlockSpec((d, tf), lambda i, j: (0, j)),
            pl.BlockSpec((tf, d), lambda i, j: (j, 0)),
            pl.BlockSpec((1, d), lambda i, j: (0, 0)),
            pl.BlockSpec((1, d), lambda i, j: (0, 0)),
        ],
        out_specs=pl.BlockSpec((tm, d), lambda i, j: (i, 0)),
        out_shape=jax.ShapeDtypeStruct((n, d), F32),
        scratch_shapes=[pltpu.VMEM((tm, d), F32)],
        compiler_params=_cparams(("parallel", "arbitrary")),
        name="ffn_ln",
    )(x, wg, wu, wd, g, b)


C_QA = 0
C_KVA = 512
C_FOX = 1280
C_GM = 2816
C_GA = 4864
C_FB = 4992
C_TOT = 5120
N_QA_H = 8
N_KVA_H = 12
N_FOX_H = 24


def _head_pad(z, h):
    col = z[:, LANES * (h // 2):LANES * (h // 2 + 1)]
    if h % 2:
        col = pltpu.roll(col, HEAD_DIM, axis=1)
    lane = lax.broadcasted_iota(jnp.int32, col.shape, 1)
    return jnp.where(lane < HEAD_DIM, col, 0.0)


def _in_proj_kernel(h_ref, w_ref, b_ref, wft_ref, bft_ref,
                    nsa_ref, win_ref, fox_ref, logf_ref, logft_ref,
                    qa_ref, kva_ref, foxh_ref, ga_ref, gm_ref):
    hb = h_ref[...].astype(BF16)

    def proj(c0, c1):
        return _dot(hb, w_ref[:, c0:c1]) + b_ref[:, c0:c1]

    zq = proj(C_QA, C_KVA)
    for h in range(N_QA_H):
        qa_ref[h] = (_head_pad(zq, h) * SCALE).astype(BF16)

    zkv = proj(C_KVA, C_FOX)
    nsa_ref[...] = zkv[:, :512]
    win_ref[...] = zkv[:, 512:]
    for h in range(N_KVA_H):
        kva_ref[h] = _head_pad(zkv, h).astype(BF16)

    zf = proj(C_FOX, C_GM)
    fox_ref[...] = zf[:, 512:]
    for h in range(N_FOX_H):
        v = _head_pad(zf, h)
        if h < FOX_HEADS:
            v = v * SCALE
        foxh_ref[h] = v.astype(BF16)

    gm_ref[...] = _sigmoid(proj(C_GM, C_GA))
    ga_ref[...] = _sigmoid(proj(C_GA, C_FB))[:, :3 * NSA_HEADS]
    logf_ref[...] = _log_sigmoid(proj(C_FB, C_TOT))[:, :FOX_HEADS]
    logft_ref[...] = _log_sigmoid(_dot_nt(wft_ref[...], hb) + bft_ref[...])


def _in_proj(h, w, b, wft, bft, nseq, tseq, tm):
    n, d = h.shape
    tps = tseq // tm
    row = lambda i: (i, 0)
    headmaj = lambda i: (i // tps, 0, i % tps, 0)
    out_shape = [
        jax.ShapeDtypeStruct((n, 512), F32),
        jax.ShapeDtypeStruct((n, 256), F32),
        jax.ShapeDtypeStruct((n, 1024), F32),
        jax.ShapeDtypeStruct((n, FOX_HEADS), F32),
        jax.ShapeDtypeStruct((FOX_HEADS, n), F32),
        jax.ShapeDtypeStruct((nseq, N_QA_H, tseq, LANES), BF16),
        jax.ShapeDtypeStruct((nseq, N_KVA_H, tseq, LANES), BF16),
        jax.ShapeDtypeStruct((nseq, N_FOX_H, tseq, LANES), BF16),
        jax.ShapeDtypeStruct((n, 3 * NSA_HEADS), F32),
        jax.ShapeDtypeStruct((n, 2 * D_MODEL), F32),
    ]
    out_specs = [
        pl.BlockSpec((tm, 512), row),
        pl.BlockSpec((tm, 256), row),
        pl.BlockSpec((tm, 1024), row),
        pl.BlockSpec((tm, FOX_HEADS), row),
        pl.BlockSpec((FOX_HEADS, tm), lambda i: (0, i)),
        pl.BlockSpec((None, N_QA_H, tm, LANES), headmaj),
        pl.BlockSpec((None, N_KVA_H, tm, LANES), headmaj),
        pl.BlockSpec((None, N_FOX_H, tm, LANES), headmaj),
        pl.BlockSpec((tm, 3 * NSA_HEADS), row),
        pl.BlockSpec((tm, 2 * D_MODEL), row),
    ]
    return pl.pallas_call(
        _in_proj_kernel,
        grid=(n // tm,),
        in_specs=[
            pl.BlockSpec((tm, d), row),
            pl.BlockSpec((d, C_TOT), lambda i: (0, 0)),
            pl.BlockSpec((1, C_TOT), lambda i: (0, 0)),
            pl.BlockSpec((FOX_HEADS, d), lambda i: (0, 0)),
            pl.BlockSpec((FOX_HEADS, 1), lambda i: (0, 0)),
        ],
        out_specs=out_specs,
        out_shape=out_shape,
        compiler_params=_cparams(("parallel",)),
        name="in_proj",
    )(h, w, b, wft, bft)


def _cmp_parts(k_refs, v_refs, wr_ref, rows_per_ref):
    n_chunk = rows_per_ref // CMP_STRIDE
    acc = None
    for r in range(CMP_STRIDE):
        def rows(refs):
            xs = [ref[pl.ds(r, n_chunk, stride=CMP_STRIDE), :] for ref in refs]
            return xs[0] if len(xs) == 1 else jnp.concatenate(xs, axis=0)
        x = jnp.concatenate([rows(k_refs), rows(v_refs)], axis=1)
        p = _dot(x.astype(BF16), wr_ref[r])
        acc = p if acc is None else acc + p
    return acc


def _cmp_finish(parts, bias8, w2_ref, out_ref):
    n = parts.shape[0]
    for s in range(4):
        kind = s // 2
        pa = parts[:, 256 * s:256 * s + CMP_HIDDEN]
        pb = parts[:, 256 * s + CMP_HIDDEN:256 * (s + 1)]
        pb = pltpu.roll(pb, n - 1, axis=0)
        bias = bias8[kind:kind + 1, CMP_HIDDEN * kind:CMP_HIDDEN * (kind + 1)]
        hid = bias + pa + pb
        out_ref[s] = _dot(_gelu_tanh(hid).astype(BF16), w2_ref[kind]).astype(BF16)


def _cmp_bias(pe_ref, w1_ref):
    return _dot(pe_ref[...], w1_ref[...])


def _compress_prompt_kernel(k_ref, v_ref, wr_ref, pe_ref, w1_ref, w2_ref, out_ref, *, t):
    parts = _cmp_parts([k_ref], [v_ref], wr_ref, t)
    _cmp_finish(parts, _cmp_bias(pe_ref, w1_ref), w2_ref, out_ref)


def _compress_prompt(nsa_rows, cw, nb, t):
    n_chunk = t // CMP_STRIDE
    return pl.pallas_call(
        functools.partial(_compress_prompt_kernel, t=t),
        grid=(nb,),
        in_specs=[
            pl.BlockSpec((t, LANES), lambda b: (b, 0)),
            pl.BlockSpec((t, LANES), lambda b: (b, 1)),
            pl.BlockSpec((CMP_STRIDE, 256, 1024), lambda b: (0, 0, 0)),
            pl.BlockSpec((16, 2048), lambda b: (0, 0)),
            pl.BlockSpec((2048, 256), lambda b: (0, 0)),
            pl.BlockSpec((2, CMP_HIDDEN, LANES), lambda b: (0, 0, 0)),
        ],
        out_specs=pl.BlockSpec((None, 4, n_chunk, LANES), lambda b: (b, 0, 0, 0)),
        out_shape=jax.ShapeDtypeStruct((nb, 4, n_chunk, LANES), BF16),
        compiler_params=_cparams(("parallel",)),
        name="compress_prompt",
    )(nsa_rows, nsa_rows, cw["wr"], cw["pe8"], cw["w1cat"], cw["w2pad"])


def _compress_sample_kernel(pt_ref, *refs, n_pages, n_steps):
    k_refs = refs[:n_pages]
    v_refs = refs[n_pages:2 * n_pages]
    wr_ref, pe_ref, w1_ref, w2_ref, out_ref, parts_ref = refs[2 * n_pages:]
    pg = pl.program_id(1)
    rows = n_pages * (PAGE // CMP_STRIDE)
    parts = _cmp_parts(k_refs, v_refs, wr_ref, PAGE)
    parts_ref[pl.ds(pl.multiple_of(pg * rows, rows), rows), :] = parts

    @pl.when(pg == n_steps - 1)
    def _():
        _cmp_finish(parts_ref[...], _cmp_bias(pe_ref, w1_ref), w2_ref, out_ref)


def _compress_sample(pool_rows, page_table, cw, n_pages):
    nb, pages_per_seq = page_table.shape
    n_steps = pages_per_seq // n_pages
    n_chunk = pages_per_seq * (PAGE // CMP_STRIDE)
    page_specs = [
        pl.BlockSpec((None, PAGE, LANES), lambda b, pg, pt, j=j, c=c: (pt[b, pg * n_pages + j], 0, c))
        for c in range(2) for j in range(n_pages)
    ]
    grid_spec = pltpu.PrefetchScalarGridSpec(
        num_scalar_prefetch=1,
        grid=(nb, n_steps),
        in_specs=page_specs + [
            pl.BlockSpec((CMP_STRIDE, 256, 1024), lambda b, pg, pt: (0, 0, 0)),
            pl.BlockSpec((16, 2048), lambda b, pg, pt: (0, 0)),
            pl.BlockSpec((2048, 256), lambda b, pg, pt: (0, 0)),
            pl.BlockSpec((2, CMP_HIDDEN, LANES), lambda b, pg, pt: (0, 0, 0)),
        ],
        out_specs=pl.BlockSpec((None, 4, n_chunk, LANES), lambda b, pg, pt: (b, 0, 0, 0)),
        scratch_shapes=[pltpu.VMEM((n_chunk, 1024), F32)],
    )
    return pl.pallas_call(
        functools.partial(_compress_sample_kernel, n_pages=n_pages, n_steps=n_steps),
        grid_spec=grid_spec,
        out_shape=jax.ShapeDtypeStruct((nb, 4, n_chunk, LANES), BF16),
        compiler_params=_cparams(("parallel", "arbitrary")),
        name="compress_sample",
    )(page_table, *([pool_rows] * (2 * n_pages)), cw["wr"], cw["pe8"], cw["w1cat"], cw["w2pad"])


def _softmax_update(s, v, m_ref, l_ref, acc_ref):
    m_prev = m_ref[...]
    m_new = jnp.maximum(m_prev, jnp.max(s, axis=1, keepdims=True))
    a = jnp.exp(m_prev - m_new)
    p = jnp.exp(s - m_new)
    l_ref[...] = a * l_ref[...] + jnp.sum(p, axis=1, keepdims=True)
    acc_ref[...] = a * acc_ref[...] + _dot(p.astype(BF16), v)
    m_ref[...] = m_new


def _nsa_prompt_kernel(q_ref, selk_ref, selv_ref, wink_ref, winv_ref, kc_ref, vc_ref, ga_ref,
                       kaug_sel_ref, kaug_win_ref, kaug_cmp_ref, ovl_ref,
                       o_ref, ksel_s, kwin_s, m_ref, l_ref, acc_ref, *, tq, tk, n_cmp):
    g = pl.program_id(1)
    qt = pl.program_id(2)
    rows = NSA_GROUP * tq

    @pl.when(qt == 0)
    def _():
        ksel_s[...] = selk_ref[...] + kaug_sel_ref[...]
        kwin_s[...] = wink_ref[...] + kaug_win_ref[...]

    q4 = q_ref[...].reshape(rows, LANES)

    row = lax.broadcasted_iota(jnp.int32, (rows, LANES), 0)
    lane = lax.broadcasted_iota(jnp.int32, (rows, LANES), 1)
    r = row >> (tq.bit_length() - 1)
    t = qt * tq + (row & (tq - 1))
    sl0 = jnp.where(r == 0, 0.5, jnp.where(r == 1, 0.25, jnp.where(r == 2, 0.125, 0.0625))).astype(F32)
    slope = jnp.where(g == 0, sl0, sl0 * 0.0625)
    t_hi = ((t >> 7) << 7).astype(F32)
    t_lo = (t & 127).astype(F32)
    al = jnp.where(lane == ALIBI0, -slope * t_hi,
                   jnp.where(lane == ALIBI0 + 1, -slope * t_lo,
                             jnp.where((lane == ALIBI0 + 2) | (lane == ALIBI0 + 3), slope, 0.0)))
    qa = q4 + al.astype(BF16)

    kc = kc_ref[...] + kaug_cmp_ref[...]
    s = _dot_nt(qa, kc)
    valid = (t >= lane * CMP_STRIDE + (CMP_BLOCK - 1)) & (lane < n_cmp)
    s = jnp.where(valid, s, NEG)
    m = jnp.max(s, axis=1, keepdims=True)
    p = jnp.where(valid, jnp.exp(s - m), 0.0)
    den = jnp.sum(p, axis=1, keepdims=True)
    p = p / jnp.where(den > 0.0, den, 1.0)
    o_cmp = _dot(p.astype(BF16), vc_ref[...])

    imp = p[0:tq] + p[tq:2 * tq] + p[2 * tq:3 * tq] + p[3 * tq:4 * tq]
    imp_slc = _dot3(imp, ovl_ref[...])
    lane1 = lax.broadcasted_iota(jnp.int32, (tq, LANES), 1)
    t1 = qt * tq + lax.broadcasted_iota(jnp.int32, (tq, LANES), 0)
    n_slc = 32
    blk = lane1 - SELBIT0
    inb = (blk >= 0) & (blk < n_slc)
    cur = t1 >> 6
    forced = (blk == 0) | (blk == cur) | (blk == cur - 1)
    score = jnp.where(forced, FORCE, jnp.where(blk * SEL_BLOCK <= t1, imp_slc, -FORCE))
    score = jnp.where(inb, score, -3.0 * FORCE)
    rank = jnp.zeros((tq, LANES), F32)
    for i in range(n_slc):
        col = score[:, SELBIT0 + i:SELBIT0 + i + 1]
        beats = (col > score) | ((col == score) & (blk > i))
        rank = rank + jnp.where(beats, 1.0, 0.0)
    notsel = jnp.where(inb & (rank >= float(N_SEL)), 1.0, 0.0).astype(BF16)
    qs = qa + jnp.concatenate([notsel] * NSA_GROUP, axis=0)

    col_iota = lax.broadcasted_iota(jnp.int32, (rows, tk), 1)
    t_rows = qt * tq + (lax.broadcasted_iota(jnp.int32, (rows, tk), 0) & (tq - 1))

    def reset():
        m_ref[...] = jnp.full_like(m_ref, NEG)
        l_ref[...] = jnp.zeros_like(l_ref)
        acc_ref[...] = jnp.zeros_like(acc_ref)

    reset()

    def sel_body(kt, carry):
        k0 = pl.multiple_of(kt * tk, tk)
        sc = _dot_nt(qs, ksel_s[pl.ds(k0, tk), :])
        sc = jnp.where(k0 + col_iota <= t_rows, sc, NEG)
        _softmax_update(sc, selv_ref[pl.ds(k0, tk), :], m_ref, l_ref, acc_ref)
        return carry

    n_kt = (qt * tq + tq + tk - 1) // tk
    lax.fori_loop(0, n_kt, sel_body, 0)
    o_sel = acc_ref[...] / l_ref[...]

    reset()

    def win_body(kt, carry):
        k0 = pl.multiple_of(kt * tk, tk)
        sc = _dot_nt(qa, kwin_s[pl.ds(k0, tk), :])
        dist = t_rows - (k0 + col_iota)
        sc = jnp.where((dist >= 0) & (dist < WINDOW), sc, NEG)
        _softmax_update(sc, winv_ref[pl.ds(k0, tk), :], m_ref, l_ref, acc_ref)
        return carry

    kt_lo = jnp.maximum(qt * tq - WINDOW, 0) // tk
    lax.fori_loop(kt_lo, n_kt, win_body, 0)
    o_win = acc_ref[...] / l_ref[...]

    ga = ga_ref[...]
    outs = []
    for rr in range(NSA_GROUP):
        def gate(br):
            c = br * NSA_HEADS + rr
            return jnp.where(g == 0, ga[:, c:c + 1], ga[:, c + NSA_GROUP:c + NSA_GROUP + 1])
        sl = slice(rr * tq, (rr + 1) * tq)
        outs.append(gate(0) * o_cmp[sl] + gate(1) * o_sel[sl] + gate(2) * o_win[sl])
    lo = outs[0] + pltpu.roll(outs[1], HEAD_DIM, axis=1)
    hi = outs[2] + pltpu.roll(outs[3], HEAD_DIM, axis=1)
    o_ref[...] = jnp.concatenate([lo, hi], axis=1).astype(BF16)


def _nsa_prompt(qa_h, kva_h, kcvc, ga, consts, nb, t, tq, tk):
    n_cmp = (t - CMP_BLOCK) // CMP_STRIDE + 1
    n_chunk = t // CMP_STRIDE
    q5 = qa_h.reshape(nb, NSA_KV_HEADS, NSA_GROUP, t, LANES)
    ga3 = ga.reshape(nb, t, 3 * NSA_HEADS)
    kv_spec = lambda off: pl.BlockSpec((None, None, t, LANES), lambda b, g, qt: (b, off + g, 0, 0))
    const2 = lambda shape: pl.BlockSpec(shape, lambda b, g, qt: (0, 0))
    rows = NSA_GROUP * tq
    return pl.pallas_call(
        functools.partial(_nsa_prompt_kernel, tq=tq, tk=tk, n_cmp=n_cmp),
        grid=(nb, NSA_KV_HEADS, t // tq),
        in_specs=[
            pl.BlockSpec((None, None, NSA_GROUP, tq, LANES), lambda b, g, qt: (b, g, 0, qt, 0)),
            kv_spec(4), kv_spec(6), kv_spec(8), kv_spec(10),
            pl.BlockSpec((None, None, n_chunk, LANES), lambda b, g, qt: (b, g, 0, 0)),
            pl.BlockSpec((None, None, n_chunk, LANES), lambda b, g, qt: (b, 2 + g, 0, 0)),
            pl.BlockSpec((None, tq, 3 * NSA_HEADS), lambda b, g, qt: (b, qt, 0)),
            const2((t, LANES)), const2((t, LANES)), const2((n_chunk, LANES)), const2((n_chunk, LANES)),
        ],
        out_specs=pl.BlockSpec((None, tq, NSA_GROUP * HEAD_DIM), lambda b, g, qt: (b, qt, g)),
        out_shape=jax.ShapeDtypeStruct((nb, t, NSA_HEADS * HEAD_DIM), BF16),
        scratch_shapes=[
            pltpu.VMEM((t, LANES), BF16), pltpu.VMEM((t, LANES), BF16),
            pltpu.VMEM((rows, 1), F32), pltpu.VMEM((rows, 1), F32), pltpu.VMEM((rows, LANES), F32),
        ],
        compiler_params=_cparams(("parallel", "parallel", "arbitrary")),
        name="nsa_prompt",
    )(q5, kva_h, kva_h, kva_h, kva_h, kcvc, kcvc, ga3,
      consts["kaug_sel"], consts["kaug_win"], consts["kaug_cmp"], consts["ovl"])


def _prefix_rows(x, tri):
    n_blk = x.shape[1] // LANES
    carry = jnp.zeros((x.shape[0], 1), F32)
    outs = []
    for i in range(n_blk):
        c = _dot3(x[:, i * LANES:(i + 1) * LANES], tri) + carry
        outs.append(c)
        carry = c[:, LANES - 1:LANES]
    return jnp.concatenate(outs, axis=1)


def _fox_prompt_kernel(q_ref, k_ref, v_ref, lft_ref, tri_ref, o_ref, c_s, m_ref, l_ref, acc_ref, *, tq, tk):
    hp = pl.program_id(1)
    qt = pl.program_id(2)

    @pl.when((hp == 0) & (qt == 0))
    def _():
        c_s[...] = _prefix_rows(lft_ref[...], tri_ref[...])

    col_iota = lax.broadcasted_iota(jnp.int32, (tq, tk), 1)
    t_rows = qt * tq + lax.broadcasted_iota(jnp.int32, (tq, tk), 0)
    n_kt = (qt * tq + tq + tk - 1) // tk
    outs = []
    for i in range(2):
        q = q_ref[i]
        m_ref[...] = jnp.full_like(m_ref, NEG)
        l_ref[...] = jnp.zeros_like(l_ref)
        acc_ref[...] = jnp.zeros_like(acc_ref)

        def body(kt, carry):
            k0 = pl.multiple_of(kt * tk, tk)
            sc = _dot_nt(q, k_ref[i, pl.ds(k0, tk), :])
            sc = sc - c_s[pl.ds(2 * hp + i, 1), pl.ds(k0, tk)]
            sc = jnp.where(k0 + col_iota <= t_rows, sc, NEG)
            _softmax_update(sc, v_ref[i, pl.ds(k0, tk), :], m_ref, l_ref, acc_ref)
            return carry

        lax.fori_loop(0, n_kt, body, 0)
        outs.append(acc_ref[...] / l_ref[...])
    o_ref[...] = (outs[0] + pltpu.roll(outs[1], HEAD_DIM, axis=1)).astype(BF16)


def _fox_prompt(fox_h, logft, tri, nb, t, tq, tk):
    f6 = fox_h.reshape(nb, 3, FOX_HEADS // 2, 2, t, LANES)
    return pl.pallas_call(
        functools.partial(_fox_prompt_kernel, tq=tq, tk=tk),
        grid=(nb, FOX_HEADS // 2, t // tq),
        in_specs=[
            pl.BlockSpec((None, None, None, 2, tq, LANES), lambda b, hp, qt: (b, 0, hp, 0, qt, 0)),
            pl.BlockSpec((None, None, None, 2, t, LANES), lambda b, hp, qt: (b, 1, hp, 0, 0, 0)),
            pl.BlockSpec((None, None, None, 2, t, LANES), lambda b, hp, qt: (b, 2, hp, 0, 0, 0)),
            pl.BlockSpec((FOX_HEADS, t), lambda b, hp, qt: (0, b)),
            pl.BlockSpec((LANES, LANES), lambda b, hp, qt: (0, 0)),
        ],
        out_specs=pl.BlockSpec((None, tq, LANES), lambda b, hp, qt: (b, qt, hp)),
        out_shape=jax.ShapeDtypeStruct((nb, t, FOX_HEADS * HEAD_DIM), BF16),
        scratch_shapes=[
            pltpu.VMEM((FOX_HEADS, t), F32),
            pltpu.VMEM((tq, 1), F32), pltpu.VMEM((tq, 1), F32), pltpu.VMEM((tq, LANES), F32),
        ],
        compiler_params=_cparams(("parallel", "arbitrary", "arbitrary")),
        name="fox_prompt",
    )(f6, f6, f6, logft, tri)


def _merge_ln_kernel(oa_ref, ob_ref, gm_ref, h_ref, wa_ref, wb_ref, wo_ref, g_ref, b_ref, o_ref, *, alpha):
    pa = _dot(oa_ref[...], wa_ref[...])
    pb = _dot(ob_ref[...], wb_ref[...])
    merged = gm_ref[:, :D_MODEL] * pa + gm_ref[:, D_MODEL:] * pb
    y = alpha * h_ref[...] + _dot(merged.astype(BF16), wo_ref[...])
    mu = jnp.mean(y, axis=-1, keepdims=True)
    d = y - mu
    var = jnp.mean(d * d, axis=-1, keepdims=True)
    o_ref[...] = d * lax.rsqrt(var + LN_EPS) * g_ref[...] + b_ref[...]


def _merge_ln(oa, ob, gm, h, wa, wb, wo, g, b, alpha, tm):
    n, d = h.shape
    row = lambda i: (i, 0)
    fixed = lambda i: (0, 0)
    return pl.pallas_call(
        functools.partial(_merge_ln_kernel, alpha=alpha),
        grid=(n // tm,),
        in_specs=[
            pl.BlockSpec((tm, 512), row), pl.BlockSpec((tm, 512), row),
            pl.BlockSpec((tm, 2 * d), row), pl.BlockSpec((tm, d), row),
            pl.BlockSpec((512, d), fixed), pl.BlockSpec((512, d), fixed), pl.BlockSpec((d, d), fixed),
            pl.BlockSpec((1, d), fixed), pl.BlockSpec((1, d), fixed),
        ],
        out_specs=pl.BlockSpec((tm, d), row),
        out_shape=jax.ShapeDtypeStruct((n, d), F32),
        compiler_params=_cparams(("parallel",)),
        name="merge_ln",
    )(oa, ob, gm, h, wa, wb, wo, g, b)


SAMPLE_ROWS = 32


def _row_slopes(rows, width, head0=0):
    row = lax.broadcasted_iota(jnp.int32, (rows, width), 0)
    head = head0 + (row >> 2)
    slope = lax.bitcast_convert_type((126 - head) << 23, F32)
    return slope, row & 3


def _sample_select_kernel(q_ref, kc_ref, vc_ref, ovl_ref, ocmp_ref, sel_ref, *, q_off, n_cmp, n_slc, t_new):
    n_key = kc_ref.shape[1]
    hrows = SAMPLE_ROWS // NSA_KV_HEADS
    lane = lax.broadcasted_iota(jnp.int32, (hrows, n_key), 1)
    ps, os_ = [], []
    for g in range(NSA_KV_HEADS):
        slope, tok = _row_slopes(hrows, n_key, head0=g * NSA_GROUP)
        dist = (q_off + tok) - (lane * CMP_STRIDE + (CMP_BLOCK - 1))
        valid = (dist >= 0) & (lane < n_cmp)
        s = _dot_nt(q_ref[g * hrows:(g + 1) * hrows], kc_ref[g]) - slope * dist.astype(F32)
        s = jnp.where(valid, s, NEG)
        m = jnp.max(s, axis=1, keepdims=True)
        p = jnp.where(valid, jnp.exp(s - m), 0.0)
        den = jnp.sum(p, axis=1, keepdims=True)
        p = p / jnp.where(den > 0.0, den, 1.0)
        ps.append(p)
        os_.append(_dot(p.astype(BF16), vc_ref[g]))
    ocmp_ref[...] = jnp.concatenate(os_, axis=0)

    p_all = jnp.concatenate(ps, axis=0)
    rr = lax.broadcasted_iota(jnp.int32, (2 * 8, SAMPLE_ROWS), 0)
    cc = lax.broadcasted_iota(jnp.int32, (2 * 8, SAMPLE_ROWS), 1)
    gather = jnp.where(((rr >> 3) == (cc >> 4)) & ((rr & 7) == (cc & 3)), 1.0, 0.0).astype(BF16)
    hi, mid, lo = _split3(p_all)
    imp = _dot(gather, hi) + _dot(gather, mid) + _dot(gather, lo)
    imp_slc = _dot3(imp, ovl_ref[...])
    nb_pad = imp_slc.shape[1]
    blk = lax.broadcasted_iota(jnp.int32, (16, nb_pad), 1)
    blk_f = blk.astype(F32)
    tpos = q_off + (lax.broadcasted_iota(jnp.int32, (16, nb_pad), 0) & 7)
    cur = tpos >> 6
    forced = (blk == 0) | (blk == cur) | (blk == cur - 1)
    score = jnp.where(forced, FORCE, jnp.where(blk * SEL_BLOCK <= tpos, imp_slc, -FORCE))
    score = jnp.where(blk < n_slc, score, -3.0 * FORCE)
    sel = jnp.zeros((16, nb_pad), F32)
    for _ in range(min(N_SEL, n_slc)):
        mx = jnp.max(score, axis=1, keepdims=True)
        first = jnp.min(jnp.where(score == mx, blk_f, float(nb_pad)), axis=1, keepdims=True)
        hit = blk_f == first
        sel = jnp.where(hit, 1.0, sel)
        score = jnp.where(hit, -4.0 * FORCE, score)
    sel_ref[...] = sel


def _sample_select(q32, kcvc, ovl, q_off, n_cmp, n_slc, t_new):
    nb = q32.shape[0]
    n_key = kcvc.shape[2]
    nb_pad = ovl.shape[1]
    return pl.pallas_call(
        functools.partial(_sample_select_kernel, q_off=q_off, n_cmp=n_cmp, n_slc=n_slc, t_new=t_new),
        grid=(nb,),
        in_specs=[
            pl.BlockSpec((None, SAMPLE_ROWS, LANES), lambda b: (b, 0, 0)),
            pl.BlockSpec((None, None, 2, n_key, LANES), lambda b: (b, 0, 0, 0, 0)),
            pl.BlockSpec((None, None, 2, n_key, LANES), lambda b: (b, 1, 0, 0, 0)),
            pl.BlockSpec((n_key, nb_pad), lambda b: (0, 0)),
        ],
        out_specs=[
            pl.BlockSpec((None, SAMPLE_ROWS, LANES), lambda b: (b, 0, 0)),
            pl.BlockSpec((None, 16, nb_pad), lambda b: (b, 0, 0)),
        ],
        out_shape=[
            jax.ShapeDtypeStruct((nb, SAMPLE_ROWS, LANES), F32),
            jax.ShapeDtypeStruct((nb, 16, nb_pad), F32),
        ],
        compiler_params=_cparams(("parallel",)),
        name="sample_select",
    )(q32, kcvc.reshape(nb, 2, 2, n_key, LANES), kcvc.reshape(nb, 2, 2, n_key, LANES), ovl)


TAIL = 128


def _tail_scores(q, knew, t_new):
    s = _dot_nt(q, knew)
    slope, tok = _row_slopes(SAMPLE_ROWS, TAIL)
    j = lax.broadcasted_iota(jnp.int32, s.shape, 1)
    d = tok - j
    s = s - slope * d.astype(F32)
    return jnp.where((d >= 0) & (j < t_new), s, NEG)


def _sample_nsa_kernel(pt_ref, *refs, n_pages, n_steps, q_off, t_new):
    page_refs = refs[:n_pages]
    (q_ref, selx_ref, exp_ref, knew_ref, vnew_ref, wbuf_ref, wknew_ref, wvnew_ref, ocmp_ref, gate_ref,
     o_ref, m_ref, l_ref, acc_ref) = refs[n_pages:]
    pg = pl.program_id(1)
    n_key = n_pages * PAGE
    q = q_ref[...]
    slope, tok = _row_slopes(SAMPLE_ROWS, n_key)

    @pl.when(pg == 0)
    def _():
        m_ref[...] = jnp.full_like(m_ref, NEG)
        l_ref[...] = jnp.zeros_like(l_ref)
        acc_ref[...] = jnp.zeros_like(acc_ref)

    kt = jnp.concatenate([r[:, 0:LANES] for r in page_refs], axis=0).astype(BF16)
    vt = jnp.concatenate([r[:, LANES:2 * LANES] for r in page_refs], axis=0).astype(BF16)
    s = _dot_nt(q, kt)
    kpos = pg * n_key + lax.broadcasted_iota(jnp.int32, (SAMPLE_ROWS, n_key), 1)
    dist = (q_off + tok) - kpos
    s = s - slope * dist.astype(F32)
    chosen = _dot(selx_ref[...], exp_ref[...]) > 0.5
    s = jnp.where(chosen & (dist >= 0), s, NEG)
    _softmax_update(s, vt, m_ref, l_ref, acc_ref)

    @pl.when(pg == n_steps - 1)
    def _():
        lane = lax.broadcasted_iota(jnp.int32, (SAMPLE_ROWS, LANES), 1)
        row = lax.broadcasted_iota(jnp.int32, (SAMPLE_ROWS, LANES), 0)

        def own_group(x):
            return jnp.where(lane < HEAD_DIM, jnp.where(row < 16, x, pltpu.roll(x, HEAD_DIM, axis=1)), 0.0)

        st = _tail_scores(q, knew_ref[...], t_new)
        _softmax_update(st, vnew_ref[...], m_ref, l_ref, acc_ref)
        o_sel = own_group(acc_ref[...] / l_ref[...])

        wb = wbuf_ref[...]
        wk = wb[:, 0:LANES].astype(BF16)
        wv = wb[:, LANES:2 * LANES].astype(BF16)
        sw = _dot_nt(q, wk)
        wslope, wtok = _row_slopes(SAMPLE_ROWS, WINDOW)
        wpos = (q_off - WINDOW) + lax.broadcasted_iota(jnp.int32, (SAMPLE_ROWS, WINDOW), 1)
        wd = (q_off + wtok) - wpos
        sw = sw - wslope * wd.astype(F32)
        sw = jnp.where((wd >= 0) & (wd < WINDOW) & (wpos >= 0), sw, NEG)
        swt = _tail_scores(q, wknew_ref[...], t_new)
        mw = jnp.maximum(jnp.max(sw, axis=1, keepdims=True), jnp.max(swt, axis=1, keepdims=True))
        pw = jnp.exp(sw - mw)
        pwt = jnp.exp(swt - mw)
        lw = jnp.sum(pw, axis=1, keepdims=True) + jnp.sum(pwt, axis=1, keepdims=True)
        ow = _dot(pw.astype(BF16), wv) + _dot(pwt.astype(BF16), wvnew_ref[...])
        o_win = own_group(ow / lw)

        gate = gate_ref[...]
        o_ref[...] = gate[:, 0:1] * ocmp_ref[...] + gate[:, 1:2] * o_sel + gate[:, 2:3] * o_win


def _sample_nsa(pool_rows, page_table, q32, selx, expand, knew, vnew, wbuf, wknew, wvnew, ocmp, gate32,
                n_pages, q_off, t_new):
    nb, pages_per_seq = page_table.shape
    n_steps = pages_per_seq // n_pages
    page_specs = [
        pl.BlockSpec((None, PAGE, 256), lambda b, pg, pt, j=j: (pt[b, pg * n_pages + j], 0, 1))
        for j in range(n_pages)
    ]
    per_b = lambda shape: pl.BlockSpec((None,) + shape, lambda b, pg, pt: (b,) + (0,) * len(shape))
    grid_spec = pltpu.PrefetchScalarGridSpec(
        num_scalar_prefetch=1,
        grid=(nb, n_steps),
        in_specs=page_specs + [
            per_b((SAMPLE_ROWS, LANES)),
            pl.BlockSpec((None, None, SAMPLE_ROWS, LANES), lambda b, pg, pt: (b, pg, 0, 0)),
            pl.BlockSpec((LANES, n_pages * PAGE), lambda b, pg, pt: (0, 0)),
            per_b((TAIL, LANES)), per_b((TAIL, LANES)),
            per_b((WINDOW, 256)),
            per_b((TAIL, LANES)), per_b((TAIL, LANES)),
            per_b((SAMPLE_ROWS, LANES)), per_b((SAMPLE_ROWS, LANES)),
        ],
        out_specs=per_b((SAMPLE_ROWS, LANES)),
        scratch_shapes=[
            pltpu.VMEM((SAMPLE_ROWS, 1), F32), pltpu.VMEM((SAMPLE_ROWS, 1), F32),
            pltpu.VMEM((SAMPLE_ROWS, LANES), F32),
        ],
    )
    return pl.pallas_call(
        functools.partial(_sample_nsa_kernel, n_pages=n_pages, n_steps=n_steps, q_off=q_off, t_new=t_new),
        grid_spec=grid_spec,
        out_shape=jax.ShapeDtypeStruct((nb, SAMPLE_ROWS, LANES), F32),
        compiler_params=_cparams(("parallel", "arbitrary")),
        name="sample_nsa",
    )(page_table, *([pool_rows] * n_pages), q32, selx, expand, knew, vnew, wbuf, wknew, wvnew, ocmp, gate32)


def _sample_fox_kernel(pt_ref, *refs, n_pages, n_steps, t_new):
    kv_refs = refs[:n_pages]
    lf_refs = refs[n_pages:2 * n_pages]
    (q_ref, tril_ref, rsel_ref, knew_ref, vnew_ref, lfnew_ref,
     o_ref, m_ref, l_ref, acc_ref, carry_ref) = refs[2 * n_pages:]
    pg = pl.program_id(1)
    q = q_ref[...]
    width = FOX_HEADS * HEAD_DIM

    @pl.when(pg == 0)
    def _():
        m_ref[...] = jnp.full_like(m_ref, NEG)
        l_ref[...] = jnp.zeros_like(l_ref)
        acc_ref[...] = jnp.zeros_like(acc_ref)
        carry_ref[...] = jnp.zeros_like(carry_ref)

    def bias_rows(lf, tril):
        hi, mid, lo = _split3(lf)
        pre = _dot(tril, hi) + _dot(tril, mid) + _dot(tril, lo) + carry_ref[...]
        return pre, _dot3_nt(rsel_ref[...], pre)

    kt = jnp.concatenate([r[:, 0:width] for r in kv_refs], axis=0).astype(BF16)
    vt = jnp.concatenate([r[:, width:2 * width] for r in kv_refs], axis=0).astype(BF16)
    lf = jnp.concatenate([r[...] for r in lf_refs], axis=0)
    pre, bias = bias_rows(lf, tril_ref[...])
    s = _dot_nt(q, kt) - bias
    _softmax_update(s, vt, m_ref, l_ref, acc_ref)
    n_key = n_pages * PAGE
    carry_ref[...] = pre[n_key - 1:n_key, :]

    @pl.when(pg == n_steps - 1)
    def _():
        pre_n, bias_n = bias_rows(lfnew_ref[...], tril_ref[0:TAIL, 0:TAIL])
        sn = _dot_nt(q, knew_ref[...]) - bias_n
        j = lax.broadcasted_iota(jnp.int32, sn.shape, 1)
        tok = lax.broadcasted_iota(jnp.int32, sn.shape, 0) & 3
        sn = jnp.where((j <= tok) & (j < t_new), sn, NEG)
        _softmax_update(sn, vnew_ref[...], m_ref, l_ref, acc_ref)
        o_ref[...] = acc_ref[...] / l_ref[...]


def _sample_fox(pool_kv, pool_lf, page_table, qbd, tril, rsel, knew, vnew, lfnew, n_pages, t_new):
    nb, pages_per_seq = page_table.shape
    n_steps = pages_per_seq // n_pages
    width = FOX_HEADS * HEAD_DIM
    kv_specs = [
        pl.BlockSpec((None, PAGE, 2 * width), lambda b, pg, pt, j=j: (pt[b, pg * n_pages + j], 0, 0))
        for j in range(n_pages)
    ]
    lf_specs = [
        pl.BlockSpec((None, PAGE, FOX_HEADS), lambda b, pg, pt, j=j: (pt[b, pg * n_pages + j], 0, 0))
        for j in range(n_pages)
    ]
    per_b = lambda shape: pl.BlockSpec((None,) + shape, lambda b, pg, pt: (b,) + (0,) * len(shape))
    n_key = n_pages * PAGE
    grid_spec = pltpu.PrefetchScalarGridSpec(
        num_scalar_prefetch=1,
        grid=(nb, n_steps),
        in_specs=kv_specs + lf_specs + [
            per_b((SAMPLE_ROWS, width)),
            pl.BlockSpec((n_key, n_key), lambda b, pg, pt: (0, 0)),
            pl.BlockSpec((SAMPLE_ROWS, FOX_HEADS), lambda b, pg, pt: (0, 0)),
            per_b((TAIL, width)), per_b((TAIL, width)), per_b((TAIL, FOX_HEADS)),
        ],
        out_specs=per_b((SAMPLE_ROWS, width)),
        scratch_shapes=[
            pltpu.VMEM((SAMPLE_ROWS, 1), F32), pltpu.VMEM((SAMPLE_ROWS, 1), F32),
            pltpu.VMEM((SAMPLE_ROWS, width), F32), pltpu.VMEM((1, FOX_HEADS), F32),
        ],
    )
    return pl.pallas_call(
        functools.partial(_sample_fox_kernel, n_pages=n_pages, n_steps=n_steps, t_new=t_new),
        grid_spec=grid_spec,
        out_shape=jax.ShapeDtypeStruct((nb, SAMPLE_ROWS, width), F32),
        compiler_params=_cparams(("parallel", "arbitrary")),
        name="sample_fox",
    )(page_table, *([pool_kv] * n_pages), *([pool_lf] * n_pages), qbd, tril, rsel, knew, vnew, lfnew)


def _alibi_key_cols(pos):
    tab = np.zeros((len(pos), LANES), np.float32)
    tab[:, ALIBI0] = 1.0
    tab[:, ALIBI0 + 1] = 1.0
    tab[:, ALIBI0 + 2] = (pos // 128) * 128
    tab[:, ALIBI0 + 3] = pos % 128
    return tab


def _prompt_consts(t):
    pos = np.arange(t)
    kaug_win = _alibi_key_cols(pos)
    kaug_sel = kaug_win.copy()
    kaug_sel[pos, SELBIT0 + pos // SEL_BLOCK] = -MASK_BIG
    n_chunk = t // CMP_STRIDE
    kaug_cmp = _alibi_key_cols(np.arange(n_chunk) * CMP_STRIDE + CMP_BLOCK - 1)
    ovl = np.zeros((n_chunk, LANES), np.float32)
    n_cmp = (t - CMP_BLOCK) // CMP_STRIDE + 1
    n_slc = -(-t // SEL_BLOCK)
    cs = np.arange(n_cmp)[:, None] * CMP_STRIDE
    ss = np.arange(n_slc)[None, :] * SEL_BLOCK
    ovl[:n_cmp, SELBIT0:SELBIT0 + n_slc] = (cs < ss + SEL_BLOCK) & (cs + CMP_BLOCK > ss)
    tri = np.triu(np.ones((LANES, LANES), np.float32))
    as_bf = lambda a: jnp.asarray(a, BF16)
    return dict(kaug_sel=as_bf(kaug_sel), kaug_win=as_bf(kaug_win), kaug_cmp=as_bf(kaug_cmp),
                ovl=as_bf(ovl), tri=as_bf(tri))


def _sample_ovl(n_key, n_cmp, n_slc, nb_pad):
    ovl = np.zeros((n_key, nb_pad), np.float32)
    cs = np.arange(n_cmp)[:, None] * CMP_STRIDE
    ss = np.arange(n_slc)[None, :] * SEL_BLOCK
    ovl[:n_cmp, :n_slc] = (cs < ss + SEL_BLOCK) & (cs + CMP_BLOCK > ss)
    return jnp.asarray(ovl, BF16)


def _perm_in_proj(w_in, b_in):
    o_qa, o_kva, o_ga, o_fox, o_fb, o_gm = 0, 512, 1280, 1304, 2840, 2848
    d = w_in.shape[0]

    def cols(x, zeros):
        return jnp.concatenate([
            x[..., o_qa:o_kva], x[..., o_kva:o_ga], x[..., o_fox:o_fb], x[..., o_gm:],
            x[..., o_ga:o_fox], zeros(LANES - 3 * NSA_HEADS),
            x[..., o_fb:o_gm], zeros(LANES - FOX_HEADS)], axis=-1)

    w = cols(w_in, lambda k: jnp.zeros((d, k), w_in.dtype)).astype(BF16)
    b = cols(b_in, lambda k: jnp.zeros((k,), b_in.dtype)).reshape(1, C_TOT)
    wft = w_in[:, o_fb:o_gm].T.astype(BF16)
    bft = b_in[o_fb:o_gm].reshape(FOX_HEADS, 1)
    return w, b, wft, bft


def _compress_weights(pe_k, w1_k, w2_k, pe_v, w1_v, w2_v):
    n_sub = CMP_BLOCK // CMP_STRIDE

    def per_row(w1):
        w = w1.reshape(n_sub, CMP_STRIDE, HEAD_DIM, CMP_HIDDEN)
        return w.transpose(1, 2, 0, 3).reshape(CMP_STRIDE, HEAD_DIM, n_sub * CMP_HIDDEN)

    blocks = [per_row(w1_k), per_row(w1_k), per_row(w1_v), per_row(w1_v)]
    wr = jnp.zeros((CMP_STRIDE, 4, HEAD_DIM, 4, n_sub * CMP_HIDDEN), F32)
    for s, blk in enumerate(blocks):
        wr = wr.at[:, s, :, s, :].set(blk)
    wr = wr.reshape(CMP_STRIDE, 4 * HEAD_DIM, 4 * n_sub * CMP_HIDDEN).astype(BF16)
    pe8 = jnp.zeros((16, CMP_BLOCK * HEAD_DIM), F32).at[0].set(pe_k.reshape(-1)).at[1].set(pe_v.reshape(-1))
    w1cat = jnp.concatenate([w1_k, w1_v], axis=1).astype(BF16)
    pad = jnp.zeros((CMP_HIDDEN, LANES - HEAD_DIM), F32)
    w2pad = jnp.stack([jnp.concatenate([w2_k, pad], axis=1), jnp.concatenate([w2_v, pad], axis=1)]).astype(BF16)
    return dict(wr=wr, pe8=pe8.astype(BF16), w1cat=w1cat, w2pad=w2pad)


def _prompt_mixer(h1, w, nb, t):
    nsa, win, fox, logf, logft, qa_h, kva_h, fox_h, ga, gm = _in_proj(
        h1, w["w_in"], w["b_in"], w["wft"], w["bft"], nb, t, tm=256)
    consts = _prompt_consts(t)
    kcvc = _compress_prompt(nsa, w["cmp"], nb, t)
    o_a = _nsa_prompt(qa_h, kva_h, kcvc, ga, consts, nb, t, tq=256, tk=256)
    o_b = _fox_prompt(fox_h, logft, consts["tri"], nb, t, tq=512, tk=256)
    n = nb * t
    return o_a.reshape(n, -1), o_b.reshape(n, -1), gm, (nsa, win, fox, logf)


def _sample_mixer(h1, w, nb, t_new, nsa_pool, fox_pool, logf_pool, win_buf, page_table):
    n = nb * t_new
    past = page_table.shape[1] * PAGE
    nsa, win, fox, logf, logft, qa_h, kva_h, fox_h, ga, gm = _in_proj(
        h1, w["w_in"], w["b_in"], w["wft"], w["bft"], 1, n, tm=n)
    n_pool = nsa_pool.shape[0]
    nsa_rows = nsa_pool.reshape(n_pool, PAGE, 512)
    fox_rows = fox_pool.reshape(n_pool, PAGE, 2 * FOX_HEADS * HEAD_DIM)

    seq_len = past + t_new
    n_cmp = (seq_len - CMP_BLOCK) // CMP_STRIDE + 1
    n_slc = -(-seq_len // SEL_BLOCK)
    n_chunk = past // CMP_STRIDE
    assert n_cmp <= n_chunk
    kcvc = _compress_sample(nsa_rows, page_table, w["cmp"], n_pages=16)

    q = qa_h[0].reshape(NSA_HEADS, nb, t_new, LANES).transpose(1, 0, 2, 3).reshape(nb, SAMPLE_ROWS, LANES)
    nb_pad = -(-n_slc // LANES) * LANES
    ocmp, sel = _sample_select(q, kcvc, _sample_ovl(n_chunk, n_cmp, n_slc, nb_pad), past, n_cmp, n_slc, t_new)

    n_pages = 8
    n_steps = past // (n_pages * PAGE)
    blk_per_step = n_pages * PAGE // SEL_BLOCK
    selg = sel.reshape(nb, 2, 8, nb_pad)[:, :, :t_new, :n_steps * blk_per_step]
    selg = selg.reshape(nb, 2, 1, t_new, n_steps, blk_per_step)
    selx = jnp.broadcast_to(selg, (nb, 2, NSA_GROUP, t_new, n_steps, blk_per_step))
    selx = selx.transpose(0, 4, 1, 2, 3, 5).reshape(nb, n_steps, SAMPLE_ROWS, blk_per_step)
    selx = jnp.pad(selx, ((0, 0), (0, 0), (0, 0), (0, LANES - blk_per_step))).astype(BF16)
    expand = np.zeros((LANES, n_pages * PAGE), np.float32)
    expand[np.arange(n_pages * PAGE) // SEL_BLOCK, np.arange(n_pages * PAGE)] = 1.0
    q_sel = jnp.concatenate([q[:, :16], jnp.roll(q[:, 16:], HEAD_DIM, axis=-1)], axis=1)

    def new_rows(x, c0):
        r = x.reshape(nb, t_new, -1)[:, :, c0:c0 + LANES]
        return jnp.pad(r, ((0, 0), (0, TAIL - t_new), (0, 0))).astype(BF16)

    gate32 = ga.reshape(nb, t_new, 3, NSA_HEADS).transpose(0, 3, 1, 2).reshape(nb, SAMPLE_ROWS, 3)
    gate32 = jnp.pad(gate32, ((0, 0), (0, 0), (0, LANES - 3)))
    wbuf = win_buf.reshape(nb, win_buf.shape[1], 256)
    o_a32 = _sample_nsa(nsa_rows, page_table, q_sel, selx, jnp.asarray(expand, BF16),
                        new_rows(nsa, 256), new_rows(nsa, 384), wbuf, new_rows(win, 0), new_rows(win, 128),
                        ocmp, gate32, n_pages, past, t_new)
    o_a = o_a32[:, :, :HEAD_DIM].reshape(nb, NSA_HEADS, t_new, HEAD_DIM).transpose(0, 2, 1, 3)
    o_a = o_a.reshape(n, NSA_HEADS * HEAD_DIM).astype(BF16)

    width = FOX_HEADS * HEAD_DIM
    qf = fox_h[0, :FOX_HEADS, :, :HEAD_DIM].reshape(FOX_HEADS, nb, t_new, HEAD_DIM).transpose(1, 0, 2, 3)
    eye = jnp.eye(FOX_HEADS, dtype=qf.dtype)
    qbd = (qf[:, :, :, None, :] * eye[None, :, None, :, None]).reshape(nb, SAMPLE_ROWS, width)
    f_pages = 8
    tril = jnp.asarray(np.tril(np.ones((f_pages * PAGE, f_pages * PAGE), np.float32)), BF16)
    rsel = np.zeros((SAMPLE_ROWS, FOX_HEADS), np.float32)
    rsel[np.arange(SAMPLE_ROWS), np.arange(SAMPLE_ROWS) // t_new] = 1.0

    def new_fox(c0):
        r = fox.reshape(nb, t_new, 2 * width)[:, :, c0:c0 + width]
        return jnp.pad(r, ((0, 0), (0, TAIL - t_new), (0, 0))).astype(BF16)

    lfnew = jnp.pad(logf.reshape(nb, t_new, FOX_HEADS), ((0, 0), (0, TAIL - t_new), (0, 0)))
    o_b32 = _sample_fox(fox_rows, logf_pool, page_table, qbd, tril, jnp.asarray(rsel, BF16),
                        new_fox(0), new_fox(width), lfnew, f_pages, t_new)
    o_b = o_b32.reshape(nb, FOX_HEADS, t_new, FOX_HEADS, HEAD_DIM)
    o_b = jnp.einsum("bhtgd,hg->bthd", o_b, jnp.eye(FOX_HEADS, dtype=o_b.dtype))
    o_b = o_b.reshape(n, width).astype(BF16)
    return o_a, o_b, gm, (nsa, win, fox, logf)


def _layer(x, w, alpha, mixer, tm):
    h1 = _ffn_ln(x, w["ffn1_wg"], w["ffn1_wu"], w["ffn1_wd"], w["ln1_g"], w["ln1_b"], alpha, tm)
    o_a, o_b, gm, state = mixer(h1, w)
    h2 = _merge_ln(o_a, o_b, gm, h1, w["w_proj_a"], w["w_proj_b"], w["w_out"], w["ln2_g"], w["ln2_b"], alpha, tm)
    y = _ffn_ln(h2, w["ffn2_wg"], w["ffn2_wu"], w["ffn2_wd"], w["ln3_g"], w["ln3_b"], alpha, tm)
    return y, state


def kernel(x_prompt, x_sample, cache_nsa_kv, cache_fox_kv, cache_fox_logf, state_win_kv, page_table,
           ln1_g, ln1_b, ffn1_w_gate, ffn1_w_up, ffn1_w_down, w_in, b_in,
           cmp_pe_k, cmp_w1_k, cmp_w2_k, cmp_pe_v, cmp_w1_v, cmp_w2_v,
           w_proj_a, w_proj_b, w_out, ln2_g, ln2_b,
           ffn2_w_gate, ffn2_w_up, ffn2_w_down, ln3_g, ln3_b):
    depth = w_in.shape[0]
    nb_p, t_p, d = x_prompt.shape
    nb_s, t_s, _ = x_sample.shape
    alpha = float((2.0 * depth) ** 0.25)
    h_p = x_prompt.reshape(nb_p * t_p, d)
    h_s = x_sample.reshape(nb_s * t_s, d)
    st_p, st_s = [], []
    for l in range(depth):
        wi, bi, wft, bft = _perm_in_proj(w_in[l], b_in[l])
        vec = lambda a: a[l].reshape(1, d)
        w = dict(
            ln1_g=vec(ln1_g), ln1_b=vec(ln1_b), ln2_g=vec(ln2_g), ln2_b=vec(ln2_b), ln3_g=vec(ln3_g), ln3_b=vec(ln3_b),
            ffn1_wg=ffn1_w_gate[l].astype(BF16), ffn1_wu=ffn1_w_up[l].astype(BF16), ffn1_wd=ffn1_w_down[l].astype(BF16),
            ffn2_wg=ffn2_w_gate[l].astype(BF16), ffn2_wu=ffn2_w_up[l].astype(BF16), ffn2_wd=ffn2_w_down[l].astype(BF16),
            w_in=wi, b_in=bi, wft=wft, bft=bft,
            cmp=_compress_weights(cmp_pe_k[l], cmp_w1_k[l], cmp_w2_k[l], cmp_pe_v[l], cmp_w1_v[l], cmp_w2_v[l]),
            w_proj_a=w_proj_a[l].astype(BF16), w_proj_b=w_proj_b[l].astype(BF16), w_out=w_out[l].astype(BF16),
        )
        h_p, s_p = _layer(h_p, w, alpha, functools.partial(_prompt_mixer, nb=nb_p, t=t_p), tm=512)
        h_s, s_s = _layer(h_s, w, alpha, functools.partial(
            _sample_mixer, nb=nb_s, t_new=t_s, nsa_pool=cache_nsa_kv[l], fox_pool=cache_fox_kv[l],
            logf_pool=cache_fox_logf[l], win_buf=state_win_kv[l], page_table=page_table), tm=nb_s * t_s)
        st_p.append(s_p)
        st_s.append(s_s)

    def states(st, nb, t, win_prev):
        nsa = jnp.stack([s[0].reshape(nb, t, 4, NSA_KV_HEADS, HEAD_DIM) for s in st])
        fox = jnp.stack([s[2].reshape(nb, t, 2, FOX_HEADS, HEAD_DIM) for s in st])
        logf = jnp.stack([s[3].reshape(nb, t, FOX_HEADS) for s in st])
        wins = []
        for l, s in enumerate(st):
            wr = s[1].reshape(nb, t, 2, NSA_KV_HEADS, HEAD_DIM)
            if win_prev is None:
                wins.append(wr[:, t - min(WINDOW, t):])
            else:
                wins.append(jnp.concatenate([win_prev[l], wr], axis=1)[:, t:])
        return nsa, fox, logf, jnp.stack(wins)

    nsa_p, fox_p, logf_p, win_p = states(st_p, nb_p, t_p, None)
    nsa_s, fox_s, logf_s, win_s = states(st_s, nb_s, t_s, state_win_kv)
    return (h_p.reshape(nb_p, t_p, d), h_s.reshape(nb_s, t_s, d), nsa_p, fox_p, logf_p, win_p,
            nsa_s, fox_s, logf_s, win_s)
```

```python
import functools

import numpy as np
import jax
import jax.numpy as jnp
from jax import lax
from jax.experimental import pallas as pl
from jax.experimental.pallas import tpu as pltpu

F32 = jnp.float32
BF16 = jnp.bfloat16

LANES = 128
VMEM_LIMIT = 56 * 1024 * 1024

D_MODEL = 1024
HEAD_DIM = 64
NSA_HEADS = 8
NSA_KV_HEADS = 2
NSA_GROUP = NSA_HEADS // NSA_KV_HEADS
FOX_HEADS = 8
CMP_BLOCK = 32
CMP_STRIDE = 16
CMP_HIDDEN = 2 * HEAD_DIM
SEL_BLOCK = 64
N_SEL = 16
WINDOW = 512
PAGE = 128
LN_EPS = 1e-5
NEG = -1e30
FORCE = 1e9
SCALE = HEAD_DIM ** -0.5
MASK_BIG = 30000.0

SELBIT0 = 64
ALIBI0 = 96


def _dot(a, b):
    return jnp.dot(a, b, preferred_element_type=F32)


def _dot_nt(a, b):
    return lax.dot_general(a, b, (((1,), (1,)), ((), ())), preferred_element_type=F32)


def _split3(x):
    hi = x.astype(BF16)
    r1 = x - hi.astype(F32)
    mid = r1.astype(BF16)
    lo = (r1 - mid.astype(F32)).astype(BF16)
    return hi, mid, lo


def _dot3(x, w):
    hi, mid, lo = _split3(x)
    return _dot(hi, w) + _dot(mid, w) + _dot(lo, w)


def _dot3_nt(a, x):
    hi, mid, lo = _split3(x)
    return _dot_nt(a, hi) + _dot_nt(a, mid) + _dot_nt(a, lo)


def _sigmoid(x):
    return 1.0 / (1.0 + jnp.exp(-x))


def _log_sigmoid(x):
    return -(jnp.maximum(-x, 0.0) + jnp.log(1.0 + jnp.exp(-jnp.abs(x))))


def _gelu_tanh(x):
    c = np.float32(np.sqrt(2.0 / np.pi))
    return 0.5 * x * (1.0 + jnp.tanh(c * (x + np.float32(0.044715) * (x * x * x))))


def _cparams(sem):
    return pltpu.CompilerParams(dimension_semantics=sem, vmem_limit_bytes=VMEM_LIMIT)


def _ffn_ln_kernel(x_ref, wg_ref, wu_ref, wd_ref, g_ref, b_ref, o_ref, acc_ref, *, alpha, n_ff):
    j = pl.program_id(1)

    @pl.when(j == 0)
    def _():
        acc_ref[...] = jnp.zeros_like(acc_ref)

    xb = x_ref[...].astype(BF16)
    gate = _dot(xb, wg_ref[...])
    up = _dot(xb, wu_ref[...])
    mid = (gate * _sigmoid(gate) * up).astype(BF16)
    acc_ref[...] += _dot(mid, wd_ref[...])

    @pl.when(j == n_ff - 1)
    def _():
        y = alpha * x_ref[...] + 0.5 * acc_ref[...]
        mu = jnp.mean(y, axis=-1, keepdims=True)
        d = y - mu
        var = jnp.mean(d * d, axis=-1, keepdims=True)
        o_ref[...] = d * lax.rsqrt(var + LN_EPS) * g_ref[...] + b_ref[...]


def _ffn_ln(x, wg, wu, wd, g, b, alpha, tm):
    n, d = x.shape
    dff = wg.shape[1]
    tf = dff // 2
    n_ff = dff // tf
    return pl.pallas_call(
        functools.partial(_ffn_ln_kernel, alpha=alpha, n_ff=n_ff),
        grid=(n // tm, n_ff),
        in_specs=[
            pl.BlockSpec((tm, d), lambda i, j: (i, 0)),
            pl.BlockSpec((d, tf), lambda i, j: (0, j)),
            pl.BlockSpec((d, tf), lambda i, j: (0, j)),
            pl.BlockSpec((tf, d), lambda i, j: (j, 0)),
            pl.BlockSpec((1, d), lambda i, j: (0, 0)),
            pl.BlockSpec((1, d), lambda i, j: (0, 0)),
        ],
        out_specs=pl.BlockSpec((tm, d), lambda i, j: (i, 0)),
        out_shape=jax.ShapeDtypeStruct((n, d), F32),
        scratch_shapes=[pltpu.VMEM((tm, d), F32)],
        compiler_params=_cparams(("parallel", "arbitrary")),
        name="ffn_ln",
    )(x, wg, wu, wd, g, b)


C_KVA = 0
C_FOX = 768
C_GM = 1792
C_FB = 3840
C_TOT = 3968
R_QA = 0
R_NV = 512
R_FQ = 768
R_FV = 1280
R_GA = 1792
R_TOT = 1824
N_KH = 12
VT_ROWS = 80


def _head_pad(z, h):
    col = z[:, LANES * (h // 2):LANES * (h // 2 + 1)]
    if h % 2:
        col = pltpu.roll(col, HEAD_DIM, axis=1)
    lane = lax.broadcasted_iota(jnp.int32, col.shape, 1)
    return jnp.where(lane < HEAD_DIM, col, 0.0)


def _ones_row_block(width):
    sub = lax.broadcasted_iota(jnp.int32, (VT_ROWS - HEAD_DIM, width), 0)
    return jnp.where(sub == 0, 1.0, 0.0)


def _in_proj_kernel(h_ref, w_ref, b_ref, wt_ref, bt_ref,
                    nsa_ref, win_ref, fox_ref, logf_ref, kh_ref,
                    qat_ref, nvt_ref, fqt_ref, fvt_ref, gat_ref, gm_ref):
    hb = h_ref[...].astype(BF16)
    tm = hb.shape[0]

    def proj(c0, c1):
        return _dot(hb, w_ref[:, c0:c1]) + b_ref[:, c0:c1]

    def proj_t(r0, r1):
        return _dot_nt(wt_ref[r0:r1, :], hb) + bt_ref[r0:r1, :]

    zkv = proj(C_KVA, C_FOX)
    nsa_ref[...] = zkv[:, :512]
    win_ref[...] = zkv[:, 512:]
    for j, h in enumerate((4, 5, 8, 9)):
        kh_ref[j] = _head_pad(zkv, h).astype(BF16)

    zf = proj(C_FOX, C_GM)
    fox_ref[...] = zf
    for h in range(FOX_HEADS):
        kh_ref[4 + h] = _head_pad(zf, h).astype(BF16)

    gm_ref[...] = _sigmoid(proj(C_GM, C_FB))
    logf_ref[...] = _log_sigmoid(proj(C_FB, C_TOT))[:, :FOX_HEADS]

    ones_blk = _ones_row_block(tm)

    def value_heads(zt, out_ref, n_heads):
        for h in range(n_heads):
            vt = jnp.concatenate([zt[h * HEAD_DIM:(h + 1) * HEAD_DIM], ones_blk], axis=0)
            out_ref[h] = vt.astype(BF16)

    qat_ref[...] = (proj_t(R_QA, R_NV) * SCALE).astype(BF16).reshape(qat_ref.shape)
    value_heads(proj_t(R_NV, R_FQ), nvt_ref, 4)
    fqt_ref[...] = (proj_t(R_FQ, R_FV) * SCALE).astype(BF16).reshape(fqt_ref.shape)
    value_heads(proj_t(R_FV, R_GA), fvt_ref, FOX_HEADS)
    gat_ref[...] = _sigmoid(proj_t(R_GA, R_TOT))


def _in_proj(h, w, b, wt, bt, nseq, tseq, tm):
    n, d = h.shape
    tps = tseq // tm
    row = lambda i: (i, 0)
    headmaj = lambda i: (i // tps, 0, i % tps, 0)
    featmaj = lambda i: (i // tps, 0, 0, i % tps)
    out_shape = [
        jax.ShapeDtypeStruct((n, 512), F32),
        jax.ShapeDtypeStruct((n, 256), F32),
        jax.ShapeDtypeStruct((n, 1024), F32),
        jax.ShapeDtypeStruct((n, FOX_HEADS), F32),
        jax.ShapeDtypeStruct((nseq, N_KH, tseq, LANES), BF16),
        jax.ShapeDtypeStruct((nseq, NSA_HEADS, HEAD_DIM, tseq), BF16),
        jax.ShapeDtypeStruct((nseq, 4, VT_ROWS, tseq), BF16),
        jax.ShapeDtypeStruct((nseq, FOX_HEADS, HEAD_DIM, tseq), BF16),
        jax.ShapeDtypeStruct((nseq, FOX_HEADS, VT_ROWS, tseq), BF16),
        jax.ShapeDtypeStruct((R_TOT - R_GA, n), F32),
        jax.ShapeDtypeStruct((n, 2 * D_MODEL), F32),
    ]
    out_specs = [
        pl.BlockSpec((tm, 512), row),
        pl.BlockSpec((tm, 256), row),
        pl.BlockSpec((tm, 1024), row),
        pl.BlockSpec((tm, FOX_HEADS), row),
        pl.BlockSpec((None, N_KH, tm, LANES), headmaj),
        pl.BlockSpec((None, NSA_HEADS, HEAD_DIM, tm), featmaj),
        pl.BlockSpec((None, 4, VT_ROWS, tm), featmaj),
        pl.BlockSpec((None, FOX_HEADS, HEAD_DIM, tm), featmaj),
        pl.BlockSpec((None, FOX_HEADS, VT_ROWS, tm), featmaj),
        pl.BlockSpec((R_TOT - R_GA, tm), lambda i: (0, i)),
        pl.BlockSpec((tm, 2 * D_MODEL), row),
    ]
    return pl.pallas_call(
        _in_proj_kernel,
        grid=(n // tm,),
        in_specs=[
            pl.BlockSpec((tm, d), row),
            pl.BlockSpec((d, C_TOT), lambda i: (0, 0)),
            pl.BlockSpec((1, C_TOT), lambda i: (0, 0)),
            pl.BlockSpec((R_TOT, d), lambda i: (0, 0)),
            pl.BlockSpec((R_TOT, 1), lambda i: (0, 0)),
        ],
        out_specs=out_specs,
        out_shape=out_shape,
        compiler_params=_cparams(("parallel",)),
        name="in_proj",
    )(h, w, b, wt, bt)


def _cmp_parts(k_refs, v_refs, wr_ref, rows_per_ref):
    n_chunk = rows_per_ref // CMP_STRIDE
    acc = None
    for r in range(CMP_STRIDE):
        def rows(refs):
            xs = [ref[pl.ds(r, n_chunk, stride=CMP_STRIDE), :] for ref in refs]
            return xs[0] if len(xs) == 1 else jnp.concatenate(xs, axis=0)
        x = jnp.concatenate([rows(k_refs), rows(v_refs)], axis=1)
        p = _dot(x.astype(BF16), wr_ref[r])
        acc = p if acc is None else acc + p
    return acc


def _cmp_finish(parts, bias8, w2_ref, out_ref, vt_ref=None):
    n = parts.shape[0]
    for s in range(4):
        kind = s // 2
        pa = parts[:, 256 * s:256 * s + CMP_HIDDEN]
        pb = parts[:, 256 * s + CMP_HIDDEN:256 * (s + 1)]
        pb = pltpu.roll(pb, n - 1, axis=0)
        bias = bias8[kind:kind + 1, CMP_HIDDEN * kind:CMP_HIDDEN * (kind + 1)]
        hid = bias + pa + pb
        c = _dot(_gelu_tanh(hid).astype(BF16), w2_ref[kind])
        out_ref[s] = c.astype(BF16)
        if vt_ref is not None and kind == 1:
            vt = jnp.concatenate([c.T[:HEAD_DIM], _ones_row_block(n)], axis=0)
            vt_ref[s - 2] = vt.astype(BF16)


def _cmp_bias(pe_ref, w1_ref):
    return _dot(pe_ref[...], w1_ref[...])


def _compress_prompt_kernel(k_ref, v_ref, wr_ref, pe_ref, w1_ref, w2_ref, out_ref, vt_ref, *, t):
    parts = _cmp_parts([k_ref], [v_ref], wr_ref, t)
    _cmp_finish(parts, _cmp_bias(pe_ref, w1_ref), w2_ref, out_ref, vt_ref)


def _compress_prompt(nsa_rows, cw, nb, t):
    n_chunk = t // CMP_STRIDE
    return pl.pallas_call(
        functools.partial(_compress_prompt_kernel, t=t),
        grid=(nb,),
        in_specs=[
            pl.BlockSpec((t, LANES), lambda b: (b, 0)),
            pl.BlockSpec((t, LANES), lambda b: (b, 1)),
            pl.BlockSpec((CMP_STRIDE, 256, 1024), lambda b: (0, 0, 0)),
            pl.BlockSpec((16, 2048), lambda b: (0, 0)),
            pl.BlockSpec((2048, 256), lambda b: (0, 0)),
            pl.BlockSpec((2, CMP_HIDDEN, LANES), lambda b: (0, 0, 0)),
        ],
        out_specs=[
            pl.BlockSpec((None, 4, n_chunk, LANES), lambda b: (b, 0, 0, 0)),
            pl.BlockSpec((None, 2, VT_ROWS, n_chunk), lambda b: (b, 0, 0, 0)),
        ],
        out_shape=[
            jax.ShapeDtypeStruct((nb, 4, n_chunk, LANES), BF16),
            jax.ShapeDtypeStruct((nb, 2, VT_ROWS, n_chunk), BF16),
        ],
        compiler_params=_cparams(("parallel",)),
        name="compress_prompt",
    )(nsa_rows, nsa_rows, cw["wr"], cw["pe8"], cw["w1cat"], cw["w2pad"])


def _compress_sample_kernel(pt_ref, *refs, n_pages, n_steps):
    k_refs = refs[:n_pages]
    v_refs = refs[n_pages:2 * n_pages]
    wr_ref, pe_ref, w1_ref, w2_ref, out_ref, parts_ref = refs[2 * n_pages:]
    pg = pl.program_id(1)
    rows = n_pages * (PAGE // CMP_STRIDE)
    parts = _cmp_parts(k_refs, v_refs, wr_ref, PAGE)
    parts_ref[pl.ds(pl.multiple_of(pg * rows, rows), rows), :] = parts

    @pl.when(pg == n_steps - 1)
    def _():
        _cmp_finish(parts_ref[...], _cmp_bias(pe_ref, w1_ref), w2_ref, out_ref)


def _compress_sample(pool_rows, page_table, cw, n_pages):
    nb, pages_per_seq = page_table.shape
    n_steps = pages_per_seq // n_pages
    n_chunk = pages_per_seq * (PAGE // CMP_STRIDE)
    page_specs = [
        pl.BlockSpec((None, PAGE, LANES), lambda b, pg, pt, j=j, c=c: (pt[b, pg * n_pages + j], 0, c))
        for c in range(2) for j in range(n_pages)
    ]
    grid_spec = pltpu.PrefetchScalarGridSpec(
        num_scalar_prefetch=1,
        grid=(nb, n_steps),
        in_specs=page_specs + [
            pl.BlockSpec((CMP_STRIDE, 256, 1024), lambda b, pg, pt: (0, 0, 0)),
            pl.BlockSpec((16, 2048), lambda b, pg, pt: (0, 0)),
            pl.BlockSpec((2048, 256), lambda b, pg, pt: (0, 0)),
            pl.BlockSpec((2, CMP_HIDDEN, LANES), lambda b, pg, pt: (0, 0, 0)),
        ],
        out_specs=pl.BlockSpec((None, 4, n_chunk, LANES), lambda b, pg, pt: (b, 0, 0, 0)),
        scratch_shapes=[pltpu.VMEM((n_chunk, 1024), F32)],
    )
    return pl.pallas_call(
        functools.partial(_compress_sample_kernel, n_pages=n_pages, n_steps=n_steps),
        grid_spec=grid_spec,
        out_shape=jax.ShapeDtypeStruct((nb, 4, n_chunk, LANES), BF16),
        compiler_params=_cparams(("parallel", "arbitrary")),
        name="compress_sample",
    )(page_table, *([pool_rows] * (2 * n_pages)), cw["wr"], cw["pe8"], cw["w1cat"], cw["w2pad"])


def _softmax_update(s, v, m_ref, l_ref, acc_ref):
    m_prev = m_ref[...]
    m_new = jnp.maximum(m_prev, jnp.max(s, axis=1, keepdims=True))
    a = jnp.exp(m_prev - m_new)
    p = jnp.exp(s - m_new)
    l_ref[...] = a * l_ref[...] + jnp.sum(p, axis=1, keepdims=True)
    acc_ref[...] = a * acc_ref[...] + _dot(p.astype(BF16), v)
    m_ref[...] = m_new


def _softmax_update_t(s, vt, m_ref, acc_ref):
    m_prev = m_ref[...]
    m_new = jnp.maximum(m_prev, jnp.max(s, axis=0, keepdims=True))
    a = jnp.exp(m_prev - m_new)
    p = jnp.exp(s - m_new).astype(BF16)
    acc_ref[...] = a * acc_ref[...] + _dot(vt, p)
    m_ref[...] = m_new


def _softmax_reset(m_ref, acc_ref):
    m_ref[...] = jnp.full_like(m_ref, NEG)
    acc_ref[...] = jnp.zeros_like(acc_ref)


def _softmax_result(acc_ref):
    acc = acc_ref[...]
    return acc[:HEAD_DIM] / acc[HEAD_DIM:HEAD_DIM + 1]


def _nsa_prompt_kernel(q_ref, selk_ref, selvt_ref, wink_ref, winvt_ref, kc_ref, vct_ref, gat_ref,
                       kaug_sel_ref, kaug_win_ref, kaug_cmp_ref, ovlt_ref,
                       o_ref, ksel_s, kwin_s, m_ref, acc_ref, *, tq, tk, n_cmp, n_slc):
    g = pl.program_id(1)
    qt = pl.program_id(2)
    cols = NSA_GROUP * tq
    q0 = qt * tq

    @pl.when(qt == 0)
    def _():
        ksel_s[...] = selk_ref[...] + kaug_sel_ref[...]
        kwin_s[...] = wink_ref[...] + kaug_win_ref[...]

    qt4 = jnp.concatenate([q_ref[r] for r in range(NSA_GROUP)], axis=1)

    def head_and_pos(shape):
        lane = lax.broadcasted_iota(jnp.int32, shape, 1)
        return lane >> (tq.bit_length() - 1), q0 + (lane & (tq - 1))

    sub = lax.broadcasted_iota(jnp.int32, (16, cols), 0)
    r, t = head_and_pos((16, cols))
    sl0 = jnp.where(r == 0, 0.5, jnp.where(r == 1, 0.25, jnp.where(r == 2, 0.125, 0.0625))).astype(F32)
    slope = jnp.where(g == 0, sl0, sl0 * 0.0625)
    t_hi = ((t >> 7) << 7).astype(F32)
    t_lo = (t & 127).astype(F32)
    al = jnp.where(sub == 0, -slope * t_hi,
                   jnp.where(sub == 1, -slope * t_lo,
                             jnp.where((sub == 2) | (sub == 3), slope, 0.0))).astype(BF16)
    zeros16 = jnp.zeros((16, cols), BF16)

    def query_cols(selbits):
        return jnp.concatenate([qt4, selbits, al, zeros16], axis=0)

    qa = query_cols(jnp.zeros((n_slc, cols), BF16))

    n_chunk = kc_ref.shape[0]
    kc = kc_ref[...] + kaug_cmp_ref[...]
    s = _dot(kc, qa)
    nidx = lax.broadcasted_iota(jnp.int32, (n_chunk, cols), 0)
    _, t_c = head_and_pos((n_chunk, cols))
    valid = (t_c >= nidx * CMP_STRIDE + (CMP_BLOCK - 1)) & (nidx < n_cmp)
    s = jnp.where(valid, s, NEG)
    m = jnp.max(s, axis=0, keepdims=True)
    p = jnp.where(valid, jnp.exp(s - m), 0.0)
    den = jnp.sum(p, axis=0, keepdims=True)
    p = p / jnp.where(den > 0.0, den, 1.0)
    o_cmp = _dot(vct_ref[...], p.astype(BF16))[:HEAD_DIM]

    imp = p[:, 0:tq] + p[:, tq:2 * tq] + p[:, 2 * tq:3 * tq] + p[:, 3 * tq:4 * tq]
    hi, mid, lo = _split3(imp)
    ovlt = ovlt_ref[...]
    imp_slc = _dot(ovlt, hi) + _dot(ovlt, mid) + _dot(ovlt, lo)
    blk = lax.broadcasted_iota(jnp.int32, (n_slc, tq), 0)
    t1 = q0 + lax.broadcasted_iota(jnp.int32, (n_slc, tq), 1)
    cur = t1 >> 6
    forced = (blk == 0) | (blk == cur) | (blk == cur - 1)
    score = jnp.where(forced, FORCE, jnp.where(blk * SEL_BLOCK <= t1, imp_slc, -FORCE))
    rank = jnp.zeros((n_slc, tq), F32)
    for i in range(n_slc):
        row = score[i:i + 1, :]
        beats = (row > score) | ((row == score) & (blk > i))
        rank = rank + jnp.where(beats, 1.0, 0.0)
    notsel = jnp.where(rank >= float(N_SEL), 1.0, 0.0).astype(BF16)
    qs = query_cols(jnp.concatenate([notsel] * NSA_GROUP, axis=1))

    kpos_iota = lax.broadcasted_iota(jnp.int32, (tk, cols), 0)
    _, t_row = head_and_pos((1, cols))
    n_full = q0 // tk

    def branch(qx, k_s, vt_ref, kt_lo, window):
        _softmax_reset(m_ref, acc_ref)

        def tile(k0, causal):
            sc = _dot(k_s[pl.ds(k0, tk), :], qx)
            rel = t_row - k0
            if causal:
                sc = jnp.where(kpos_iota <= rel, sc, NEG)
            elif window:
                sc = jnp.where(kpos_iota > rel - WINDOW, sc, NEG)
            _softmax_update_t(sc, vt_ref[:, pl.ds(k0, tk)], m_ref, acc_ref)

        def body(kt, carry):
            tile(pl.multiple_of(kt * tk, tk), False)
            return carry

        lax.fori_loop(kt_lo, n_full, body, 0)
        for j in range(tq // tk):
            tile(pl.multiple_of(q0 + j * tk, tk), True)
        return _softmax_result(acc_ref)

    o_sel = branch(qs, ksel_s, selvt_ref, 0, False)
    o_win = branch(qa, kwin_s, winvt_ref, jnp.maximum(q0 - WINDOW, 0) // tk, True)

    outs = []
    for rr in range(NSA_GROUP):
        def gate(br):
            return gat_ref[pl.ds(br * NSA_HEADS + g * NSA_GROUP + rr, 1), :]
        sl = slice(rr * tq, (rr + 1) * tq)
        outs.append(gate(0) * o_cmp[:, sl] + gate(1) * o_sel[:, sl] + gate(2) * o_win[:, sl])
    o_ref[...] = jnp.concatenate(outs, axis=0).T.astype(BF16)


def _nsa_prompt(qat, kh, nvt, kcvc, vct, gat, consts, nb, t, tq, tk):
    n_cmp = (t - CMP_BLOCK) // CMP_STRIDE + 1
    n_slc = -(-t // SEL_BLOCK)
    n_chunk = t // CMP_STRIDE
    assert n_slc == 32 and tq % tk == 0 and tq <= WINDOW
    q5 = qat.reshape(nb, NSA_KV_HEADS, NSA_GROUP, HEAD_DIM, t)
    k_spec = lambda off: pl.BlockSpec((None, None, t, LANES), lambda b, g, qt: (b, off + g, 0, 0))
    vt_spec = lambda off: pl.BlockSpec((None, None, VT_ROWS, t), lambda b, g, qt: (b, off + g, 0, 0))
    const2 = lambda shape: pl.BlockSpec(shape, lambda b, g, qt: (0, 0))
    cols = NSA_GROUP * tq
    return pl.pallas_call(
        functools.partial(_nsa_prompt_kernel, tq=tq, tk=tk, n_cmp=n_cmp, n_slc=n_slc),
        grid=(nb, NSA_KV_HEADS, t // tq),
        in_specs=[
            pl.BlockSpec((None, None, NSA_GROUP, HEAD_DIM, tq), lambda b, g, qt: (b, g, 0, 0, qt)),
            k_spec(0), vt_spec(0), k_spec(2), vt_spec(2),
            pl.BlockSpec((None, None, n_chunk, LANES), lambda b, g, qt: (b, g, 0, 0)),
            pl.BlockSpec((None, None, VT_ROWS, n_chunk), lambda b, g, qt: (b, g, 0, 0)),
            pl.BlockSpec((R_TOT - R_GA, tq), lambda b, g, qt: (0, b * (t // tq) + qt)),
            const2((t, LANES)), const2((t, LANES)), const2((n_chunk, LANES)), const2((n_slc, n_chunk)),
        ],
        out_specs=pl.BlockSpec((None, tq, NSA_GROUP * HEAD_DIM), lambda b, g, qt: (b, qt, g)),
        out_shape=jax.ShapeDtypeStruct((nb, t, NSA_HEADS * HEAD_DIM), BF16),
        scratch_shapes=[
            pltpu.VMEM((t, LANES), BF16), pltpu.VMEM((t, LANES), BF16),
            pltpu.VMEM((1, cols), F32), pltpu.VMEM((VT_ROWS, cols), F32),
        ],
        compiler_params=_cparams(("parallel", "parallel", "arbitrary")),
        name="nsa_prompt",
    )(q5, kh, nvt, kh, nvt, kcvc, vct, gat,
      consts["kaug_sel"], consts["kaug_win"], consts["kaug_cmp"], consts["ovlt"])


FOX_BIAS0 = 64
FOX_PAIR = 2


def _fox_prompt_kernel(q_ref, k_ref, vt_ref, lf_ref, tril_ref, o_ref, kaug_s, m_ref, acc_ref, *, tq, tk):
    hp = pl.program_id(1)
    qt = pl.program_id(2)
    q0 = qt * tq
    t = k_ref.shape[1]

    @pl.when(qt == 0)
    def _():
        tb = tril_ref.shape[0]
        tril = tril_ref[...]
        carry = jnp.zeros((1, FOX_HEADS), F32)
        cs = []
        for i in range(t // tb):
            hi, mid, lo = _split3(lf_ref[i * tb:(i + 1) * tb, :])
            c = _dot(tril, hi) + _dot(tril, mid) + _dot(tril, lo) + carry
            cs.append(c)
            carry = c[tb - 1:tb, :]
        hi, mid, lo = _split3(-jnp.concatenate(cs, axis=0))
        hrow = lax.broadcasted_iota(jnp.int32, (FOX_HEADS, LANES), 0)
        lane = lax.broadcasted_iota(jnp.int32, (FOX_HEADS, LANES), 1)
        for i in range(FOX_PAIR):
            head = FOX_PAIR * hp + i

            def place(j):
                return jnp.where((hrow == head) & (lane == FOX_BIAS0 + j), 1.0, 0.0).astype(BF16)

            aug = _dot(hi, place(0)) + _dot(mid, place(1)) + _dot(lo, place(2))
            kaug_s[i] = k_ref[i] + aug.astype(BF16)

    sub = lax.broadcasted_iota(jnp.int32, (16, tq), 0)
    ones3 = jnp.where(sub < 3, 1.0, 0.0).astype(BF16)
    zeros = jnp.zeros((LANES - HEAD_DIM - 16, tq), BF16)
    qx = [jnp.concatenate([q_ref[i], ones3, zeros], axis=0) for i in range(FOX_PAIR)]

    kpos_iota = lax.broadcasted_iota(jnp.int32, (tk, tq), 0)
    t_row = q0 + lax.broadcasted_iota(jnp.int32, (1, tq), 1)
    for i in range(FOX_PAIR):
        _softmax_reset(m_ref.at[i], acc_ref.at[i])

    def tile(k0, masked):
        for i in range(FOX_PAIR):
            sc = _dot(kaug_s[i, pl.ds(k0, tk), :], qx[i])
            if masked:
                sc = jnp.where(kpos_iota <= t_row - k0, sc, NEG)
            _softmax_update_t(sc, vt_ref[i, :, pl.ds(k0, tk)], m_ref.at[i], acc_ref.at[i])

    def body(kt, carry):
        tile(pl.multiple_of(kt * tk, tk), False)
        return carry

    lax.fori_loop(0, q0 // tk, body, 0)
    for j in range(tq // tk):
        tile(pl.multiple_of(q0 + j * tk, tk), True)
    o_t = jnp.concatenate([_softmax_result(acc_ref.at[i]) for i in range(FOX_PAIR)], axis=0)
    o_ref[...] = o_t.T.astype(BF16)


def _fox_prompt(fqt, kh, fvt, logf, tril, nb, t, tq, tk):
    n_pair = FOX_HEADS // FOX_PAIR
    q5 = fqt.reshape(nb, n_pair, FOX_PAIR, HEAD_DIM, t)
    k5 = kh.reshape(nb, N_KH // FOX_PAIR, FOX_PAIR, t, LANES)
    fox_pair0 = (N_KH - FOX_HEADS) // FOX_PAIR
    v5 = fvt.reshape(nb, n_pair, FOX_PAIR, VT_ROWS, t)
    return pl.pallas_call(
        functools.partial(_fox_prompt_kernel, tq=tq, tk=tk),
        grid=(nb, n_pair, t // tq),
        in_specs=[
            pl.BlockSpec((None, None, FOX_PAIR, HEAD_DIM, tq), lambda b, hp, qt: (b, hp, 0, 0, qt)),
            pl.BlockSpec((None, None, FOX_PAIR, t, LANES), lambda b, hp, qt: (b, fox_pair0 + hp, 0, 0, 0)),
            pl.BlockSpec((None, None, FOX_PAIR, VT_ROWS, t), lambda b, hp, qt: (b, hp, 0, 0, 0)),
            pl.BlockSpec((t, FOX_HEADS), lambda b, hp, qt: (b, 0)),
            pl.BlockSpec(tril.shape, lambda b, hp, qt: (0, 0)),
        ],
        out_specs=pl.BlockSpec((None, tq, FOX_PAIR * HEAD_DIM), lambda b, hp, qt: (b, qt, hp)),
        out_shape=jax.ShapeDtypeStruct((nb, t, FOX_HEADS * HEAD_DIM), BF16),
        scratch_shapes=[
            pltpu.VMEM((FOX_PAIR, t, LANES), BF16),
            pltpu.VMEM((FOX_PAIR, 1, tq), F32), pltpu.VMEM((FOX_PAIR, VT_ROWS, tq), F32),
        ],
        compiler_params=_cparams(("parallel", "parallel", "arbitrary")),
        name="fox_prompt",
    )(q5, k5, v5, logf, tril)


def _merge_ln_kernel(oa_ref, ob_ref, gm_ref, h_ref, wa_ref, wb_ref, wo_ref, g_ref, b_ref, o_ref, *, alpha):
    pa = _dot(oa_ref[...], wa_ref[...])
    pb = _dot(ob_ref[...], wb_ref[...])
    merged = gm_ref[:, :D_MODEL] * pa + gm_ref[:, D_MODEL:] * pb
    y = alpha * h_ref[...] + _dot(merged.astype(BF16), wo_ref[...])
    mu = jnp.mean(y, axis=-1, keepdims=True)
    d = y - mu
    var = jnp.mean(d * d, axis=-1, keepdims=True)
    o_ref[...] = d * lax.rsqrt(var + LN_EPS) * g_ref[...] + b_ref[...]


def _merge_ln(oa, ob, gm, h, wa, wb, wo, g, b, alpha, tm):
    n, d = h.shape
    row = lambda i: (i, 0)
    fixed = lambda i: (0, 0)
    return pl.pallas_call(
        functools.partial(_merge_ln_kernel, alpha=alpha),
        grid=(n // tm,),
        in_specs=[
            pl.BlockSpec((tm, 512), row), pl.BlockSpec((tm, 512), row),
            pl.BlockSpec((tm, 2 * d), row), pl.BlockSpec((tm, d), row),
            pl.BlockSpec((512, d), fixed), pl.BlockSpec((512, d), fixed), pl.BlockSpec((d, d), fixed),
            pl.BlockSpec((1, d), fixed), pl.BlockSpec((1, d), fixed),
        ],
        out_specs=pl.BlockSpec((tm, d), row),
        out_shape=jax.ShapeDtypeStruct((n, d), F32),
        compiler_params=_cparams(("parallel",)),
        name="merge_ln",
    )(oa, ob, gm, h, wa, wb, wo, g, b)


SAMPLE_ROWS = 32


def _row_slopes(rows, width, head0=0):
    row = lax.broadcasted_iota(jnp.int32, (rows, width), 0)
    head = head0 + (row >> 2)
    slope = lax.bitcast_convert_type((126 - head) << 23, F32)
    return slope, row & 3


def _sample_select_kernel(q_ref, kc_ref, vc_ref, ovl_ref, ocmp_ref, sel_ref, *, q_off, n_cmp, n_slc, t_new):
    n_key = kc_ref.shape[1]
    hrows = SAMPLE_ROWS // NSA_KV_HEADS
    lane = lax.broadcasted_iota(jnp.int32, (hrows, n_key), 1)
    ps, os_ = [], []
    for g in range(NSA_KV_HEADS):
        slope, tok = _row_slopes(hrows, n_key, head0=g * NSA_GROUP)
        dist = (q_off + tok) - (lane * CMP_STRIDE + (CMP_BLOCK - 1))
        valid = (dist >= 0) & (lane < n_cmp)
        s = _dot_nt(q_ref[g * hrows:(g + 1) * hrows], kc_ref[g]) - slope * dist.astype(F32)
        s = jnp.where(valid, s, NEG)
        m = jnp.max(s, axis=1, keepdims=True)
        p = jnp.where(valid, jnp.exp(s - m), 0.0)
        den = jnp.sum(p, axis=1, keepdims=True)
        p = p / jnp.where(den > 0.0, den, 1.0)
        ps.append(p)
        os_.append(_dot(p.astype(BF16), vc_ref[g]))
    ocmp_ref[...] = jnp.concatenate(os_, axis=0)

    p_all = jnp.concatenate(ps, axis=0)
    rr = lax.broadcasted_iota(jnp.int32, (2 * 8, SAMPLE_ROWS), 0)
    cc = lax.broadcasted_iota(jnp.int32, (2 * 8, SAMPLE_ROWS), 1)
    gather = jnp.where(((rr >> 3) == (cc >> 4)) & ((rr & 7) == (cc & 3)), 1.0, 0.0).astype(BF16)
    hi, mid, lo = _split3(p_all)
    imp = _dot(gather, hi) + _dot(gather, mid) + _dot(gather, lo)
    imp_slc = _dot3(imp, ovl_ref[...])
    nb_pad = imp_slc.shape[1]
    blk = lax.broadcasted_iota(jnp.int32, (16, nb_pad), 1)
    blk_f = blk.astype(F32)
    tpos = q_off + (lax.broadcasted_iota(jnp.int32, (16, nb_pad), 0) & 7)
    cur = tpos >> 6
    forced = (blk == 0) | (blk == cur) | (blk == cur - 1)
    score = jnp.where(forced, FORCE, jnp.where(blk * SEL_BLOCK <= tpos, imp_slc, -FORCE))
    score = jnp.where(blk < n_slc, score, -3.0 * FORCE)
    sel = jnp.zeros((16, nb_pad), F32)
    for _ in range(min(N_SEL, n_slc)):
        mx = jnp.max(score, axis=1, keepdims=True)
        first = jnp.min(jnp.where(score == mx, blk_f, float(nb_pad)), axis=1, keepdims=True)
        hit = blk_f == first
        sel = jnp.where(hit, 1.0, sel)
        score = jnp.where(hit, -4.0 * FORCE, score)
    sel_ref[...] = sel


def _sample_select(q32, kcvc, ovl, q_off, n_cmp, n_slc, t_new):
    nb = q32.shape[0]
    n_key = kcvc.shape[2]
    nb_pad = ovl.shape[1]
    return pl.pallas_call(
        functools.partial(_sample_select_kernel, q_off=q_off, n_cmp=n_cmp, n_slc=n_slc, t_new=t_new),
        grid=(nb,),
        in_specs=[
            pl.BlockSpec((None, SAMPLE_ROWS, LANES), lambda b: (b, 0, 0)),
            pl.BlockSpec((None, None, 2, n_key, LANES), lambda b: (b, 0, 0, 0, 0)),
            pl.BlockSpec((None, None, 2, n_key, LANES), lambda b: (b, 1, 0, 0, 0)),
            pl.BlockSpec((n_key, nb_pad), lambda b: (0, 0)),
        ],
        out_specs=[
            pl.BlockSpec((None, SAMPLE_ROWS, LANES), lambda b: (b, 0, 0)),
            pl.BlockSpec((None, 16, nb_pad), lambda b: (b, 0, 0)),
        ],
        out_shape=[
            jax.ShapeDtypeStruct((nb, SAMPLE_ROWS, LANES), F32),
            jax.ShapeDtypeStruct((nb, 16, nb_pad), F32),
        ],
        compiler_params=_cparams(("parallel",)),
        name="sample_select",
    )(q32, kcvc.reshape(nb, 2, 2, n_key, LANES), kcvc.reshape(nb, 2, 2, n_key, LANES), ovl)


TAIL = 128


def _tail_scores(q, knew, t_new):
    s = _dot_nt(q, knew)
    slope, tok = _row_slopes(SAMPLE_ROWS, TAIL)
    j = lax.broadcasted_iota(jnp.int32, s.shape, 1)
    d = tok - j
    s = s - slope * d.astype(F32)
    return jnp.where((d >= 0) & (j < t_new), s, NEG)


def _sample_nsa_kernel(pt_ref, *refs, n_pages, n_steps, q_off, t_new):
    page_refs = refs[:n_pages]
    (q_ref, selx_ref, exp_ref, knew_ref, vnew_ref, wbuf_ref, wknew_ref, wvnew_ref, ocmp_ref, gate_ref,
     o_ref, m_ref, l_ref, acc_ref) = refs[n_pages:]
    pg = pl.program_id(1)
    n_key = n_pages * PAGE
    q = q_ref[...]
    slope, tok = _row_slopes(SAMPLE_ROWS, n_key)

    @pl.when(pg == 0)
    def _():
        m_ref[...] = jnp.full_like(m_ref, NEG)
        l_ref[...] = jnp.zeros_like(l_ref)
        acc_ref[...] = jnp.zeros_like(acc_ref)

    kt = jnp.concatenate([r[:, 0:LANES] for r in page_refs], axis=0).astype(BF16)
    vt = jnp.concatenate([r[:, LANES:2 * LANES] for r in page_refs], axis=0).astype(BF16)
    s = _dot_nt(q, kt)
    kpos = pg * n_key + lax.broadcasted_iota(jnp.int32, (SAMPLE_ROWS, n_key), 1)
    dist = (q_off + tok) - kpos
    s = s - slope * dist.astype(F32)
    chosen = _dot(selx_ref[...], exp_ref[...]) > 0.5
    s = jnp.where(chosen & (dist >= 0), s, NEG)
    _softmax_update(s, vt, m_ref, l_ref, acc_ref)

    @pl.when(pg == n_steps - 1)
    def _():
        lane = lax.broadcasted_iota(jnp.int32, (SAMPLE_ROWS, LANES), 1)
        row = lax.broadcasted_iota(jnp.int32, (SAMPLE_ROWS, LANES), 0)

        def own_group(x):
            return jnp.where(lane < HEAD_DIM, jnp.where(row < 16, x, pltpu.roll(x, HEAD_DIM, axis=1)), 0.0)

        st = _tail_scores(q, knew_ref[...], t_new)
        _softmax_update(st, vnew_ref[...], m_ref, l_ref, acc_ref)
        o_sel = own_group(acc_ref[...] / l_ref[...])

        wb = wbuf_ref[...]
        wk = wb[:, 0:LANES].astype(BF16)
        wv = wb[:, LANES:2 * LANES].astype(BF16)
        sw = _dot_nt(q, wk)
        wslope, wtok = _row_slopes(SAMPLE_ROWS, WINDOW)
        wpos = (q_off - WINDOW) + lax.broadcasted_iota(jnp.int32, (SAMPLE_ROWS, WINDOW), 1)
        wd = (q_off + wtok) - wpos
        sw = sw - wslope * wd.astype(F32)
        sw = jnp.where((wd >= 0) & (wd < WINDOW) & (wpos >= 0), sw, NEG)
        swt = _tail_scores(q, wknew_ref[...], t_new)
        mw = jnp.maximum(jnp.max(sw, axis=1, keepdims=True), jnp.max(swt, axis=1, keepdims=True))
        pw = jnp.exp(sw - mw)
        pwt = jnp.exp(swt - mw)
        lw = jnp.sum(pw, axis=1, keepdims=True) + jnp.sum(pwt, axis=1, keepdims=True)
        ow = _dot(pw.astype(BF16), wv) + _dot(pwt.astype(BF16), wvnew_ref[...])
        o_win = own_group(ow / lw)

        gate = gate_ref[...]
        o_ref[...] = gate[:, 0:1] * ocmp_ref[...] + gate[:, 1:2] * o_sel + gate[:, 2:3] * o_win


def _sample_nsa(pool_rows, page_table, q32, selx, expand, knew, vnew, wbuf, wknew, wvnew, ocmp, gate32,
                n_pages, q_off, t_new):
    nb, pages_per_seq = page_table.shape
    n_steps = pages_per_seq // n_pages
    page_specs = [
        pl.BlockSpec((None, PAGE, 256), lambda b, pg, pt, j=j: (pt[b, pg * n_pages + j], 0, 1))
        for j in range(n_pages)
    ]
    per_b = lambda shape: pl.BlockSpec((None,) + shape, lambda b, pg, pt: (b,) + (0,) * len(shape))
    grid_spec = pltpu.PrefetchScalarGridSpec(
        num_scalar_prefetch=1,
        grid=(nb, n_steps),
        in_specs=page_specs + [
            per_b((SAMPLE_ROWS, LANES)),
            pl.BlockSpec((None, None, SAMPLE_ROWS, LANES), lambda b, pg, pt: (b, pg, 0, 0)),
            pl.BlockSpec((LANES, n_pages * PAGE), lambda b, pg, pt: (0, 0)),
            per_b((TAIL, LANES)), per_b((TAIL, LANES)),
            per_b((WINDOW, 256)),
            per_b((TAIL, LANES)), per_b((TAIL, LANES)),
            per_b((SAMPLE_ROWS, LANES)), per_b((SAMPLE_ROWS, LANES)),
        ],
        out_specs=per_b((SAMPLE_ROWS, LANES)),
        scratch_shapes=[
            pltpu.VMEM((SAMPLE_ROWS, 1), F32), pltpu.VMEM((SAMPLE_ROWS, 1), F32),
            pltpu.VMEM((SAMPLE_ROWS, LANES), F32),
        ],
    )
    return pl.pallas_call(
        functools.partial(_sample_nsa_kernel, n_pages=n_pages, n_steps=n_steps, q_off=q_off, t_new=t_new),
        grid_spec=grid_spec,
        out_shape=jax.ShapeDtypeStruct((nb, SAMPLE_ROWS, LANES), F32),
        compiler_params=_cparams(("parallel", "arbitrary")),
        name="sample_nsa",
    )(page_table, *([pool_rows] * n_pages), q32, selx, expand, knew, vnew, wbuf, wknew, wvnew, ocmp, gate32)


def _sample_fox_kernel(pt_ref, *refs, n_pages, n_steps, t_new):
    kv_refs = refs[:n_pages]
    lf_refs = refs[n_pages:2 * n_pages]
    (q_ref, tril_ref, rsel_ref, knew_ref, vnew_ref, lfnew_ref,
     o_ref, m_ref, l_ref, acc_ref, carry_ref) = refs[2 * n_pages:]
    pg = pl.program_id(1)
    q = q_ref[...]
    width = FOX_HEADS * HEAD_DIM

    @pl.when(pg == 0)
    def _():
        m_ref[...] = jnp.full_like(m_ref, NEG)
        l_ref[...] = jnp.zeros_like(l_ref)
        acc_ref[...] = jnp.zeros_like(acc_ref)
        carry_ref[...] = jnp.zeros_like(carry_ref)

    def bias_rows(lf, tril):
        hi, mid, lo = _split3(lf)
        pre = _dot(tril, hi) + _dot(tril, mid) + _dot(tril, lo) + carry_ref[...]
        return pre, _dot3_nt(rsel_ref[...], pre)

    kt = jnp.concatenate([r[:, 0:width] for r in kv_refs], axis=0).astype(BF16)
    vt = jnp.concatenate([r[:, width:2 * width] for r in kv_refs], axis=0).astype(BF16)
    lf = jnp.concatenate([r[...] for r in lf_refs], axis=0)
    pre, bias = bias_rows(lf, tril_ref[...])
    s = _dot_nt(q, kt) - bias
    _softmax_update(s, vt, m_ref, l_ref, acc_ref)
    n_key = n_pages * PAGE
    carry_ref[...] = pre[n_key - 1:n_key, :]

    @pl.when(pg == n_steps - 1)
    def _():
        pre_n, bias_n = bias_rows(lfnew_ref[...], tril_ref[0:TAIL, 0:TAIL])
        sn = _dot_nt(q, knew_ref[...]) - bias_n
        j = lax.broadcasted_iota(jnp.int32, sn.shape, 1)
        tok = lax.broadcasted_iota(jnp.int32, sn.shape, 0) & 3
        sn = jnp.where((j <= tok) & (j < t_new), sn, NEG)
        _softmax_update(sn, vnew_ref[...], m_ref, l_ref, acc_ref)
        o_ref[...] = acc_ref[...] / l_ref[...]


def _sample_fox(pool_kv, pool_lf, page_table, qbd, tril, rsel, knew, vnew, lfnew, n_pages, t_new):
    nb, pages_per_seq = page_table.shape
    n_steps = pages_per_seq // n_pages
    width = FOX_HEADS * HEAD_DIM
    kv_specs = [
        pl.BlockSpec((None, PAGE, 2 * width), lambda b, pg, pt, j=j: (pt[b, pg * n_pages + j], 0, 0))
        for j in range(n_pages)
    ]
    lf_specs = [
        pl.BlockSpec((None, PAGE, FOX_HEADS), lambda b, pg, pt, j=j: (pt[b, pg * n_pages + j], 0, 0))
        for j in range(n_pages)
    ]
    per_b = lambda shape: pl.BlockSpec((None,) + shape, lambda b, pg, pt: (b,) + (0,) * len(shape))
    n_key = n_pages * PAGE
    grid_spec = pltpu.PrefetchScalarGridSpec(
        num_scalar_prefetch=1,
        grid=(nb, n_steps),
        in_specs=kv_specs + lf_specs + [
            per_b((SAMPLE_ROWS, width)),
            pl.BlockSpec((n_key, n_key), lambda b, pg, pt: (0, 0)),
            pl.BlockSpec((SAMPLE_ROWS, FOX_HEADS), lambda b, pg, pt: (0, 0)),
            per_b((TAIL, width)), per_b((TAIL, width)), per_b((TAIL, FOX_HEADS)),
        ],
        out_specs=per_b((SAMPLE_ROWS, width)),
        scratch_shapes=[
            pltpu.VMEM((SAMPLE_ROWS, 1), F32), pltpu.VMEM((SAMPLE_ROWS, 1), F32),
            pltpu.VMEM((SAMPLE_ROWS, width), F32), pltpu.VMEM((1, FOX_HEADS), F32),
        ],
    )
    return pl.pallas_call(
        functools.partial(_sample_fox_kernel, n_pages=n_pages, n_steps=n_steps, t_new=t_new),
        grid_spec=grid_spec,
        out_shape=jax.ShapeDtypeStruct((nb, SAMPLE_ROWS, width), F32),
        compiler_params=_cparams(("parallel", "arbitrary")),
        name="sample_fox",
    )(page_table, *([pool_kv] * n_pages), *([pool_lf] * n_pages), qbd, tril, rsel, knew, vnew, lfnew)


def _alibi_key_cols(pos):
    tab = np.zeros((len(pos), LANES), np.float32)
    tab[:, ALIBI0] = 1.0
    tab[:, ALIBI0 + 1] = 1.0
    tab[:, ALIBI0 + 2] = (pos // 128) * 128
    tab[:, ALIBI0 + 3] = pos % 128
    return tab


def _prompt_consts(t):
    pos = np.arange(t)
    kaug_win = _alibi_key_cols(pos)
    kaug_sel = kaug_win.copy()
    kaug_sel[pos, SELBIT0 + pos // SEL_BLOCK] = -MASK_BIG
    n_chunk = t // CMP_STRIDE
    kaug_cmp = _alibi_key_cols(np.arange(n_chunk) * CMP_STRIDE + CMP_BLOCK - 1)
    n_cmp = (t - CMP_BLOCK) // CMP_STRIDE + 1
    n_slc = -(-t // SEL_BLOCK)
    ovlt = np.zeros((n_slc, n_chunk), np.float32)
    cs = np.arange(n_cmp)[None, :] * CMP_STRIDE
    ss = np.arange(n_slc)[:, None] * SEL_BLOCK
    ovlt[:, :n_cmp] = (cs < ss + SEL_BLOCK) & (cs + CMP_BLOCK > ss)
    tril = np.tril(np.ones((512, 512), np.float32))
    as_bf = lambda a: jnp.asarray(a, BF16)
    return dict(kaug_sel=as_bf(kaug_sel), kaug_win=as_bf(kaug_win), kaug_cmp=as_bf(kaug_cmp),
                ovlt=as_bf(ovlt), tril=as_bf(tril))


def _sample_ovl(n_key, n_cmp, n_slc, nb_pad):
    ovl = np.zeros((n_key, nb_pad), np.float32)
    cs = np.arange(n_cmp)[:, None] * CMP_STRIDE
    ss = np.arange(n_slc)[None, :] * SEL_BLOCK
    ovl[:n_cmp, :n_slc] = (cs < ss + SEL_BLOCK) & (cs + CMP_BLOCK > ss)
    return jnp.asarray(ovl, BF16)


def _perm_in_proj(w_in, b_in):
    o_qa, o_kva, o_ga, o_fox, o_fb, o_gm = 0, 512, 1280, 1304, 2840, 2848
    d = w_in.shape[0]
    o_fk = o_fox + FOX_HEADS * HEAD_DIM
    o_fv = o_fk + FOX_HEADS * HEAD_DIM
    o_sv = o_kva + 6 * HEAD_DIM
    o_wv = o_kva + 10 * HEAD_DIM

    def cols(x, zeros):
        return jnp.concatenate([x[..., o_kva:o_ga], x[..., o_fk:o_fb], x[..., o_gm:],
                                x[..., o_fb:o_gm], zeros(LANES - FOX_HEADS)], axis=-1)

    def rows(x, zeros):
        return jnp.concatenate([x[..., o_qa:o_kva], x[..., o_sv:o_sv + 2 * HEAD_DIM], x[..., o_wv:o_wv + 2 * HEAD_DIM],
                                x[..., o_fox:o_fk], x[..., o_fv:o_fb],
                                x[..., o_ga:o_fox], zeros(R_TOT - R_GA - 3 * NSA_HEADS)], axis=-1)

    w = cols(w_in, lambda k: jnp.zeros((d, k), w_in.dtype)).astype(BF16)
    b = cols(b_in, lambda k: jnp.zeros((k,), b_in.dtype)).reshape(1, C_TOT)
    wt = rows(w_in, lambda k: jnp.zeros((d, k), w_in.dtype)).T.astype(BF16)
    bt = rows(b_in, lambda k: jnp.zeros((k,), b_in.dtype)).reshape(R_TOT, 1)
    return w, b, wt, bt


def _compress_weights(pe_k, w1_k, w2_k, pe_v, w1_v, w2_v):
    n_sub = CMP_BLOCK // CMP_STRIDE

    def per_row(w1):
        w = w1.reshape(n_sub, CMP_STRIDE, HEAD_DIM, CMP_HIDDEN)
        return w.transpose(1, 2, 0, 3).reshape(CMP_STRIDE, HEAD_DIM, n_sub * CMP_HIDDEN)

    blocks = [per_row(w1_k), per_row(w1_k), per_row(w1_v), per_row(w1_v)]
    wr = jnp.zeros((CMP_STRIDE, 4, HEAD_DIM, 4, n_sub * CMP_HIDDEN), F32)
    for s, blk in enumerate(blocks):
        wr = wr.at[:, s, :, s, :].set(blk)
    wr = wr.reshape(CMP_STRIDE, 4 * HEAD_DIM, 4 * n_sub * CMP_HIDDEN).astype(BF16)
    pe8 = jnp.zeros((16, CMP_BLOCK * HEAD_DIM), F32).at[0].set(pe_k.reshape(-1)).at[1].set(pe_v.reshape(-1))
    w1cat = jnp.concatenate([w1_k, w1_v], axis=1).astype(BF16)
    pad = jnp.zeros((CMP_HIDDEN, LANES - HEAD_DIM), F32)
    w2pad = jnp.stack([jnp.concatenate([w2_k, pad], axis=1), jnp.concatenate([w2_v, pad], axis=1)]).astype(BF16)
    return dict(wr=wr, pe8=pe8.astype(BF16), w1cat=w1cat, w2pad=w2pad)


def _prompt_mixer(h1, w, nb, t):
    nsa, win, fox, logf, kh, qat, nvt, fqt, fvt, gat, gm = _in_proj(
        h1, w["w_in"], w["b_in"], w["wt"], w["bt"], nb, t, tm=256)
    consts = _prompt_consts(t)
    kcvc, vct = _compress_prompt(nsa, w["cmp"], nb, t)
    o_a = _nsa_prompt(qat, kh, nvt, kcvc, vct, gat, consts, nb, t, tq=512, tk=512)
    o_b = _fox_prompt(fqt, kh, fvt, logf, consts["tril"], nb, t, tq=1024, tk=512)
    n = nb * t
    return o_a.reshape(n, -1), o_b.reshape(n, -1), gm, (nsa, win, fox, logf)


def _sample_mixer(h1, w, nb, t_new, nsa_pool, fox_pool, logf_pool, win_buf, page_table):
    n = nb * t_new
    past = page_table.shape[1] * PAGE
    nsa, win, fox, logf, _, qat, _, fqt, _, gat, gm = _in_proj(
        h1, w["w_in"], w["b_in"], w["wt"], w["bt"], 1, n, tm=n)
    ga = gat[:3 * NSA_HEADS].T
    n_pool = nsa_pool.shape[0]
    nsa_rows = nsa_pool.reshape(n_pool, PAGE, 512)
    fox_rows = fox_pool.reshape(n_pool, PAGE, 2 * FOX_HEADS * HEAD_DIM)

    seq_len = past + t_new
    n_cmp = (seq_len - CMP_BLOCK) // CMP_STRIDE + 1
    n_slc = -(-seq_len // SEL_BLOCK)
    n_chunk = past // CMP_STRIDE
    assert n_cmp <= n_chunk
    kcvc = _compress_sample(nsa_rows, page_table, w["cmp"], n_pages=16)

    q = qat[0].reshape(NSA_HEADS, HEAD_DIM, nb, t_new).transpose(2, 0, 3, 1).reshape(nb, SAMPLE_ROWS, HEAD_DIM)
    q = jnp.pad(q, ((0, 0), (0, 0), (0, LANES - HEAD_DIM)))
    nb_pad = -(-n_slc // LANES) * LANES
    ocmp, sel = _sample_select(q, kcvc, _sample_ovl(n_chunk, n_cmp, n_slc, nb_pad), past, n_cmp, n_slc, t_new)

    n_pages = 8
    n_steps = past // (n_pages * PAGE)
    blk_per_step = n_pages * PAGE // SEL_BLOCK
    selg = sel.reshape(nb, 2, 8, nb_pad)[:, :, :t_new, :n_steps * blk_per_step]
    selg = selg.reshape(nb, 2, 1, t_new, n_steps, blk_per_step)
    selx = jnp.broadcast_to(selg, (nb, 2, NSA_GROUP, t_new, n_steps, blk_per_step))
    selx = selx.transpose(0, 4, 1, 2, 3, 5).reshape(nb, n_steps, SAMPLE_ROWS, blk_per_step)
    selx = jnp.pad(selx, ((0, 0), (0, 0), (0, 0), (0, LANES - blk_per_step))).astype(BF16)
    expand = np.zeros((LANES, n_pages * PAGE), np.float32)
    expand[np.arange(n_pages * PAGE) // SEL_BLOCK, np.arange(n_pages * PAGE)] = 1.0
    q_sel = jnp.concatenate([q[:, :16], jnp.roll(q[:, 16:], HEAD_DIM, axis=-1)], axis=1)

    def new_rows(x, c0):
        r = x.reshape(nb, t_new, -1)[:, :, c0:c0 + LANES]
        return jnp.pad(r, ((0, 0), (0, TAIL - t_new), (0, 0))).astype(BF16)

    gate32 = ga.reshape(nb, t_new, 3, NSA_HEADS).transpose(0, 3, 1, 2).reshape(nb, SAMPLE_ROWS, 3)
    gate32 = jnp.pad(gate32, ((0, 0), (0, 0), (0, LANES - 3)))
    wbuf = win_buf.reshape(nb, win_buf.shape[1], 256)
    o_a32 = _sample_nsa(nsa_rows, page_table, q_sel, selx, jnp.asarray(expand, BF16),
                        new_rows(nsa, 256), new_rows(nsa, 384), wbuf, new_rows(win, 0), new_rows(win, 128),
                        ocmp, gate32, n_pages, past, t_new)
    o_a = o_a32[:, :, :HEAD_DIM].reshape(nb, NSA_HEADS, t_new, HEAD_DIM).transpose(0, 2, 1, 3)
    o_a = o_a.reshape(n, NSA_HEADS * HEAD_DIM).astype(BF16)

    width = FOX_HEADS * HEAD_DIM
    qf = fqt[0].reshape(FOX_HEADS, HEAD_DIM, nb, t_new).transpose(2, 0, 3, 1)
    eye = jnp.eye(FOX_HEADS, dtype=qf.dtype)
    qbd = (qf[:, :, :, None, :] * eye[None, :, None, :, None]).reshape(nb, SAMPLE_ROWS, width)
    f_pages = 8
    tril = jnp.asarray(np.tril(np.ones((f_pages * PAGE, f_pages * PAGE), np.float32)), BF16)
    rsel = np.zeros((SAMPLE_ROWS, FOX_HEADS), np.float32)
    rsel[np.arange(SAMPLE_ROWS), np.arange(SAMPLE_ROWS) // t_new] = 1.0

    def new_fox(c0):
        r = fox.reshape(nb, t_new, 2 * width)[:, :, c0:c0 + width]
        return jnp.pad(r, ((0, 0), (0, TAIL - t_new), (0, 0))).astype(BF16)

    lfnew = jnp.pad(logf.reshape(nb, t_new, FOX_HEADS), ((0, 0), (0, TAIL - t_new), (0, 0)))
    o_b32 = _sample_fox(fox_rows, logf_pool, page_table, qbd, tril, jnp.asarray(rsel, BF16),
                        new_fox(0), new_fox(width), lfnew, f_pages, t_new)
    o_b = o_b32.reshape(nb, FOX_HEADS, t_new, FOX_HEADS, HEAD_DIM)
    o_b = jnp.einsum("bhtgd,hg->bthd", o_b, jnp.eye(FOX_HEADS, dtype=o_b.dtype))
    o_b = o_b.reshape(n, width).astype(BF16)
    return o_a, o_b, gm, (nsa, win, fox, logf)


def _layer(x, w, alpha, mixer, tm):
    h1 = _ffn_ln(x, w["ffn1_wg"], w["ffn1_wu"], w["ffn1_wd"], w["ln1_g"], w["ln1_b"], alpha, tm)
    o_a, o_b, gm, state = mixer(h1, w)
    h2 = _merge_ln(o_a, o_b, gm, h1, w["w_proj_a"], w["w_proj_b"], w["w_out"], w["ln2_g"], w["ln2_b"], alpha, tm)
    y = _ffn_ln(h2, w["ffn2_wg"], w["ffn2_wu"], w["ffn2_wd"], w["ln3_g"], w["ln3_b"], alpha, tm)
    return y, state


def kernel(x_prompt, x_sample, cache_nsa_kv, cache_fox_kv, cache_fox_logf, state_win_kv, page_table,
           ln1_g, ln1_b, ffn1_w_gate, ffn1_w_up, ffn1_w_down, w_in, b_in,
           cmp_pe_k, cmp_w1_k, cmp_w2_k, cmp_pe_v, cmp_w1_v, cmp_w2_v,
           w_proj_a, w_proj_b, w_out, ln2_g, ln2_b,
           ffn2_w_gate, ffn2_w_up, ffn2_w_down, ln3_g, ln3_b):
    depth = w_in.shape[0]
    nb_p, t_p, d = x_prompt.shape
    nb_s, t_s, _ = x_sample.shape
    alpha = float((2.0 * depth) ** 0.25)
    h_p = x_prompt.reshape(nb_p * t_p, d)
    h_s = x_sample.reshape(nb_s * t_s, d)
    st_p, st_s = [], []
    for l in range(depth):
        wi, bi, wt, bt = _perm_in_proj(w_in[l], b_in[l])
        vec = lambda a: a[l].reshape(1, d)
        w = dict(
            ln1_g=vec(ln1_g), ln1_b=vec(ln1_b), ln2_g=vec(ln2_g), ln2_b=vec(ln2_b), ln3_g=vec(ln3_g), ln3_b=vec(ln3_b),
            ffn1_wg=ffn1_w_gate[l].astype(BF16), ffn1_wu=ffn1_w_up[l].astype(BF16), ffn1_wd=ffn1_w_down[l].astype(BF16),
            ffn2_wg=ffn2_w_gate[l].astype(BF16), ffn2_wu=ffn2_w_up[l].astype(BF16), ffn2_wd=ffn2_w_down[l].astype(BF16),
            w_in=wi, b_in=bi, wt=wt, bt=bt,
            cmp=_compress_weights(cmp_pe_k[l], cmp_w1_k[l], cmp_w2_k[l], cmp_pe_v[l], cmp_w1_v[l], cmp_w2_v[l]),
            w_proj_a=w_proj_a[l].astype(BF16), w_proj_b=w_proj_b[l].astype(BF16), w_out=w_out[l].astype(BF16),
        )
        h_p, s_p = _layer(h_p, w, alpha, functools.partial(_prompt_mixer, nb=nb_p, t=t_p), tm=512)
        h_s, s_s = _layer(h_s, w, alpha, functools.partial(
            _sample_mixer, nb=nb_s, t_new=t_s, nsa_pool=cache_nsa_kv[l], fox_pool=cache_fox_kv[l],
            logf_pool=cache_fox_logf[l], win_buf=state_win_kv[l], page_table=page_table), tm=nb_s * t_s)
        st_p.append(s_p)
        st_s.append(s_s)

    def states(st, nb, t, win_prev):
        nsa = jnp.stack([s[0].reshape(nb, t, 4, NSA_KV_HEADS, HEAD_DIM) for s in st])
        fox = jnp.stack([s[2].reshape(nb, t, 2, FOX_HEADS, HEAD_DIM) for s in st])
        logf = jnp.stack([s[3].reshape(nb, t, FOX_HEADS) for s in st])
        wins = []
        for l, s in enumerate(st):
            wr = s[1].reshape(nb, t, 2, NSA_KV_HEADS, HEAD_DIM)
            if win_prev is None:
                wins.append(wr[:, t - min(WINDOW, t):])
            else:
                wins.append(jnp.concatenate([win_prev[l], wr], axis=1)[:, t:])
        return nsa, fox, logf, jnp.stack(wins)

    nsa_p, fox_p, logf_p, win_p = states(st_p, nb_p, t_p, None)
    nsa_s, fox_s, logf_s, win_s = states(st_s, nb_s, t_s, state_win_kv)
    return (h_p.reshape(nb_p, t_p, d), h_s.reshape(nb_s, t_s, d), nsa_p, fox_p, logf_p, win_p,
            nsa_s, fox_s, logf_s, win_s)
```

```python
import functools

import numpy as np
import jax
import jax.numpy as jnp
from jax import lax
from jax.experimental import pallas as pl
from jax.experimental.pallas import tpu as pltpu

F32 = jnp.float32
BF16 = jnp.bfloat16

LANES = 128
VMEM_LIMIT = 56 * 1024 * 1024

D_MODEL = 1024
HEAD_DIM = 64
NSA_HEADS = 8
NSA_KV_HEADS = 2
NSA_GROUP = NSA_HEADS // NSA_KV_HEADS
FOX_HEADS = 8
CMP_BLOCK = 32
CMP_STRIDE = 16
CMP_HIDDEN = 2 * HEAD_DIM
SEL_BLOCK = 64
N_SEL = 16
WINDOW = 512
PAGE = 128
LN_EPS = 1e-5
NEG = -1e30
FORCE = 1e9
SCALE = HEAD_DIM ** -0.5
MASK_BIG = 30000.0

SELBIT0 = 64
ALIBI0 = 96


def _dot(a, b):
    return jnp.dot(a, b, preferred_element_type=F32)


def _dot_nt(a, b):
    return lax.dot_general(a, b, (((1,), (1,)), ((), ())), preferred_element_type=F32)


def _split3(x):
    hi = x.astype(BF16)
    r1 = x - hi.astype(F32)
    mid = r1.astype(BF16)
    lo = (r1 - mid.astype(F32)).astype(BF16)
    return hi, mid, lo


def _dot3(x, w):
    hi, mid, lo = _split3(x)
    return _dot(hi, w) + _dot(mid, w) + _dot(lo, w)


def _dot3_nt(a, x):
    hi, mid, lo = _split3(x)
    return _dot_nt(a, hi) + _dot_nt(a, mid) + _dot_nt(a, lo)


def _sigmoid(x):
    return 1.0 / (1.0 + jnp.exp(-x))


def _log_sigmoid(x):
    return -(jnp.maximum(-x, 0.0) + jnp.log(1.0 + jnp.exp(-jnp.abs(x))))


def _gelu_tanh(x):
    c = np.float32(np.sqrt(2.0 / np.pi))
    return 0.5 * x * (1.0 + jnp.tanh(c * (x + np.float32(0.044715) * (x * x * x))))


def _cparams(sem):
    return pltpu.CompilerParams(dimension_semantics=sem, vmem_limit_bytes=VMEM_LIMIT)


def _ffn_ln_kernel(x_ref, wg_ref, wu_ref, wd_ref, g_ref, b_ref, o_ref, acc_ref, *, alpha, n_ff):
    j = pl.program_id(1)

    @pl.when(j == 0)
    def _():
        acc_ref[...] = jnp.zeros_like(acc_ref)

    xb = x_ref[...].astype(BF16)
    gate = _dot(xb, wg_ref[...])
    up = _dot(xb, wu_ref[...])
    mid = (gate * _sigmoid(gate) * up).astype(BF16)
    acc_ref[...] += _dot(mid, wd_ref[...])

    @pl.when(j == n_ff - 1)
    def _():
        y = alpha * x_ref[...] + 0.5 * acc_ref[...]
        mu = jnp.mean(y, axis=-1, keepdims=True)
        d = y - mu
        var = jnp.mean(d * d, axis=-1, keepdims=True)
        o_ref[...] = d * lax.rsqrt(var + LN_EPS) * g_ref[...] + b_ref[...]


def _ffn_ln(x, wg, wu, wd, g, b, alpha, tm):
    n, d = x.shape
    dff = wg.shape[1]
    tf = dff // 2
    n_ff = dff // tf
    return pl.pallas_call(
        functools.partial(_ffn_ln_kernel, alpha=alpha, n_ff=n_ff),
        grid=(n // tm, n_ff),
        in_specs=[
            pl.BlockSpec((tm, d), lambda i, j: (i, 0)),
            pl.BlockSpec((d, tf), lambda i, j: (0, j)),
            pl.BlockSpec((d, tf), lambda i, j: (0, j)),
            pl.BlockSpec((tf, d), lambda i, j: (j, 0)),
            pl.BlockSpec((1, d), lambda i, j: (0, 0)),
            pl.BlockSpec((1, d), lambda i, j: (0, 0)),
        ],
        out_specs=pl.BlockSpec((tm, d), lambda i, j: (i, 0)),
        out_shape=jax.ShapeDtypeStruct((n, d), F32),
        scratch_shapes=[pltpu.VMEM((tm, d), F32)],
        compiler_params=_cparams(("parallel", "arbitrary")),
        name="ffn_ln",
    )(x, wg, wu, wd, g, b)


C_CMP = 0
C_KH = 256
C_GM = 1024
C_FB = 3072
C_TOT = 3200
R_QA = 0
R_KVA = 512
R_FQ = 1280
R_FKV = 1792
R_GA = 2816
R_FB = 2848
R_TOT = 2864
N_KH = 12
VT_ROWS = 80


def _head_pad(z, h):
    col = z[:, LANES * (h // 2):LANES * (h // 2 + 1)]
    if h % 2:
        col = pltpu.roll(col, HEAD_DIM, axis=1)
    lane = lax.broadcasted_iota(jnp.int32, col.shape, 1)
    return jnp.where(lane < HEAD_DIM, col, 0.0)


def _ones_row_block(width):
    sub = lax.broadcasted_iota(jnp.int32, (VT_ROWS - HEAD_DIM, width), 0)
    return jnp.where(sub == 0, 1.0, 0.0)


def _in_proj_kernel(h_ref, w_ref, b_ref, wt_ref, bt_ref,
                    cmp_ref, logf_ref, kh_ref, gm_ref,
                    nsat_ref, wint_ref, foxt_ref, logft_ref,
                    qat_ref, nvt_ref, fqt_ref, fvt_ref, gat_ref):
    hb = h_ref[...].astype(BF16)
    tm = hb.shape[0]

    def proj(c0, c1):
        return _dot(hb, w_ref[:, c0:c1]) + b_ref[:, c0:c1]

    def proj_t(r0, r1):
        return _dot_nt(wt_ref[r0:r1, :], hb) + bt_ref[r0:r1, :]

    cmp_ref[...] = proj(C_CMP, C_KH)
    zk = proj(C_KH, C_GM)
    for h in range(N_KH):
        kh_ref[h] = _head_pad(zk, h).astype(BF16)
    gm_ref[...] = _sigmoid(proj(C_GM, C_FB))
    logf_ref[...] = _log_sigmoid(proj(C_FB, C_TOT))[:, :FOX_HEADS]

    ones_blk = _ones_row_block(tm)

    def value_head(zt, r0):
        return jnp.concatenate([zt[r0:r0 + HEAD_DIM], ones_blk], axis=0).astype(BF16)

    qat_ref[...] = (proj_t(R_QA, R_KVA) * SCALE).astype(BF16).reshape(qat_ref.shape)

    zkv = proj_t(R_KVA, R_FQ)
    nsat_ref[...] = zkv[:512]
    wint_ref[...] = zkv[512:]
    for j, head in enumerate((6, 7, 10, 11)):
        nvt_ref[j] = value_head(zkv, head * HEAD_DIM)

    fqt_ref[...] = (proj_t(R_FQ, R_FKV) * SCALE).astype(BF16).reshape(fqt_ref.shape)
    zf = proj_t(R_FKV, R_GA)
    foxt_ref[...] = zf
    for h in range(FOX_HEADS):
        fvt_ref[h] = value_head(zf, (FOX_HEADS + h) * HEAD_DIM)

    gat_ref[...] = _sigmoid(proj_t(R_GA, R_FB))
    logft_ref[...] = _log_sigmoid(proj_t(R_FB, R_TOT))[:FOX_HEADS]


def _in_proj(h, w, b, wt, bt, nseq, tseq, tm):
    n, d = h.shape
    tps = tseq // tm
    row = lambda i: (i, 0)
    headmaj = lambda i: (i // tps, 0, i % tps, 0)
    featmaj = lambda i: (i // tps, 0, 0, i % tps)
    feat3 = lambda i: (i // tps, 0, i % tps)
    n_ga = R_FB - R_GA
    out_shape = [
        jax.ShapeDtypeStruct((n, 256), F32),
        jax.ShapeDtypeStruct((n, FOX_HEADS), F32),
        jax.ShapeDtypeStruct((nseq, N_KH, tseq, LANES), BF16),
        jax.ShapeDtypeStruct((n, 2 * D_MODEL), F32),
        jax.ShapeDtypeStruct((nseq, 512, tseq), F32),
        jax.ShapeDtypeStruct((nseq, 256, tseq), F32),
        jax.ShapeDtypeStruct((nseq, 1024, tseq), F32),
        jax.ShapeDtypeStruct((nseq, FOX_HEADS, tseq), F32),
        jax.ShapeDtypeStruct((nseq, NSA_HEADS, HEAD_DIM, tseq), BF16),
        jax.ShapeDtypeStruct((nseq, 4, VT_ROWS, tseq), BF16),
        jax.ShapeDtypeStruct((nseq, FOX_HEADS, HEAD_DIM, tseq), BF16),
        jax.ShapeDtypeStruct((nseq, FOX_HEADS, VT_ROWS, tseq), BF16),
        jax.ShapeDtypeStruct((n_ga, n), F32),
    ]
    out_specs = [
        pl.BlockSpec((tm, 256), row),
        pl.BlockSpec((tm, FOX_HEADS), row),
        pl.BlockSpec((None, N_KH, tm, LANES), headmaj),
        pl.BlockSpec((tm, 2 * D_MODEL), row),
        pl.BlockSpec((None, 512, tm), feat3),
        pl.BlockSpec((None, 256, tm), feat3),
        pl.BlockSpec((None, 1024, tm), feat3),
        pl.BlockSpec((None, FOX_HEADS, tm), feat3),
        pl.BlockSpec((None, NSA_HEADS, HEAD_DIM, tm), featmaj),
        pl.BlockSpec((None, 4, VT_ROWS, tm), featmaj),
        pl.BlockSpec((None, FOX_HEADS, HEAD_DIM, tm), featmaj),
        pl.BlockSpec((None, FOX_HEADS, VT_ROWS, tm), featmaj),
        pl.BlockSpec((n_ga, tm), lambda i: (0, i)),
    ]
    return pl.pallas_call(
        _in_proj_kernel,
        grid=(n // tm,),
        in_specs=[
            pl.BlockSpec((tm, d), row),
            pl.BlockSpec((d, C_TOT), lambda i: (0, 0)),
            pl.BlockSpec((1, C_TOT), lambda i: (0, 0)),
            pl.BlockSpec((R_TOT, d), lambda i: (0, 0)),
            pl.BlockSpec((R_TOT, 1), lambda i: (0, 0)),
        ],
        out_specs=out_specs,
        out_shape=out_shape,
        compiler_params=_cparams(("parallel",)),
        name="in_proj",
    )(h, w, b, wt, bt)


def _cmp_parts(k_refs, v_refs, wr_ref, rows_per_ref):
    n_chunk = rows_per_ref // CMP_STRIDE
    acc = None
    for r in range(CMP_STRIDE):
        def rows(refs):
            xs = [ref[pl.ds(r, n_chunk, stride=CMP_STRIDE), :] for ref in refs]
            return xs[0] if len(xs) == 1 else jnp.concatenate(xs, axis=0)
        x = jnp.concatenate([rows(k_refs), rows(v_refs)], axis=1)
        p = _dot(x.astype(BF16), wr_ref[r])
        acc = p if acc is None else acc + p
    return acc


def _cmp_finish(parts, bias8, w2_ref, out_ref, vt_ref=None):
    n = parts.shape[0]
    for s in range(4):
        kind = s // 2
        pa = parts[:, 256 * s:256 * s + CMP_HIDDEN]
        pb = parts[:, 256 * s + CMP_HIDDEN:256 * (s + 1)]
        pb = pltpu.roll(pb, n - 1, axis=0)
        bias = bias8[kind:kind + 1, CMP_HIDDEN * kind:CMP_HIDDEN * (kind + 1)]
        hid = bias + pa + pb
        c = _dot(_gelu_tanh(hid).astype(BF16), w2_ref[kind])
        out_ref[s] = c.astype(BF16)
        if vt_ref is not None and kind == 1:
            vt = jnp.concatenate([c.T[:HEAD_DIM], _ones_row_block(n)], axis=0)
            vt_ref[s - 2] = vt.astype(BF16)


def _cmp_bias(pe_ref, w1_ref):
    return _dot(pe_ref[...], w1_ref[...])


def _compress_prompt_kernel(k_ref, v_ref, wr_ref, pe_ref, w1_ref, w2_ref, out_ref, vt_ref, *, t):
    parts = _cmp_parts([k_ref], [v_ref], wr_ref, t)
    _cmp_finish(parts, _cmp_bias(pe_ref, w1_ref), w2_ref, out_ref, vt_ref)


def _compress_prompt(nsa_rows, cw, nb, t):
    n_chunk = t // CMP_STRIDE
    return pl.pallas_call(
        functools.partial(_compress_prompt_kernel, t=t),
        grid=(nb,),
        in_specs=[
            pl.BlockSpec((t, LANES), lambda b: (b, 0)),
            pl.BlockSpec((t, LANES), lambda b: (b, 1)),
            pl.BlockSpec((CMP_STRIDE, 256, 1024), lambda b: (0, 0, 0)),
            pl.BlockSpec((16, 2048), lambda b: (0, 0)),
            pl.BlockSpec((2048, 256), lambda b: (0, 0)),
            pl.BlockSpec((2, CMP_HIDDEN, LANES), lambda b: (0, 0, 0)),
        ],
        out_specs=[
            pl.BlockSpec((None, 4, n_chunk, LANES), lambda b: (b, 0, 0, 0)),
            pl.BlockSpec((None, 2, VT_ROWS, n_chunk), lambda b: (b, 0, 0, 0)),
        ],
        out_shape=[
            jax.ShapeDtypeStruct((nb, 4, n_chunk, LANES), BF16),
            jax.ShapeDtypeStruct((nb, 2, VT_ROWS, n_chunk), BF16),
        ],
        compiler_params=_cparams(("parallel",)),
        name="compress_prompt",
    )(nsa_rows, nsa_rows, cw["wr"], cw["pe8"], cw["w1cat"], cw["w2pad"])


def _compress_sample_kernel(pt_ref, *refs, n_pages, n_steps):
    k_refs = refs[:n_pages]
    v_refs = refs[n_pages:2 * n_pages]
    wr_ref, pe_ref, w1_ref, w2_ref, out_ref, parts_ref, krows_ref, vrows_ref = refs[2 * n_pages:]
    pg = pl.program_id(1)
    rows = n_pages * (PAGE // CMP_STRIDE)
    for j in range(n_pages):
        krows_ref[j * PAGE:(j + 1) * PAGE, :] = k_refs[j][...].T
        vrows_ref[j * PAGE:(j + 1) * PAGE, :] = v_refs[j][...].T
    parts = _cmp_parts([krows_ref], [vrows_ref], wr_ref, n_pages * PAGE)
    parts_ref[pl.ds(pl.multiple_of(pg * rows, rows), rows), :] = parts

    @pl.when(pg == n_steps - 1)
    def _():
        _cmp_finish(parts_ref[...], _cmp_bias(pe_ref, w1_ref), w2_ref, out_ref)


def _compress_sample(pool_t, page_table, cw, n_pages):
    nb, pages_per_seq = page_table.shape
    n_steps = pages_per_seq // n_pages
    n_chunk = pages_per_seq * (PAGE // CMP_STRIDE)
    page_specs = [
        pl.BlockSpec((None, None, LANES, PAGE), lambda b, pg, pt, j=j, c=c: (pt[b, pg * n_pages + j], c, 0, 0))
        for c in range(2) for j in range(n_pages)
    ]
    grid_spec = pltpu.PrefetchScalarGridSpec(
        num_scalar_prefetch=1,
        grid=(nb, n_steps),
        in_specs=page_specs + [
            pl.BlockSpec((CMP_STRIDE, 256, 1024), lambda b, pg, pt: (0, 0, 0)),
            pl.BlockSpec((16, 2048), lambda b, pg, pt: (0, 0)),
            pl.BlockSpec((2048, 256), lambda b, pg, pt: (0, 0)),
            pl.BlockSpec((2, CMP_HIDDEN, LANES), lambda b, pg, pt: (0, 0, 0)),
        ],
        out_specs=pl.BlockSpec((None, 4, n_chunk, LANES), lambda b, pg, pt: (b, 0, 0, 0)),
        scratch_shapes=[pltpu.VMEM((n_chunk, 1024), F32),
                        pltpu.VMEM((n_pages * PAGE, LANES), F32), pltpu.VMEM((n_pages * PAGE, LANES), F32)],
    )
    return pl.pallas_call(
        functools.partial(_compress_sample_kernel, n_pages=n_pages, n_steps=n_steps),
        grid_spec=grid_spec,
        out_shape=jax.ShapeDtypeStruct((nb, 4, n_chunk, LANES), BF16),
        compiler_params=_cparams(("parallel", "arbitrary")),
        name="compress_sample",
    )(page_table, *([pool_t] * (2 * n_pages)), cw["wr"], cw["pe8"], cw["w1cat"], cw["w2pad"])


def _softmax_update(s, v, m_ref, l_ref, acc_ref):
    m_prev = m_ref[...]
    m_new = jnp.maximum(m_prev, jnp.max(s, axis=1, keepdims=True))
    a = jnp.exp(m_prev - m_new)
    p = jnp.exp(s - m_new)
    l_ref[...] = a * l_ref[...] + jnp.sum(p, axis=1, keepdims=True)
    acc_ref[...] = a * acc_ref[...] + _dot(p.astype(BF16), v)
    m_ref[...] = m_new


def _softmax_update_t(s, vt, m_ref, acc_ref):
    m_prev = m_ref[...]
    m_new = jnp.maximum(m_prev, jnp.max(s, axis=0, keepdims=True))
    a = jnp.exp(m_prev - m_new)
    p = jnp.exp(s - m_new).astype(BF16)
    acc_ref[...] = a * acc_ref[...] + _dot(vt, p)
    m_ref[...] = m_new


def _softmax_reset(m_ref, acc_ref):
    m_ref[...] = jnp.full_like(m_ref, NEG)
    acc_ref[...] = jnp.zeros_like(acc_ref)


def _softmax_result(acc_ref):
    acc = acc_ref[...]
    return acc[:HEAD_DIM] / acc[HEAD_DIM:HEAD_DIM + 1]


def _nsa_prompt_kernel(q_ref, selk_ref, selvt_ref, wink_ref, winvt_ref, kc_ref, vct_ref, gat_ref,
                       kaug_sel_ref, kaug_win_ref, kaug_cmp_ref, ovlt_ref,
                       o_ref, ksel_s, kwin_s, m_ref, acc_ref, *, tq, tk, n_cmp, n_slc):
    g = pl.program_id(1)
    qt = pl.program_id(2)
    cols = NSA_GROUP * tq
    q0 = qt * tq

    @pl.when(qt == 0)
    def _():
        ksel_s[...] = selk_ref[...] + kaug_sel_ref[...]
        kwin_s[...] = wink_ref[...] + kaug_win_ref[...]

    qt4 = jnp.concatenate([q_ref[r] for r in range(NSA_GROUP)], axis=1)

    def head_and_pos(shape):
        lane = lax.broadcasted_iota(jnp.int32, shape, 1)
        return lane >> (tq.bit_length() - 1), q0 + (lane & (tq - 1))

    sub = lax.broadcasted_iota(jnp.int32, (16, cols), 0)
    r, t = head_and_pos((16, cols))
    sl0 = jnp.where(r == 0, 0.5, jnp.where(r == 1, 0.25, jnp.where(r == 2, 0.125, 0.0625))).astype(F32)
    slope = jnp.where(g == 0, sl0, sl0 * 0.0625)
    t_hi = ((t >> 7) << 7).astype(F32)
    t_lo = (t & 127).astype(F32)
    al = jnp.where(sub == 0, -slope * t_hi,
                   jnp.where(sub == 1, -slope * t_lo,
                             jnp.where((sub == 2) | (sub == 3), slope, 0.0))).astype(BF16)
    zeros16 = jnp.zeros((16, cols), BF16)

    def query_cols(selbits):
        return jnp.concatenate([qt4, selbits, al, zeros16], axis=0)

    qa = query_cols(jnp.zeros((n_slc, cols), BF16))

    n_chunk = kc_ref.shape[0]
    kc = kc_ref[...] + kaug_cmp_ref[...]
    s = _dot(kc, qa)
    nidx = lax.broadcasted_iota(jnp.int32, (n_chunk, cols), 0)
    _, t_c = head_and_pos((n_chunk, cols))
    valid = (t_c >= nidx * CMP_STRIDE + (CMP_BLOCK - 1)) & (nidx < n_cmp)
    s = jnp.where(valid, s, NEG)
    m = jnp.max(s, axis=0, keepdims=True)
    p = jnp.where(valid, jnp.exp(s - m), 0.0)
    den = jnp.sum(p, axis=0, keepdims=True)
    p = p / jnp.where(den > 0.0, den, 1.0)
    o_cmp = _dot(vct_ref[...], p.astype(BF16))[:HEAD_DIM]

    imp = p[:, 0:tq] + p[:, tq:2 * tq] + p[:, 2 * tq:3 * tq] + p[:, 3 * tq:4 * tq]
    hi, mid, lo = _split3(imp)
    ovlt = ovlt_ref[...]
    imp_slc = _dot(ovlt, hi) + _dot(ovlt, mid) + _dot(ovlt, lo)
    blk = lax.broadcasted_iota(jnp.int32, (n_slc, tq), 0)
    t1 = q0 + lax.broadcasted_iota(jnp.int32, (n_slc, tq), 1)
    cur = t1 >> 6
    forced = (blk == 0) | (blk == cur) | (blk == cur - 1)
    score = jnp.where(forced, FORCE, jnp.where(blk * SEL_BLOCK <= t1, imp_slc, -FORCE))
    rank = jnp.zeros((n_slc, tq), F32)
    for i in range(n_slc):
        row = score[i:i + 1, :]
        beats = (row > score) | ((row == score) & (blk > i))
        rank = rank + jnp.where(beats, 1.0, 0.0)
    notsel = jnp.where(rank >= float(N_SEL), 1.0, 0.0).astype(BF16)
    qs = query_cols(jnp.concatenate([notsel] * NSA_GROUP, axis=1))

    kpos_iota = lax.broadcasted_iota(jnp.int32, (tk, cols), 0)
    _, t_row = head_and_pos((1, cols))
    n_full = q0 // tk

    def branch(qx, k_s, vt_ref, kt_lo, window):
        _softmax_reset(m_ref, acc_ref)

        def tile(k0, causal):
            sc = _dot(k_s[pl.ds(k0, tk), :], qx)
            rel = t_row - k0
            if causal:
                sc = jnp.where(kpos_iota <= rel, sc, NEG)
            elif window:
                sc = jnp.where(kpos_iota > rel - WINDOW, sc, NEG)
            _softmax_update_t(sc, vt_ref[:, pl.ds(k0, tk)], m_ref, acc_ref)

        def body(kt, carry):
            tile(pl.multiple_of(kt * tk, tk), False)
            return carry

        lax.fori_loop(kt_lo, n_full, body, 0)
        for j in range(tq // tk):
            tile(pl.multiple_of(q0 + j * tk, tk), True)
        return _softmax_result(acc_ref)

    o_sel = branch(qs, ksel_s, selvt_ref, 0, False)
    o_win = branch(qa, kwin_s, winvt_ref, jnp.maximum(q0 - WINDOW, 0) // tk, True)

    outs = []
    for rr in range(NSA_GROUP):
        def gate(br):
            return gat_ref[pl.ds(br * NSA_HEADS + g * NSA_GROUP + rr, 1), :]
        sl = slice(rr * tq, (rr + 1) * tq)
        outs.append(gate(0) * o_cmp[:, sl] + gate(1) * o_sel[:, sl] + gate(2) * o_win[:, sl])
    o_ref[...] = jnp.concatenate(outs, axis=0).T.astype(BF16)


def _nsa_prompt(qat, kh, nvt, kcvc, vct, gat, consts, nb, t, tq, tk):
    n_cmp = (t - CMP_BLOCK) // CMP_STRIDE + 1
    n_slc = -(-t // SEL_BLOCK)
    n_chunk = t // CMP_STRIDE
    assert n_slc == 32 and tq % tk == 0 and tq <= WINDOW
    q5 = qat.reshape(nb, NSA_KV_HEADS, NSA_GROUP, HEAD_DIM, t)
    k_spec = lambda off: pl.BlockSpec((None, None, t, LANES), lambda b, g, qt: (b, off + g, 0, 0))
    vt_spec = lambda off: pl.BlockSpec((None, None, VT_ROWS, t), lambda b, g, qt: (b, off + g, 0, 0))
    const2 = lambda shape: pl.BlockSpec(shape, lambda b, g, qt: (0, 0))
    cols = NSA_GROUP * tq
    return pl.pallas_call(
        functools.partial(_nsa_prompt_kernel, tq=tq, tk=tk, n_cmp=n_cmp, n_slc=n_slc),
        grid=(nb, NSA_KV_HEADS, t // tq),
        in_specs=[
            pl.BlockSpec((None, None, NSA_GROUP, HEAD_DIM, tq), lambda b, g, qt: (b, g, 0, 0, qt)),
            k_spec(0), vt_spec(0), k_spec(2), vt_spec(2),
            pl.BlockSpec((None, None, n_chunk, LANES), lambda b, g, qt: (b, g, 0, 0)),
            pl.BlockSpec((None, None, VT_ROWS, n_chunk), lambda b, g, qt: (b, g, 0, 0)),
            pl.BlockSpec((R_FB - R_GA, tq), lambda b, g, qt: (0, b * (t // tq) + qt)),
            const2((t, LANES)), const2((t, LANES)), const2((n_chunk, LANES)), const2((n_slc, n_chunk)),
        ],
        out_specs=pl.BlockSpec((None, tq, NSA_GROUP * HEAD_DIM), lambda b, g, qt: (b, qt, g)),
        out_shape=jax.ShapeDtypeStruct((nb, t, NSA_HEADS * HEAD_DIM), BF16),
        scratch_shapes=[
            pltpu.VMEM((t, LANES), BF16), pltpu.VMEM((t, LANES), BF16),
            pltpu.VMEM((1, cols), F32), pltpu.VMEM((VT_ROWS, cols), F32),
        ],
        compiler_params=_cparams(("parallel", "parallel", "arbitrary")),
        name="nsa_prompt",
    )(q5, kh, nvt, kh, nvt, kcvc, vct, gat,
      consts["kaug_sel"], consts["kaug_win"], consts["kaug_cmp"], consts["ovlt"])


FOX_BIAS0 = 64
FOX_PAIR = 2


def _fox_prompt_kernel(q_ref, k_ref, vt_ref, lf_ref, tril_ref, o_ref, kaug_s, m_ref, acc_ref, *, tq, tk):
    hp = pl.program_id(1)
    qt = pl.program_id(2)
    q0 = qt * tq
    t = k_ref.shape[1]

    @pl.when(qt == 0)
    def _():
        tb = tril_ref.shape[0]
        tril = tril_ref[...]
        carry = jnp.zeros((1, FOX_HEADS), F32)
        cs = []
        for i in range(t // tb):
            hi, mid, lo = _split3(lf_ref[i * tb:(i + 1) * tb, :])
            c = _dot(tril, hi) + _dot(tril, mid) + _dot(tril, lo) + carry
            cs.append(c)
            carry = c[tb - 1:tb, :]
        hi, mid, lo = _split3(-jnp.concatenate(cs, axis=0))
        hrow = lax.broadcasted_iota(jnp.int32, (FOX_HEADS, LANES), 0)
        lane = lax.broadcasted_iota(jnp.int32, (FOX_HEADS, LANES), 1)
        for i in range(FOX_PAIR):
            head = FOX_PAIR * hp + i

            def place(j):
                return jnp.where((hrow == head) & (lane == FOX_BIAS0 + j), 1.0, 0.0).astype(BF16)

            aug = _dot(hi, place(0)) + _dot(mid, place(1)) + _dot(lo, place(2))
            kaug_s[i] = k_ref[i] + aug.astype(BF16)

    sub = lax.broadcasted_iota(jnp.int32, (16, tq), 0)
    ones3 = jnp.where(sub < 3, 1.0, 0.0).astype(BF16)
    zeros = jnp.zeros((LANES - HEAD_DIM - 16, tq), BF16)
    qx = [jnp.concatenate([q_ref[i], ones3, zeros], axis=0) for i in range(FOX_PAIR)]

    kpos_iota = lax.broadcasted_iota(jnp.int32, (tk, tq), 0)
    t_row = q0 + lax.broadcasted_iota(jnp.int32, (1, tq), 1)
    for i in range(FOX_PAIR):
        _softmax_reset(m_ref.at[i], acc_ref.at[i])

    def tile(k0, masked):
        for i in range(FOX_PAIR):
            sc = _dot(kaug_s[i, pl.ds(k0, tk), :], qx[i])
            if masked:
                sc = jnp.where(kpos_iota <= t_row - k0, sc, NEG)
            _softmax_update_t(sc, vt_ref[i, :, pl.ds(k0, tk)], m_ref.at[i], acc_ref.at[i])

    def body(kt, carry):
        tile(pl.multiple_of(kt * tk, tk), False)
        return carry

    lax.fori_loop(0, q0 // tk, body, 0)
    for j in range(tq // tk):
        tile(pl.multiple_of(q0 + j * tk, tk), True)
    o_t = jnp.concatenate([_softmax_result(acc_ref.at[i]) for i in range(FOX_PAIR)], axis=0)
    o_ref[...] = o_t.T.astype(BF16)


def _fox_prompt(fqt, kh, fvt, logf, tril, nb, t, tq, tk):
    n_pair = FOX_HEADS // FOX_PAIR
    q5 = fqt.reshape(nb, n_pair, FOX_PAIR, HEAD_DIM, t)
    k5 = kh.reshape(nb, N_KH // FOX_PAIR, FOX_PAIR, t, LANES)
    fox_pair0 = (N_KH - FOX_HEADS) // FOX_PAIR
    v5 = fvt.reshape(nb, n_pair, FOX_PAIR, VT_ROWS, t)
    return pl.pallas_call(
        functools.partial(_fox_prompt_kernel, tq=tq, tk=tk),
        grid=(nb, n_pair, t // tq),
        in_specs=[
            pl.BlockSpec((None, None, FOX_PAIR, HEAD_DIM, tq), lambda b, hp, qt: (b, hp, 0, 0, qt)),
            pl.BlockSpec((None, None, FOX_PAIR, t, LANES), lambda b, hp, qt: (b, fox_pair0 + hp, 0, 0, 0)),
            pl.BlockSpec((None, None, FOX_PAIR, VT_ROWS, t), lambda b, hp, qt: (b, hp, 0, 0, 0)),
            pl.BlockSpec((t, FOX_HEADS), lambda b, hp, qt: (b, 0)),
            pl.BlockSpec(tril.shape, lambda b, hp, qt: (0, 0)),
        ],
        out_specs=pl.BlockSpec((None, tq, FOX_PAIR * HEAD_DIM), lambda b, hp, qt: (b, qt, hp)),
        out_shape=jax.ShapeDtypeStruct((nb, t, FOX_HEADS * HEAD_DIM), BF16),
        scratch_shapes=[
            pltpu.VMEM((FOX_PAIR, t, LANES), BF16),
            pltpu.VMEM((FOX_PAIR, 1, tq), F32), pltpu.VMEM((FOX_PAIR, VT_ROWS, tq), F32),
        ],
        compiler_params=_cparams(("parallel", "parallel", "arbitrary")),
        name="fox_prompt",
    )(q5, k5, v5, logf, tril)


def _merge_ln_kernel(oa_ref, ob_ref, gm_ref, h_ref, wa_ref, wb_ref, wo_ref, g_ref, b_ref, o_ref, *, alpha):
    pa = _dot(oa_ref[...], wa_ref[...])
    pb = _dot(ob_ref[...], wb_ref[...])
    merged = gm_ref[:, :D_MODEL] * pa + gm_ref[:, D_MODEL:] * pb
    y = alpha * h_ref[...] + _dot(merged.astype(BF16), wo_ref[...])
    mu = jnp.mean(y, axis=-1, keepdims=True)
    d = y - mu
    var = jnp.mean(d * d, axis=-1, keepdims=True)
    o_ref[...] = d * lax.rsqrt(var + LN_EPS) * g_ref[...] + b_ref[...]


def _merge_ln(oa, ob, gm, h, wa, wb, wo, g, b, alpha, tm):
    n, d = h.shape
    row = lambda i: (i, 0)
    fixed = lambda i: (0, 0)
    return pl.pallas_call(
        functools.partial(_merge_ln_kernel, alpha=alpha),
        grid=(n // tm,),
        in_specs=[
            pl.BlockSpec((tm, 512), row), pl.BlockSpec((tm, 512), row),
            pl.BlockSpec((tm, 2 * d), row), pl.BlockSpec((tm, d), row),
            pl.BlockSpec((512, d), fixed), pl.BlockSpec((512, d), fixed), pl.BlockSpec((d, d), fixed),
            pl.BlockSpec((1, d), fixed), pl.BlockSpec((1, d), fixed),
        ],
        out_specs=pl.BlockSpec((tm, d), row),
        out_shape=jax.ShapeDtypeStruct((n, d), F32),
        compiler_params=_cparams(("parallel",)),
        name="merge_ln",
    )(oa, ob, gm, h, wa, wb, wo, g, b)


SAMPLE_ROWS = 32


def _row_slopes(rows, width, head0=0):
    row = lax.broadcasted_iota(jnp.int32, (rows, width), 0)
    head = head0 + (row >> 2)
    slope = lax.bitcast_convert_type((126 - head) << 23, F32)
    return slope, row & 3


def _sample_select_kernel(q_ref, kc_ref, vc_ref, ovl_ref, ocmp_ref, sel_ref, *, q_off, n_cmp, n_slc, t_new):
    n_key = kc_ref.shape[1]
    hrows = SAMPLE_ROWS // NSA_KV_HEADS
    lane = lax.broadcasted_iota(jnp.int32, (hrows, n_key), 1)
    ps, os_ = [], []
    for g in range(NSA_KV_HEADS):
        slope, tok = _row_slopes(hrows, n_key, head0=g * NSA_GROUP)
        dist = (q_off + tok) - (lane * CMP_STRIDE + (CMP_BLOCK - 1))
        valid = (dist >= 0) & (lane < n_cmp)
        s = _dot_nt(q_ref[g * hrows:(g + 1) * hrows], kc_ref[g]) - slope * dist.astype(F32)
        s = jnp.where(valid, s, NEG)
        m = jnp.max(s, axis=1, keepdims=True)
        p = jnp.where(valid, jnp.exp(s - m), 0.0)
        den = jnp.sum(p, axis=1, keepdims=True)
        p = p / jnp.where(den > 0.0, den, 1.0)
        ps.append(p)
        os_.append(_dot(p.astype(BF16), vc_ref[g]))
    ocmp_ref[...] = jnp.concatenate(os_, axis=0)

    p_all = jnp.concatenate(ps, axis=0)
    rr = lax.broadcasted_iota(jnp.int32, (2 * 8, SAMPLE_ROWS), 0)
    cc = lax.broadcasted_iota(jnp.int32, (2 * 8, SAMPLE_ROWS), 1)
    gather = jnp.where(((rr >> 3) == (cc >> 4)) & ((rr & 7) == (cc & 3)), 1.0, 0.0).astype(BF16)
    hi, mid, lo = _split3(p_all)
    imp = _dot(gather, hi) + _dot(gather, mid) + _dot(gather, lo)
    imp_slc = _dot3(imp, ovl_ref[...])
    nb_pad = imp_slc.shape[1]
    blk = lax.broadcasted_iota(jnp.int32, (16, nb_pad), 1)
    blk_f = blk.astype(F32)
    tpos = q_off + (lax.broadcasted_iota(jnp.int32, (16, nb_pad), 0) & 7)
    cur = tpos >> 6
    forced = (blk == 0) | (blk == cur) | (blk == cur - 1)
    score = jnp.where(forced, FORCE, jnp.where(blk * SEL_BLOCK <= tpos, imp_slc, -FORCE))
    score = jnp.where(blk < n_slc, score, -3.0 * FORCE)
    sel = jnp.zeros((16, nb_pad), F32)
    for _ in range(min(N_SEL, n_slc)):
        mx = jnp.max(score, axis=1, keepdims=True)
        first = jnp.min(jnp.where(score == mx, blk_f, float(nb_pad)), axis=1, keepdims=True)
        hit = blk_f == first
        sel = jnp.where(hit, 1.0, sel)
        score = jnp.where(hit, -4.0 * FORCE, score)
    sel_ref[...] = sel


def _sample_select(q32, kcvc, ovl, q_off, n_cmp, n_slc, t_new):
    nb = q32.shape[0]
    n_key = kcvc.shape[2]
    nb_pad = ovl.shape[1]
    return pl.pallas_call(
        functools.partial(_sample_select_kernel, q_off=q_off, n_cmp=n_cmp, n_slc=n_slc, t_new=t_new),
        grid=(nb,),
        in_specs=[
            pl.BlockSpec((None, SAMPLE_ROWS, LANES), lambda b: (b, 0, 0)),
            pl.BlockSpec((None, None, 2, n_key, LANES), lambda b: (b, 0, 0, 0, 0)),
            pl.BlockSpec((None, None, 2, n_key, LANES), lambda b: (b, 1, 0, 0, 0)),
            pl.BlockSpec((n_key, nb_pad), lambda b: (0, 0)),
        ],
        out_specs=[
            pl.BlockSpec((None, SAMPLE_ROWS, LANES), lambda b: (b, 0, 0)),
            pl.BlockSpec((None, 16, nb_pad), lambda b: (b, 0, 0)),
        ],
        out_shape=[
            jax.ShapeDtypeStruct((nb, SAMPLE_ROWS, LANES), F32),
            jax.ShapeDtypeStruct((nb, 16, nb_pad), F32),
        ],
        compiler_params=_cparams(("parallel",)),
        name="sample_select",
    )(q32, kcvc.reshape(nb, 2, 2, n_key, LANES), kcvc.reshape(nb, 2, 2, n_key, LANES), ovl)


TAIL = 128


def _softmax_update_pages(s, vts, m_ref, l_ref, acc_ref):
    m_prev = m_ref[...]
    m_new = jnp.maximum(m_prev, jnp.max(s, axis=1, keepdims=True))
    a = jnp.exp(m_prev - m_new)
    p = jnp.exp(s - m_new)
    l_ref[...] = a * l_ref[...] + jnp.sum(p, axis=1, keepdims=True)
    pb = p.astype(BF16)
    pv = None
    for j, vt in enumerate(vts):
        d = _dot_nt(pb[:, j * PAGE:(j + 1) * PAGE], vt)
        pv = d if pv is None else pv + d
    acc_ref[...] = a * acc_ref[...] + pv
    m_ref[...] = m_new


def _tail_scores(q, knew_t, t_new):
    s = _dot(q, knew_t)
    slope, tok = _row_slopes(SAMPLE_ROWS, TAIL)
    j = lax.broadcasted_iota(jnp.int32, s.shape, 1)
    d = tok - j
    s = s - slope * d.astype(F32)
    return jnp.where((d >= 0) & (j < t_new), s, NEG)


def _sample_nsa_kernel(pt_ref, *refs, n_pages, n_steps, q_off, t_new):
    k_refs = refs[:n_pages]
    v_refs = refs[n_pages:2 * n_pages]
    (q_ref, selx_ref, exp_ref, knew_ref, vnew_ref, wk_ref, wv_ref, wknew_ref, wvnew_ref, ocmp_ref, gate_ref,
     o_ref, m_ref, l_ref, acc_ref) = refs[2 * n_pages:]
    pg = pl.program_id(1)
    n_key = n_pages * PAGE
    q = q_ref[...]
    slope, tok = _row_slopes(SAMPLE_ROWS, n_key)

    @pl.when(pg == 0)
    def _():
        m_ref[...] = jnp.full_like(m_ref, NEG)
        l_ref[...] = jnp.zeros_like(l_ref)
        acc_ref[...] = jnp.zeros_like(acc_ref)

    s = jnp.concatenate([_dot(q, r[...].astype(BF16)) for r in k_refs], axis=1)
    kpos = pg * n_key + lax.broadcasted_iota(jnp.int32, (SAMPLE_ROWS, n_key), 1)
    dist = (q_off + tok) - kpos
    s = s - slope * dist.astype(F32)
    chosen = _dot(selx_ref[...], exp_ref[...]) > 0.5
    s = jnp.where(chosen & (dist >= 0), s, NEG)
    _softmax_update_pages(s, [r[...].astype(BF16) for r in v_refs], m_ref, l_ref, acc_ref)

    @pl.when(pg == n_steps - 1)
    def _():
        lane = lax.broadcasted_iota(jnp.int32, (SAMPLE_ROWS, LANES), 1)
        row = lax.broadcasted_iota(jnp.int32, (SAMPLE_ROWS, LANES), 0)

        def own_group(x):
            return jnp.where(lane < HEAD_DIM, jnp.where(row < 16, x, pltpu.roll(x, HEAD_DIM, axis=1)), 0.0)

        st = _tail_scores(q, knew_ref[...], t_new)
        _softmax_update(st, vnew_ref[...], m_ref, l_ref, acc_ref)
        o_sel = own_group(acc_ref[...] / l_ref[...])

        sw = _dot(q, wk_ref[...].astype(BF16))
        wslope, wtok = _row_slopes(SAMPLE_ROWS, WINDOW)
        wpos = (q_off - WINDOW) + lax.broadcasted_iota(jnp.int32, (SAMPLE_ROWS, WINDOW), 1)
        wd = (q_off + wtok) - wpos
        sw = sw - wslope * wd.astype(F32)
        sw = jnp.where((wd >= 0) & (wd < WINDOW) & (wpos >= 0), sw, NEG)
        swt = _tail_scores(q, wknew_ref[...], t_new)
        mw = jnp.maximum(jnp.max(sw, axis=1, keepdims=True), jnp.max(swt, axis=1, keepdims=True))
        pw = jnp.exp(sw - mw)
        pwt = jnp.exp(swt - mw)
        lw = jnp.sum(pw, axis=1, keepdims=True) + jnp.sum(pwt, axis=1, keepdims=True)
        ow = _dot_nt(pw.astype(BF16), wv_ref[...].astype(BF16)) + _dot(pwt.astype(BF16), wvnew_ref[...])
        o_win = own_group(ow / lw)

        gate = gate_ref[...]
        o_ref[...] = gate[:, 0:1] * ocmp_ref[...] + gate[:, 1:2] * o_sel + gate[:, 2:3] * o_win


def _sample_nsa(pool_t, page_table, q32, selx, expand, knew_t, vnew, wbuf_t, wknew_t, wvnew, ocmp, gate32,
                n_pages, q_off, t_new):
    nb, pages_per_seq = page_table.shape
    n_steps = pages_per_seq // n_pages
    page_specs = [
        pl.BlockSpec((None, None, LANES, PAGE), lambda b, pg, pt, j=j, c=c: (pt[b, pg * n_pages + j], c, 0, 0))
        for c in (2, 3) for j in range(n_pages)
    ]
    per_b = lambda shape: pl.BlockSpec((None,) + shape, lambda b, pg, pt: (b,) + (0,) * len(shape))
    win_spec = lambda c: pl.BlockSpec((None, None, LANES, WINDOW), lambda b, pg, pt: (b, c, 0, 0))
    grid_spec = pltpu.PrefetchScalarGridSpec(
        num_scalar_prefetch=1,
        grid=(nb, n_steps),
        in_specs=page_specs + [
            per_b((SAMPLE_ROWS, LANES)),
            pl.BlockSpec((None, None, SAMPLE_ROWS, LANES), lambda b, pg, pt: (b, pg, 0, 0)),
            pl.BlockSpec((LANES, n_pages * PAGE), lambda b, pg, pt: (0, 0)),
            per_b((LANES, TAIL)), per_b((TAIL, LANES)),
            win_spec(0), win_spec(1),
            per_b((LANES, TAIL)), per_b((TAIL, LANES)),
            per_b((SAMPLE_ROWS, LANES)), per_b((SAMPLE_ROWS, LANES)),
        ],
        out_specs=per_b((SAMPLE_ROWS, LANES)),
        scratch_shapes=[
            pltpu.VMEM((SAMPLE_ROWS, 1), F32), pltpu.VMEM((SAMPLE_ROWS, 1), F32),
            pltpu.VMEM((SAMPLE_ROWS, LANES), F32),
        ],
    )
    return pl.pallas_call(
        functools.partial(_sample_nsa_kernel, n_pages=n_pages, n_steps=n_steps, q_off=q_off, t_new=t_new),
        grid_spec=grid_spec,
        out_shape=jax.ShapeDtypeStruct((nb, SAMPLE_ROWS, LANES), F32),
        compiler_params=_cparams(("parallel", "arbitrary")),
        name="sample_nsa",
    )(page_table, *([pool_t] * (2 * n_pages)), q32, selx, expand, knew_t, vnew, wbuf_t, wbuf_t, wknew_t, wvnew,
      ocmp, gate32)


def _sample_fox_kernel(pt_ref, *refs, n_pages, n_steps, t_new):
    k_refs = refs[:n_pages]
    v_refs = refs[n_pages:2 * n_pages]
    lf_refs = refs[2 * n_pages:3 * n_pages]
    (q_ref, triu_ref, knew_ref, vnew_ref, lfnew_ref,
     o_ref, m_ref, l_ref, acc_ref, carry_ref) = refs[3 * n_pages:]
    pg = pl.program_id(1)
    q = q_ref[...]

    @pl.when(pg == 0)
    def _():
        m_ref[...] = jnp.full_like(m_ref, NEG)
        l_ref[...] = jnp.zeros_like(l_ref)
        acc_ref[...] = jnp.zeros_like(acc_ref)
        carry_ref[...] = jnp.zeros_like(carry_ref)

    triu = triu_ref[...]

    def page_bias(lf, carry):
        c = _dot3(lf, triu) + carry
        total = jnp.broadcast_to(c[:, PAGE - 1:PAGE], c.shape)
        return jnp.concatenate([c] * (SAMPLE_ROWS // FOX_HEADS), axis=0), total

    carry = carry_ref[...]
    scores = []
    for j in range(n_pages):
        bias, carry = page_bias(lf_refs[j][...], carry)
        scores.append(_dot(q, k_refs[j][...].astype(BF16)) - bias)
    carry_ref[...] = carry
    _softmax_update_pages(jnp.concatenate(scores, axis=1), [r[...].astype(BF16) for r in v_refs],
                          m_ref, l_ref, acc_ref)

    @pl.when(pg == n_steps - 1)
    def _():
        bias_n, _ = page_bias(lfnew_ref[...], carry)
        sn = _dot(q, knew_ref[...]) - bias_n
        j = lax.broadcasted_iota(jnp.int32, sn.shape, 1)
        tok = lax.broadcasted_iota(jnp.int32, sn.shape, 0) >> 3
        sn = jnp.where((j <= tok) & (j < t_new), sn, NEG)
        _softmax_update(sn, vnew_ref[...], m_ref, l_ref, acc_ref)
        o_ref[...] = acc_ref[...] / l_ref[...]


def _sample_fox(pool_t, lf_t, page_table, qbd, triu, knew_t, vnew, lfnew_t, n_pages, t_new):
    nb, pages_per_seq = page_table.shape
    n_steps = pages_per_seq // n_pages
    width = FOX_HEADS * HEAD_DIM
    kv_specs = [
        pl.BlockSpec((None, None, width, PAGE), lambda b, pg, pt, j=j, c=c: (pt[b, pg * n_pages + j], c, 0, 0))
        for c in range(2) for j in range(n_pages)
    ]
    lf_specs = [
        pl.BlockSpec((None, FOX_HEADS, PAGE), lambda b, pg, pt, j=j: (pt[b, pg * n_pages + j], 0, 0))
        for j in range(n_pages)
    ]
    per_b = lambda shape: pl.BlockSpec((None,) + shape, lambda b, pg, pt: (b,) + (0,) * len(shape))
    grid_spec = pltpu.PrefetchScalarGridSpec(
        num_scalar_prefetch=1,
        grid=(nb, n_steps),
        in_specs=kv_specs + lf_specs + [
            per_b((SAMPLE_ROWS, width)),
            pl.BlockSpec((PAGE, PAGE), lambda b, pg, pt: (0, 0)),
            per_b((width, TAIL)), per_b((TAIL, width)), per_b((FOX_HEADS, TAIL)),
        ],
        out_specs=per_b((SAMPLE_ROWS, width)),
        scratch_shapes=[
            pltpu.VMEM((SAMPLE_ROWS, 1), F32), pltpu.VMEM((SAMPLE_ROWS, 1), F32),
            pltpu.VMEM((SAMPLE_ROWS, width), F32), pltpu.VMEM((FOX_HEADS, PAGE), F32),
        ],
    )
    return pl.pallas_call(
        functools.partial(_sample_fox_kernel, n_pages=n_pages, n_steps=n_steps, t_new=t_new),
        grid_spec=grid_spec,
        out_shape=jax.ShapeDtypeStruct((nb, SAMPLE_ROWS, width), F32),
        compiler_params=_cparams(("parallel", "arbitrary")),
        name="sample_fox",
    )(page_table, *([pool_t] * (2 * n_pages)), *([lf_t] * n_pages), qbd, triu, knew_t, vnew, lfnew_t)


def _alibi_key_cols(pos):
    tab = np.zeros((len(pos), LANES), np.float32)
    tab[:, ALIBI0] = 1.0
    tab[:, ALIBI0 + 1] = 1.0
    tab[:, ALIBI0 + 2] = (pos // 128) * 128
    tab[:, ALIBI0 + 3] = pos % 128
    return tab


def _prompt_consts(t):
    pos = np.arange(t)
    kaug_win = _alibi_key_cols(pos)
    kaug_sel = kaug_win.copy()
    kaug_sel[pos, SELBIT0 + pos // SEL_BLOCK] = -MASK_BIG
    n_chunk = t // CMP_STRIDE
    kaug_cmp = _alibi_key_cols(np.arange(n_chunk) * CMP_STRIDE + CMP_BLOCK - 1)
    n_cmp = (t - CMP_BLOCK) // CMP_STRIDE + 1
    n_slc = -(-t // SEL_BLOCK)
    ovlt = np.zeros((n_slc, n_chunk), np.float32)
    cs = np.arange(n_cmp)[None, :] * CMP_STRIDE
    ss = np.arange(n_slc)[:, None] * SEL_BLOCK
    ovlt[:, :n_cmp] = (cs < ss + SEL_BLOCK) & (cs + CMP_BLOCK > ss)
    tril = np.tril(np.ones((512, 512), np.float32))
    as_bf = lambda a: jnp.asarray(a, BF16)
    return dict(kaug_sel=as_bf(kaug_sel), kaug_win=as_bf(kaug_win), kaug_cmp=as_bf(kaug_cmp),
                ovlt=as_bf(ovlt), tril=as_bf(tril))


def _sample_ovl(n_key, n_cmp, n_slc, nb_pad):
    ovl = np.zeros((n_key, nb_pad), np.float32)
    cs = np.arange(n_cmp)[:, None] * CMP_STRIDE
    ss = np.arange(n_slc)[None, :] * SEL_BLOCK
    ovl[:n_cmp, :n_slc] = (cs < ss + SEL_BLOCK) & (cs + CMP_BLOCK > ss)
    return jnp.asarray(ovl, BF16)


def _perm_in_proj(w_in, b_in):
    o_qa, o_kva, o_ga, o_fox, o_fb, o_gm = 0, 512, 1280, 1304, 2840, 2848
    d = w_in.shape[0]
    o_fk = o_fox + FOX_HEADS * HEAD_DIM
    o_fv = o_fk + FOX_HEADS * HEAD_DIM
    o_sk = o_kva + 4 * HEAD_DIM
    o_wk = o_kva + 8 * HEAD_DIM

    def cols(x, zeros):
        return jnp.concatenate([x[..., o_kva:o_sk], x[..., o_sk:o_sk + 2 * HEAD_DIM], x[..., o_wk:o_wk + 2 * HEAD_DIM],
                                x[..., o_fk:o_fv], x[..., o_gm:],
                                x[..., o_fb:o_gm], zeros(LANES - FOX_HEADS)], axis=-1)

    def rows(x, zeros):
        return jnp.concatenate([x[..., o_qa:o_ga], x[..., o_fox:o_fb],
                                x[..., o_ga:o_fox], zeros(R_FB - R_GA - 3 * NSA_HEADS),
                                x[..., o_fb:o_gm], zeros(R_TOT - R_FB - FOX_HEADS)], axis=-1)

    w = cols(w_in, lambda k: jnp.zeros((d, k), w_in.dtype)).astype(BF16)
    b = cols(b_in, lambda k: jnp.zeros((k,), b_in.dtype)).reshape(1, C_TOT)
    wt = rows(w_in, lambda k: jnp.zeros((d, k), w_in.dtype)).T.astype(BF16)
    bt = rows(b_in, lambda k: jnp.zeros((k,), b_in.dtype)).reshape(R_TOT, 1)
    return w, b, wt, bt


def _compress_weights(pe_k, w1_k, w2_k, pe_v, w1_v, w2_v):
    n_sub = CMP_BLOCK // CMP_STRIDE

    def per_row(w1):
        w = w1.reshape(n_sub, CMP_STRIDE, HEAD_DIM, CMP_HIDDEN)
        return w.transpose(1, 2, 0, 3).reshape(CMP_STRIDE, HEAD_DIM, n_sub * CMP_HIDDEN)

    blocks = [per_row(w1_k), per_row(w1_k), per_row(w1_v), per_row(w1_v)]
    wr = jnp.zeros((CMP_STRIDE, 4, HEAD_DIM, 4, n_sub * CMP_HIDDEN), F32)
    for s, blk in enumerate(blocks):
        wr = wr.at[:, s, :, s, :].set(blk)
    wr = wr.reshape(CMP_STRIDE, 4 * HEAD_DIM, 4 * n_sub * CMP_HIDDEN).astype(BF16)
    pe8 = jnp.zeros((16, CMP_BLOCK * HEAD_DIM), F32).at[0].set(pe_k.reshape(-1)).at[1].set(pe_v.reshape(-1))
    w1cat = jnp.concatenate([w1_k, w1_v], axis=1).astype(BF16)
    pad = jnp.zeros((CMP_HIDDEN, LANES - HEAD_DIM), F32)
    w2pad = jnp.stack([jnp.concatenate([w2_k, pad], axis=1), jnp.concatenate([w2_v, pad], axis=1)]).astype(BF16)
    return dict(wr=wr, pe8=pe8.astype(BF16), w1cat=w1cat, w2pad=w2pad)


def _prompt_mixer(h1, w, nb, t):
    cmp_rows, logf, kh, gm, nsat, wint, foxt, logft, qat, nvt, fqt, fvt, gat = _in_proj(
        h1, w["w_in"], w["b_in"], w["wt"], w["bt"], nb, t, tm=256)
    consts = _prompt_consts(t)
    kcvc, vct = _compress_prompt(cmp_rows, w["cmp"], nb, t)
    o_a = _nsa_prompt(qat, kh, nvt, kcvc, vct, gat, consts, nb, t, tq=512, tk=512)
    o_b = _fox_prompt(fqt, kh, fvt, logf, consts["tril"], nb, t, tq=1024, tk=512)
    n = nb * t
    state = (nsat.reshape(nb, 4, NSA_KV_HEADS, HEAD_DIM, t).transpose(0, 4, 1, 2, 3),
             wint.reshape(nb, 2, NSA_KV_HEADS, HEAD_DIM, t).transpose(0, 4, 1, 2, 3),
             foxt.reshape(nb, 2, FOX_HEADS, HEAD_DIM, t).transpose(0, 4, 1, 2, 3),
             logft.transpose(0, 2, 1))
    return o_a.reshape(n, -1), o_b.reshape(n, -1), gm, state


def _sample_mixer(h1, w, nb, t_new, nsa_pool, fox_pool, logf_pool, win_buf, page_table):
    n = nb * t_new
    past = page_table.shape[1] * PAGE
    _, _, _, gm, nsat, wint, foxt, logft, qat, _, fqt, _, gat = _in_proj(
        h1, w["w_in"], w["b_in"], w["wt"], w["bt"], 1, n, tm=n)
    nsa, win, fox, logf = nsat[0].T, wint[0].T, foxt[0].T, logft[0].T
    ga = gat[:3 * NSA_HEADS].T
    n_pool = nsa_pool.shape[0]
    nsa_t = nsa_pool.transpose(0, 2, 3, 4, 1).reshape(n_pool, 4, NSA_KV_HEADS * HEAD_DIM, PAGE)
    fox_t = fox_pool.transpose(0, 2, 3, 4, 1).reshape(n_pool, 2, FOX_HEADS * HEAD_DIM, PAGE)
    lf_t = logf_pool.transpose(0, 2, 1)
    wbuf_t = win_buf.transpose(0, 2, 3, 4, 1).reshape(nb, 2, NSA_KV_HEADS * HEAD_DIM, win_buf.shape[1])
    assert win_buf.shape[1] == WINDOW and t_new == 4 and past % SEL_BLOCK == 0

    seq_len = past + t_new
    n_cmp = (seq_len - CMP_BLOCK) // CMP_STRIDE + 1
    n_slc = -(-seq_len // SEL_BLOCK)
    n_chunk = past // CMP_STRIDE
    assert n_cmp <= n_chunk
    kcvc = _compress_sample(nsa_t, page_table, w["cmp"], n_pages=16)

    q = qat[0].reshape(NSA_HEADS, HEAD_DIM, nb, t_new).transpose(2, 0, 3, 1).reshape(nb, SAMPLE_ROWS, HEAD_DIM)
    q = jnp.pad(q, ((0, 0), (0, 0), (0, LANES - HEAD_DIM)))
    nb_pad = -(-n_slc // LANES) * LANES
    ocmp, sel = _sample_select(q, kcvc, _sample_ovl(n_chunk, n_cmp, n_slc, nb_pad), past, n_cmp, n_slc, t_new)

    n_pages = 16
    n_steps = past // (n_pages * PAGE)
    blk_per_step = n_pages * PAGE // SEL_BLOCK
    selg = sel.reshape(nb, 2, 8, nb_pad)[:, :, :t_new, :n_steps * blk_per_step]
    selg = selg.reshape(nb, 2, 1, t_new, n_steps, blk_per_step)
    selx = jnp.broadcast_to(selg, (nb, 2, NSA_GROUP, t_new, n_steps, blk_per_step))
    selx = selx.transpose(0, 4, 1, 2, 3, 5).reshape(nb, n_steps, SAMPLE_ROWS, blk_per_step)
    selx = jnp.pad(selx, ((0, 0), (0, 0), (0, 0), (0, LANES - blk_per_step))).astype(BF16)
    expand = np.zeros((LANES, n_pages * PAGE), np.float32)
    expand[np.arange(n_pages * PAGE) // SEL_BLOCK, np.arange(n_pages * PAGE)] = 1.0
    q_sel = jnp.concatenate([q[:, :16], jnp.roll(q[:, 16:], HEAD_DIM, axis=-1)], axis=1)

    def new_rows(x, c0, width=LANES):
        r = x.reshape(nb, t_new, -1)[:, :, c0:c0 + width]
        return jnp.pad(r, ((0, 0), (0, TAIL - t_new), (0, 0))).astype(BF16)

    def new_cols(x, c0, width=LANES):
        return new_rows(x, c0, width).transpose(0, 2, 1)

    gate32 = ga.reshape(nb, t_new, 3, NSA_HEADS).transpose(0, 3, 1, 2).reshape(nb, SAMPLE_ROWS, 3)
    gate32 = jnp.pad(gate32, ((0, 0), (0, 0), (0, LANES - 3)))
    o_a32 = _sample_nsa(nsa_t, page_table, q_sel, selx, jnp.asarray(expand, BF16),
                        new_cols(nsa, 256), new_rows(nsa, 384), wbuf_t, new_cols(win, 0), new_rows(win, 128),
                        ocmp, gate32, n_pages, past, t_new)
    o_a = o_a32[:, :, :HEAD_DIM].reshape(nb, NSA_HEADS, t_new, HEAD_DIM).transpose(0, 2, 1, 3)
    o_a = o_a.reshape(n, NSA_HEADS * HEAD_DIM).astype(BF16)

    width = FOX_HEADS * HEAD_DIM
    qf = fqt[0].reshape(FOX_HEADS, HEAD_DIM, nb, t_new).transpose(2, 3, 0, 1)
    eye = jnp.eye(FOX_HEADS, dtype=qf.dtype)
    qbd = (qf[:, :, :, None, :] * eye[None, None, :, :, None]).reshape(nb, SAMPLE_ROWS, width)
    triu = jnp.asarray(np.triu(np.ones((PAGE, PAGE), np.float32)), BF16)
    lfnew_t = jnp.pad(logf.reshape(nb, t_new, FOX_HEADS), ((0, 0), (0, TAIL - t_new), (0, 0))).transpose(0, 2, 1)
    o_b32 = _sample_fox(fox_t, lf_t, page_table, qbd, triu, new_cols(fox, 0, width), new_rows(fox, width, width),
                        lfnew_t, 8, t_new)
    o_b = o_b32.reshape(nb, t_new, FOX_HEADS, FOX_HEADS, HEAD_DIM)
    o_b = jnp.einsum("bthgd,hg->bthd", o_b, jnp.eye(FOX_HEADS, dtype=o_b.dtype))
    o_b = o_b.reshape(n, width).astype(BF16)
    state = (nsa.reshape(nb, t_new, 4, NSA_KV_HEADS, HEAD_DIM), win.reshape(nb, t_new, 2, NSA_KV_HEADS, HEAD_DIM),
             fox.reshape(nb, t_new, 2, FOX_HEADS, HEAD_DIM), logf.reshape(nb, t_new, FOX_HEADS))
    return o_a, o_b, gm, state


def _layer(x, w, alpha, mixer, tm):
    h1 = _ffn_ln(x, w["ffn1_wg"], w["ffn1_wu"], w["ffn1_wd"], w["ln1_g"], w["ln1_b"], alpha, tm)
    o_a, o_b, gm, state = mixer(h1, w)
    h2 = _merge_ln(o_a, o_b, gm, h1, w["w_proj_a"], w["w_proj_b"], w["w_out"], w["ln2_g"], w["ln2_b"], alpha, tm)
    y = _ffn_ln(h2, w["ffn2_wg"], w["ffn2_wu"], w["ffn2_wd"], w["ln3_g"], w["ln3_b"], alpha, tm)
    return y, state


def kernel(x_prompt, x_sample, cache_nsa_kv, cache_fox_kv, cache_fox_logf, state_win_kv, page_table,
           ln1_g, ln1_b, ffn1_w_gate, ffn1_w_up, ffn1_w_down, w_in, b_in,
           cmp_pe_k, cmp_w1_k, cmp_w2_k, cmp_pe_v, cmp_w1_v, cmp_w2_v,
           w_proj_a, w_proj_b, w_out, ln2_g, ln2_b,
           ffn2_w_gate, ffn2_w_up, ffn2_w_down, ln3_g, ln3_b):
    depth = w_in.shape[0]
    nb_p, t_p, d = x_prompt.shape
    nb_s, t_s, _ = x_sample.shape
    alpha = float((2.0 * depth) ** 0.25)
    h_p = x_prompt.reshape(nb_p * t_p, d)
    h_s = x_sample.reshape(nb_s * t_s, d)
    st_p, st_s = [], []
    for l in range(depth):
        wi, bi, wt, bt = _perm_in_proj(w_in[l], b_in[l])
        vec = lambda a: a[l].reshape(1, d)
        w = dict(
            ln1_g=vec(ln1_g), ln1_b=vec(ln1_b), ln2_g=vec(ln2_g), ln2_b=vec(ln2_b), ln3_g=vec(ln3_g), ln3_b=vec(ln3_b),
            ffn1_wg=ffn1_w_gate[l].astype(BF16), ffn1_wu=ffn1_w_up[l].astype(BF16), ffn1_wd=ffn1_w_down[l].astype(BF16),
            ffn2_wg=ffn2_w_gate[l].astype(BF16), ffn2_wu=ffn2_w_up[l].astype(BF16), ffn2_wd=ffn2_w_down[l].astype(BF16),
            w_in=wi, b_in=bi, wt=wt, bt=bt,
            cmp=_compress_weights(cmp_pe_k[l], cmp_w1_k[l], cmp_w2_k[l], cmp_pe_v[l], cmp_w1_v[l], cmp_w2_v[l]),
            w_proj_a=w_proj_a[l].astype(BF16), w_proj_b=w_proj_b[l].astype(BF16), w_out=w_out[l].astype(BF16),
        )
        h_p, s_p = _layer(h_p, w, alpha, functools.partial(_prompt_mixer, nb=nb_p, t=t_p), tm=512)
        h_s, s_s = _layer(h_s, w, alpha, functools.partial(
            _sample_mixer, nb=nb_s, t_new=t_s, nsa_pool=cache_nsa_kv[l], fox_pool=cache_fox_kv[l],
            logf_pool=cache_fox_logf[l], win_buf=state_win_kv[l], page_table=page_table), tm=nb_s * t_s)
        st_p.append(s_p)
        st_s.append(s_s)

    def states(st, nb, t, win_prev):
        nsa = jnp.stack([s[0] for s in st])
        fox = jnp.stack([s[2] for s in st])
        logf = jnp.stack([s[3] for s in st])
        wins = []
        for l, s in enumerate(st):
            wr = s[1]
            if win_prev is None:
                wins.append(wr[:, t - min(WINDOW, t):])
            else:
                wins.append(jnp.concatenate([win_prev[l], wr], axis=1)[:, t:])
        return nsa, fox, logf, jnp.stack(wins)

    nsa_p, fox_p, logf_p, win_p = states(st_p, nb_p, t_p, None)
    nsa_s, fox_s, logf_s, win_s = states(st_s, nb_s, t_s, state_win_kv)
    return (h_p.reshape(nb_p, t_p, d), h_s.reshape(nb_s, t_s, d), nsa_p, fox_p, logf_p, win_p,
            nsa_s, fox_s, logf_s, win_s)
```

```python
import functools

import numpy as np
import jax
import jax.numpy as jnp
from jax import lax
from jax.experimental import pallas as pl
from jax.experimental.pallas import tpu as pltpu

F32 = jnp.float32
BF16 = jnp.bfloat16

LANES = 128
VMEM_LIMIT = 56 * 1024 * 1024

TM_FFN = 512
TM_PROJ = 256
NSA_TQ, NSA_TK = 512, 512
FOX_TQ, FOX_TK = 1024, 512
CMP_PAGES = 32
NSA_PAGES = 32
FOX_PAGES = 16

D_MODEL = 1024
HEAD_DIM = 64
NSA_HEADS = 8
NSA_KV_HEADS = 2
NSA_GROUP = NSA_HEADS // NSA_KV_HEADS
FOX_HEADS = 8
CMP_BLOCK = 32
CMP_STRIDE = 16
CMP_HIDDEN = 2 * HEAD_DIM
SEL_BLOCK = 64
N_SEL = 16
WINDOW = 512
PAGE = 128
LN_EPS = 1e-5
NEG = -1e30
FORCE = 1e9
SCALE = HEAD_DIM ** -0.5
MASK_BIG = 30000.0

SELBIT0 = 64
ALIBI0 = 96


def _dot(a, b):
    return jnp.dot(a, b, preferred_element_type=F32)


def _dot_nt(a, b):
    return lax.dot_general(a, b, (((1,), (1,)), ((), ())), preferred_element_type=F32)


def _split3(x):
    hi = x.astype(BF16)
    r1 = x - hi.astype(F32)
    mid = r1.astype(BF16)
    lo = (r1 - mid.astype(F32)).astype(BF16)
    return hi, mid, lo


def _dot3(x, w):
    hi, mid, lo = _split3(x)
    return _dot(hi, w) + _dot(mid, w) + _dot(lo, w)


def _dot3_nt(a, x):
    hi, mid, lo = _split3(x)
    return _dot_nt(a, hi) + _dot_nt(a, mid) + _dot_nt(a, lo)


def _sigmoid(x):
    return 1.0 / (1.0 + jnp.exp(-x))


def _log_sigmoid(x):
    return -(jnp.maximum(-x, 0.0) + jnp.log(1.0 + jnp.exp(-jnp.abs(x))))


def _gelu_tanh(x):
    c = np.float32(np.sqrt(2.0 / np.pi))
    return 0.5 * x * (1.0 + jnp.tanh(c * (x + np.float32(0.044715) * (x * x * x))))


def _cparams(sem):
    return pltpu.CompilerParams(dimension_semantics=sem, vmem_limit_bytes=VMEM_LIMIT)


def _ffn_ln_kernel(x_ref, wg_ref, wu_ref, wd_ref, g_ref, b_ref, o_ref, acc_ref, *, alpha, n_ff):
    j = pl.program_id(1)

    @pl.when(j == 0)
    def _():
        acc_ref[...] = jnp.zeros_like(acc_ref)

    xb = x_ref[...].astype(BF16)
    gate = _dot(xb, wg_ref[...])
    up = _dot(xb, wu_ref[...])
    mid = (gate * _sigmoid(gate) * up).astype(BF16)
    acc_ref[...] += _dot(mid, wd_ref[...])

    @pl.when(j == n_ff - 1)
    def _():
        y = alpha * x_ref[...] + 0.5 * acc_ref[...]
        mu = jnp.mean(y, axis=-1, keepdims=True)
        d = y - mu
        var = jnp.mean(d * d, axis=-1, keepdims=True)
        o_ref[...] = d * lax.rsqrt(var + LN_EPS) * g_ref[...] + b_ref[...]


def _ffn_ln(x, wg, wu, wd, g, b, alpha, tm):
    n, d = x.shape
    dff = wg.shape[1]
    tf = dff // 2
    n_ff = dff // tf
    return pl.pallas_call(
        functools.partial(_ffn_ln_kernel, alpha=alpha, n_ff=n_ff),
        grid=(n // tm, n_ff),
        in_specs=[
            pl.BlockSpec((tm, d), lambda i, j: (i, 0)),
            pl.BlockSpec((d, tf), lambda i, j: (0, j)),
            pl.BlockSpec((d, tf), lambda i, j: (0, j)),
            pl.BlockSpec((tf, d), lambda i, j: (j, 0)),
            pl.BlockSpec((1, d), lambda i, j: (0, 0)),
            pl.BlockSpec((1, d), lambda i, j: (0, 0)),
        ],
        out_specs=pl.BlockSpec((tm, d), lambda i, j: (i, 0)),
        out_shape=jax.ShapeDtypeStruct((n, d), F32),
        scratch_shapes=[pltpu.VMEM((tm, d), F32)],
        compiler_params=_cparams(("parallel", "arbitrary")),
        name="ffn_ln",
    )(x, wg, wu, wd, g, b)


C_CMP = 0
C_KH = 256
C_GM = 1024
C_FB = 3072
C_TOT = 3200
R_QA = 0
R_KVA = 512
R_FQ = 1280
R_FKV = 1792
R_GA = 2816
R_FB = 2848
R_TOT = 2864
N_KH = 12
VT_ROWS = 80


def _head_pad(z, h):
    col = z[:, LANES * (h // 2):LANES * (h // 2 + 1)]
    if h % 2:
        col = pltpu.roll(col, HEAD_DIM, axis=1)
    lane = lax.broadcasted_iota(jnp.int32, col.shape, 1)
    return jnp.where(lane < HEAD_DIM, col, 0.0)


def _ones_row_block(width):
    sub = lax.broadcasted_iota(jnp.int32, (VT_ROWS - HEAD_DIM, width), 0)
    return jnp.where(sub == 0, 1.0, 0.0)


def _in_proj_kernel(h_ref, w_ref, b_ref, wt_ref, bt_ref,
                    cmp_ref, logf_ref, kh_ref, gm_ref,
                    nsat_ref, wint_ref, foxt_ref, logft_ref,
                    qat_ref, nvt_ref, fqt_ref, fvt_ref, gat_ref):
    hb = h_ref[...].astype(BF16)
    tm = hb.shape[0]

    def proj(c0, c1):
        return _dot(hb, w_ref[:, c0:c1]) + b_ref[:, c0:c1]

    def proj_t(r0, r1):
        return _dot_nt(wt_ref[r0:r1, :], hb) + bt_ref[r0:r1, :]

    cmp_ref[...] = proj(C_CMP, C_KH)
    zk = proj(C_KH, C_GM)
    for h in range(N_KH):
        kh_ref[h] = _head_pad(zk, h).astype(BF16)
    gm_ref[...] = _sigmoid(proj(C_GM, C_FB))
    logf_ref[...] = _log_sigmoid(proj(C_FB, C_TOT))[:, :FOX_HEADS]

    ones_blk = _ones_row_block(tm)

    def value_head(zt, r0):
        return jnp.concatenate([zt[r0:r0 + HEAD_DIM], ones_blk], axis=0).astype(BF16)

    qat_ref[...] = (proj_t(R_QA, R_KVA) * SCALE).astype(BF16).reshape(qat_ref.shape)

    zkv = proj_t(R_KVA, R_FQ)
    nsat_ref[...] = zkv[:512]
    wint_ref[...] = zkv[512:]
    for j, head in enumerate((6, 7, 10, 11)):
        nvt_ref[j] = value_head(zkv, head * HEAD_DIM)

    fqt_ref[...] = (proj_t(R_FQ, R_FKV) * SCALE).astype(BF16).reshape(fqt_ref.shape)
    zf = proj_t(R_FKV, R_GA)
    foxt_ref[...] = zf
    for h in range(FOX_HEADS):
        fvt_ref[h] = value_head(zf, (FOX_HEADS + h) * HEAD_DIM)

    gat_ref[...] = _sigmoid(proj_t(R_GA, R_FB))
    logft_ref[...] = _log_sigmoid(proj_t(R_FB, R_TOT))[:FOX_HEADS]


def _in_proj(h, w, b, wt, bt, nseq, tseq, tm):
    n, d = h.shape
    tps = tseq // tm
    row = lambda i: (i, 0)
    headmaj = lambda i: (i // tps, 0, i % tps, 0)
    featmaj = lambda i: (i // tps, 0, 0, i % tps)
    feat3 = lambda i: (i // tps, 0, i % tps)
    n_ga = R_FB - R_GA
    out_shape = [
        jax.ShapeDtypeStruct((n, 256), F32),
        jax.ShapeDtypeStruct((n, FOX_HEADS), F32),
        jax.ShapeDtypeStruct((nseq, N_KH, tseq, LANES), BF16),
        jax.ShapeDtypeStruct((n, 2 * D_MODEL), F32),
        jax.ShapeDtypeStruct((nseq, 512, tseq), F32),
        jax.ShapeDtypeStruct((nseq, 256, tseq), F32),
        jax.ShapeDtypeStruct((nseq, 1024, tseq), F32),
        jax.ShapeDtypeStruct((nseq, FOX_HEADS, tseq), F32),
        jax.ShapeDtypeStruct((nseq, NSA_HEADS, HEAD_DIM, tseq), BF16),
        jax.ShapeDtypeStruct((nseq, 4, VT_ROWS, tseq), BF16),
        jax.ShapeDtypeStruct((nseq, FOX_HEADS, HEAD_DIM, tseq), BF16),
        jax.ShapeDtypeStruct((nseq, FOX_HEADS, VT_ROWS, tseq), BF16),
        jax.ShapeDtypeStruct((n_ga, n), F32),
    ]
    out_specs = [
        pl.BlockSpec((tm, 256), row),
        pl.BlockSpec((tm, FOX_HEADS), row),
        pl.BlockSpec((None, N_KH, tm, LANES), headmaj),
        pl.BlockSpec((tm, 2 * D_MODEL), row),
        pl.BlockSpec((None, 512, tm), feat3),
        pl.BlockSpec((None, 256, tm), feat3),
        pl.BlockSpec((None, 1024, tm), feat3),
        pl.BlockSpec((None, FOX_HEADS, tm), feat3),
        pl.BlockSpec((None, NSA_HEADS, HEAD_DIM, tm), featmaj),
        pl.BlockSpec((None, 4, VT_ROWS, tm), featmaj),
        pl.BlockSpec((None, FOX_HEADS, HEAD_DIM, tm), featmaj),
        pl.BlockSpec((None, FOX_HEADS, VT_ROWS, tm), featmaj),
        pl.BlockSpec((n_ga, tm), lambda i: (0, i)),
    ]
    return pl.pallas_call(
        _in_proj_kernel,
        grid=(n // tm,),
        in_specs=[
            pl.BlockSpec((tm, d), row),
            pl.BlockSpec((d, C_TOT), lambda i: (0, 0)),
            pl.BlockSpec((1, C_TOT), lambda i: (0, 0)),
            pl.BlockSpec((R_TOT, d), lambda i: (0, 0)),
            pl.BlockSpec((R_TOT, 1), lambda i: (0, 0)),
        ],
        out_specs=out_specs,
        out_shape=out_shape,
        compiler_params=_cparams(("parallel",)),
        name="in_proj",
    )(h, w, b, wt, bt)


def _cmp_parts(k_ref, v_ref, wr_ref, n_rows):
    n_chunk = n_rows // CMP_STRIDE

    def kind(ref, w):
        x = jnp.concatenate([ref[pl.ds(r, n_chunk, stride=CMP_STRIDE), :] for r in range(CMP_STRIDE)], axis=1)
        return _dot(x.astype(BF16), w)

    return jnp.concatenate([kind(k_ref, wr_ref[0]), kind(v_ref, wr_ref[1])], axis=1)


def _cmp_finish(parts, bias8, w2_ref, out_ref, vt_ref=None):
    n = parts.shape[0]
    for s in range(4):
        kind = s // 2
        pa = parts[:, 256 * s:256 * s + CMP_HIDDEN]
        pb = parts[:, 256 * s + CMP_HIDDEN:256 * (s + 1)]
        pb = pltpu.roll(pb, n - 1, axis=0)
        bias = bias8[kind:kind + 1, CMP_HIDDEN * kind:CMP_HIDDEN * (kind + 1)]
        hid = bias + pa + pb
        c = _dot(_gelu_tanh(hid).astype(BF16), w2_ref[kind])
        out_ref[s] = c.astype(BF16)
        if vt_ref is not None and kind == 1:
            vt = jnp.concatenate([c.T[:HEAD_DIM], _ones_row_block(n)], axis=0)
            vt_ref[s - 2] = vt.astype(BF16)


def _cmp_bias(pe_ref, w1_ref):
    return _dot(pe_ref[...], w1_ref[...])


def _compress_prompt_kernel(k_ref, v_ref, wr_ref, pe_ref, w1_ref, w2_ref, out_ref, vt_ref, *, t):
    parts = _cmp_parts(k_ref, v_ref, wr_ref, t)
    _cmp_finish(parts, _cmp_bias(pe_ref, w1_ref), w2_ref, out_ref, vt_ref)


def _compress_prompt(nsa_rows, cw, nb, t):
    n_chunk = t // CMP_STRIDE
    return pl.pallas_call(
        functools.partial(_compress_prompt_kernel, t=t),
        grid=(nb,),
        in_specs=[
            pl.BlockSpec((t, LANES), lambda b: (b, 0)),
            pl.BlockSpec((t, LANES), lambda b: (b, 1)),
            pl.BlockSpec((2, CMP_STRIDE * LANES, 512), lambda b: (0, 0, 0)),
            pl.BlockSpec((16, 2048), lambda b: (0, 0)),
            pl.BlockSpec((2048, 256), lambda b: (0, 0)),
            pl.BlockSpec((2, CMP_HIDDEN, LANES), lambda b: (0, 0, 0)),
        ],
        out_specs=[
            pl.BlockSpec((None, 4, n_chunk, LANES), lambda b: (b, 0, 0, 0)),
            pl.BlockSpec((None, 2, VT_ROWS, n_chunk), lambda b: (b, 0, 0, 0)),
        ],
        out_shape=[
            jax.ShapeDtypeStruct((nb, 4, n_chunk, LANES), BF16),
            jax.ShapeDtypeStruct((nb, 2, VT_ROWS, n_chunk), BF16),
        ],
        compiler_params=_cparams(("parallel",)),
        name="compress_prompt",
    )(nsa_rows, nsa_rows, cw["wr"], cw["pe8"], cw["w1cat"], cw["w2pad"])


def _compress_sample_kernel(pt_ref, *refs, n_pages, n_steps):
    k_refs = refs[:n_pages]
    v_refs = refs[n_pages:2 * n_pages]
    wr_ref, pe_ref, w1_ref, w2_ref, out_ref, parts_ref, krows_ref, vrows_ref = refs[2 * n_pages:]
    pg = pl.program_id(1)
    rows = n_pages * (PAGE // CMP_STRIDE)
    for j in range(n_pages):
        krows_ref[j * PAGE:(j + 1) * PAGE, :] = k_refs[j][...].T
        vrows_ref[j * PAGE:(j + 1) * PAGE, :] = v_refs[j][...].T
    parts = _cmp_parts(krows_ref, vrows_ref, wr_ref, n_pages * PAGE)
    parts_ref[pl.ds(pl.multiple_of(pg * rows, rows), rows), :] = parts

    @pl.when(pg == n_steps - 1)
    def _():
        _cmp_finish(parts_ref[...], _cmp_bias(pe_ref, w1_ref), w2_ref, out_ref)


def _compress_sample(pool_t, page_table, cw, n_pages):
    nb, pages_per_seq = page_table.shape
    n_steps = pages_per_seq // n_pages
    n_chunk = pages_per_seq * (PAGE // CMP_STRIDE)
    page_specs = [
        pl.BlockSpec((None, None, LANES, PAGE), lambda b, pg, pt, j=j, c=c: (pt[b, pg * n_pages + j], c, 0, 0))
        for c in range(2) for j in range(n_pages)
    ]
    grid_spec = pltpu.PrefetchScalarGridSpec(
        num_scalar_prefetch=1,
        grid=(nb, n_steps),
        in_specs=page_specs + [
            pl.BlockSpec((2, CMP_STRIDE * LANES, 512), lambda b, pg, pt: (0, 0, 0)),
            pl.BlockSpec((16, 2048), lambda b, pg, pt: (0, 0)),
            pl.BlockSpec((2048, 256), lambda b, pg, pt: (0, 0)),
            pl.BlockSpec((2, CMP_HIDDEN, LANES), lambda b, pg, pt: (0, 0, 0)),
        ],
        out_specs=pl.BlockSpec((None, 4, n_chunk, LANES), lambda b, pg, pt: (b, 0, 0, 0)),
        scratch_shapes=[pltpu.VMEM((n_chunk, 1024), F32),
                        pltpu.VMEM((n_pages * PAGE, LANES), F32), pltpu.VMEM((n_pages * PAGE, LANES), F32)],
    )
    return pl.pallas_call(
        functools.partial(_compress_sample_kernel, n_pages=n_pages, n_steps=n_steps),
        grid_spec=grid_spec,
        out_shape=jax.ShapeDtypeStruct((nb, 4, n_chunk, LANES), BF16),
        compiler_params=_cparams(("parallel", "arbitrary")),
        name="compress_sample",
    )(page_table, *([pool_t] * (2 * n_pages)), cw["wr"], cw["pe8"], cw["w1cat"], cw["w2pad"])


def _softmax_update(s, v, m_ref, l_ref, acc_ref):
    m_prev = m_ref[...]
    m_new = jnp.maximum(m_prev, jnp.max(s, axis=1, keepdims=True))
    a = jnp.exp(m_prev - m_new)
    p = jnp.exp(s - m_new)
    l_ref[...] = a * l_ref[...] + jnp.sum(p, axis=1, keepdims=True)
    acc_ref[...] = a * acc_ref[...] + _dot(p.astype(BF16), v)
    m_ref[...] = m_new


def _softmax_update_t(s, vt, m_ref, acc_ref):
    m_prev = m_ref[...]
    m_new = jnp.maximum(m_prev, jnp.max(s, axis=0, keepdims=True))
    a = jnp.exp(m_prev - m_new)
    p = jnp.exp(s - m_new).astype(BF16)
    acc_ref[...] = a * acc_ref[...] + _dot(vt, p)
    m_ref[...] = m_new


def _softmax_reset(m_ref, acc_ref):
    m_ref[...] = jnp.full_like(m_ref, NEG)
    acc_ref[...] = jnp.zeros_like(acc_ref)


def _softmax_result(acc_ref):
    acc = acc_ref[...]
    return acc[:HEAD_DIM] / acc[HEAD_DIM:HEAD_DIM + 1]


def _nsa_prompt_kernel(q_ref, selk_ref, selvt_ref, wink_ref, winvt_ref, kc_ref, vct_ref, gat_ref,
                       kaug_sel_ref, kaug_win_ref, kaug_cmp_ref, ovlt_ref,
                       o_ref, ksel_s, kwin_s, m_ref, acc_ref, *, tq, tk, n_cmp, n_slc):
    g = pl.program_id(1)
    qt = pl.program_id(2)
    cols = NSA_GROUP * tq
    q0 = qt * tq

    @pl.when(qt == 0)
    def _():
        ksel_s[...] = selk_ref[...] + kaug_sel_ref[...]
        kwin_s[...] = wink_ref[...] + kaug_win_ref[...]

    qt4 = jnp.concatenate([q_ref[r] for r in range(NSA_GROUP)], axis=1)

    def head_and_pos(shape):
        lane = lax.broadcasted_iota(jnp.int32, shape, 1)
        return lane >> (tq.bit_length() - 1), q0 + (lane & (tq - 1))

    sub = lax.broadcasted_iota(jnp.int32, (16, cols), 0)
    r, t = head_and_pos((16, cols))
    sl0 = jnp.where(r == 0, 0.5, jnp.where(r == 1, 0.25, jnp.where(r == 2, 0.125, 0.0625))).astype(F32)
    slope = jnp.where(g == 0, sl0, sl0 * 0.0625)
    t_hi = ((t >> 7) << 7).astype(F32)
    t_lo = (t & 127).astype(F32)
    al = jnp.where(sub == 0, -slope * t_hi,
                   jnp.where(sub == 1, -slope * t_lo,
                             jnp.where((sub == 2) | (sub == 3), slope, 0.0))).astype(BF16)
    zeros16 = jnp.zeros((16, cols), BF16)

    def query_cols(selbits):
        return jnp.concatenate([qt4, selbits, al, zeros16], axis=0)

    qa = query_cols(jnp.zeros((n_slc, cols), BF16))

    n_chunk = kc_ref.shape[0]
    kc = kc_ref[...] + kaug_cmp_ref[...]
    s = _dot(kc, qa)
    nidx = lax.broadcasted_iota(jnp.int32, (n_chunk, cols), 0)
    _, t_c = head_and_pos((n_chunk, cols))
    valid = (t_c >= nidx * CMP_STRIDE + (CMP_BLOCK - 1)) & (nidx < n_cmp)
    s = jnp.where(valid, s, NEG)
    m = jnp.max(s, axis=0, keepdims=True)
    p = jnp.where(valid, jnp.exp(s - m), 0.0)
    den = jnp.sum(p, axis=0, keepdims=True)
    p = p / jnp.where(den > 0.0, den, 1.0)
    o_cmp = _dot(vct_ref[...], p.astype(BF16))[:HEAD_DIM]

    imp = p[:, 0:tq] + p[:, tq:2 * tq] + p[:, 2 * tq:3 * tq] + p[:, 3 * tq:4 * tq]
    hi, mid, lo = _split3(imp)
    ovlt = ovlt_ref[...]
    imp_slc = _dot(ovlt, hi) + _dot(ovlt, mid) + _dot(ovlt, lo)
    blk = lax.broadcasted_iota(jnp.int32, (n_slc, tq), 0)
    t1 = q0 + lax.broadcasted_iota(jnp.int32, (n_slc, tq), 1)
    cur = t1 >> 6
    forced = (blk == 0) | (blk == cur) | (blk == cur - 1)
    score = jnp.where(forced, FORCE, jnp.where(blk * SEL_BLOCK <= t1, imp_slc, -FORCE))
    rank = jnp.zeros((n_slc, tq), F32)
    for i in range(n_slc):
        row = score[i:i + 1, :]
        beats = (row > score) | ((row == score) & (blk > i))
        rank = rank + jnp.where(beats, 1.0, 0.0)
    notsel = jnp.where(rank >= float(N_SEL), 1.0, 0.0).astype(BF16)
    qs = query_cols(jnp.concatenate([notsel] * NSA_GROUP, axis=1))

    kpos_iota = lax.broadcasted_iota(jnp.int32, (tk, cols), 0)
    _, t_row = head_and_pos((1, cols))
    n_full = q0 // tk

    def branch(qx, k_s, vt_ref, kt_lo, window):
        _softmax_reset(m_ref, acc_ref)

        def tile(k0, causal):
            sc = _dot(k_s[pl.ds(k0, tk), :], qx)
            rel = t_row - k0
            if causal:
                sc = jnp.where(kpos_iota <= rel, sc, NEG)
            elif window:
                sc = jnp.where(kpos_iota > rel - WINDOW, sc, NEG)
            _softmax_update_t(sc, vt_ref[:, pl.ds(k0, tk)], m_ref, acc_ref)

        def body(kt, carry):
            tile(pl.multiple_of(kt * tk, tk), False)
            return carry

        lax.fori_loop(kt_lo, n_full, body, 0)
        for j in range(tq // tk):
            tile(pl.multiple_of(q0 + j * tk, tk), True)
        return _softmax_result(acc_ref)

    o_sel = branch(qs, ksel_s, selvt_ref, 0, False)
    o_win = branch(qa, kwin_s, winvt_ref, jnp.maximum(q0 - WINDOW, 0) // tk, True)

    outs = []
    for rr in range(NSA_GROUP):
        def gate(br):
            return gat_ref[pl.ds(br * NSA_HEADS + g * NSA_GROUP + rr, 1), :]
        sl = slice(rr * tq, (rr + 1) * tq)
        outs.append(gate(0) * o_cmp[:, sl] + gate(1) * o_sel[:, sl] + gate(2) * o_win[:, sl])
    o_ref[...] = jnp.concatenate(outs, axis=0).T.astype(BF16)


def _nsa_prompt(qat, kh, nvt, kcvc, vct, gat, consts, nb, t, tq, tk):
    n_cmp = (t - CMP_BLOCK) // CMP_STRIDE + 1
    n_slc = -(-t // SEL_BLOCK)
    n_chunk = t // CMP_STRIDE
    assert n_slc == 32 and tq % tk == 0 and tq <= WINDOW
    q5 = qat.reshape(nb, NSA_KV_HEADS, NSA_GROUP, HEAD_DIM, t)
    k_spec = lambda off: pl.BlockSpec((None, None, t, LANES), lambda b, g, qt: (b, off + g, 0, 0))
    vt_spec = lambda off: pl.BlockSpec((None, None, VT_ROWS, t), lambda b, g, qt: (b, off + g, 0, 0))
    const2 = lambda shape: pl.BlockSpec(shape, lambda b, g, qt: (0, 0))
    cols = NSA_GROUP * tq
    return pl.pallas_call(
        functools.partial(_nsa_prompt_kernel, tq=tq, tk=tk, n_cmp=n_cmp, n_slc=n_slc),
        grid=(nb, NSA_KV_HEADS, t // tq),
        in_specs=[
            pl.BlockSpec((None, None, NSA_GROUP, HEAD_DIM, tq), lambda b, g, qt: (b, g, 0, 0, qt)),
            k_spec(0), vt_spec(0), k_spec(2), vt_spec(2),
            pl.BlockSpec((None, None, n_chunk, LANES), lambda b, g, qt: (b, g, 0, 0)),
            pl.BlockSpec((None, None, VT_ROWS, n_chunk), lambda b, g, qt: (b, g, 0, 0)),
            pl.BlockSpec((R_FB - R_GA, tq), lambda b, g, qt: (0, b * (t // tq) + qt)),
            const2((t, LANES)), const2((t, LANES)), const2((n_chunk, LANES)), const2((n_slc, n_chunk)),
        ],
        out_specs=pl.BlockSpec((None, tq, NSA_GROUP * HEAD_DIM), lambda b, g, qt: (b, qt, g)),
        out_shape=jax.ShapeDtypeStruct((nb, t, NSA_HEADS * HEAD_DIM), BF16),
        scratch_shapes=[
            pltpu.VMEM((t, LANES), BF16), pltpu.VMEM((t, LANES), BF16),
            pltpu.VMEM((1, cols), F32), pltpu.VMEM((VT_ROWS, cols), F32),
        ],
        compiler_params=_cparams(("parallel", "parallel", "arbitrary")),
        name="nsa_prompt",
    )(q5, kh, nvt, kh, nvt, kcvc, vct, gat,
      consts["kaug_sel"], consts["kaug_win"], consts["kaug_cmp"], consts["ovlt"])


FOX_BIAS0 = 64
FOX_PAIR = 2


def _fox_prompt_kernel(q_ref, k_ref, vt_ref, lf_ref, tril_ref, o_ref, kaug_s, m_ref, acc_ref, *, tq, tk):
    hp = pl.program_id(1)
    qt = pl.program_id(2)
    q0 = qt * tq
    t = k_ref.shape[1]

    @pl.when(qt == 0)
    def _():
        tb = tril_ref.shape[0]
        tril = tril_ref[...]
        carry = jnp.zeros((1, FOX_HEADS), F32)
        cs = []
        for i in range(t // tb):
            hi, mid, lo = _split3(lf_ref[i * tb:(i + 1) * tb, :])
            c = _dot(tril, hi) + _dot(tril, mid) + _dot(tril, lo) + carry
            cs.append(c)
            carry = c[tb - 1:tb, :]
        hi, mid, lo = _split3(-jnp.concatenate(cs, axis=0))
        hrow = lax.broadcasted_iota(jnp.int32, (FOX_HEADS, LANES), 0)
        lane = lax.broadcasted_iota(jnp.int32, (FOX_HEADS, LANES), 1)
        for i in range(FOX_PAIR):
            head = FOX_PAIR * hp + i

            def place(j):
                return jnp.where((hrow == head) & (lane == FOX_BIAS0 + j), 1.0, 0.0).astype(BF16)

            aug = _dot(hi, place(0)) + _dot(mid, place(1)) + _dot(lo, place(2))
            kaug_s[i] = k_ref[i] + aug.astype(BF16)

    sub = lax.broadcasted_iota(jnp.int32, (16, tq), 0)
    ones3 = jnp.where(sub < 3, 1.0, 0.0).astype(BF16)
    zeros = jnp.zeros((LANES - HEAD_DIM - 16, tq), BF16)
    qx = [jnp.concatenate([q_ref[i], ones3, zeros], axis=0) for i in range(FOX_PAIR)]

    kpos_iota = lax.broadcasted_iota(jnp.int32, (tk, tq), 0)
    t_row = q0 + lax.broadcasted_iota(jnp.int32, (1, tq), 1)
    for i in range(FOX_PAIR):
        _softmax_reset(m_ref.at[i], acc_ref.at[i])

    def tile(k0, masked):
        for i in range(FOX_PAIR):
            sc = _dot(kaug_s[i, pl.ds(k0, tk), :], qx[i])
            if masked:
                sc = jnp.where(kpos_iota <= t_row - k0, sc, NEG)
            _softmax_update_t(sc, vt_ref[i, :, pl.ds(k0, tk)], m_ref.at[i], acc_ref.at[i])

    def body(kt, carry):
        tile(pl.multiple_of(kt * tk, tk), False)
        return carry

    lax.fori_loop(0, q0 // tk, body, 0)
    for j in range(tq // tk):
        tile(pl.multiple_of(q0 + j * tk, tk), True)
    o_t = jnp.concatenate([_softmax_result(acc_ref.at[i]) for i in range(FOX_PAIR)], axis=0)
    o_ref[...] = o_t.T.astype(BF16)


def _fox_prompt(fqt, kh, fvt, logf, tril, nb, t, tq, tk):
    n_pair = FOX_HEADS // FOX_PAIR
    q5 = fqt.reshape(nb, n_pair, FOX_PAIR, HEAD_DIM, t)
    k5 = kh.reshape(nb, N_KH // FOX_PAIR, FOX_PAIR, t, LANES)
    fox_pair0 = (N_KH - FOX_HEADS) // FOX_PAIR
    v5 = fvt.reshape(nb, n_pair, FOX_PAIR, VT_ROWS, t)
    return pl.pallas_call(
        functools.partial(_fox_prompt_kernel, tq=tq, tk=tk),
        grid=(nb, n_pair, t // tq),
        in_specs=[
            pl.BlockSpec((None, None, FOX_PAIR, HEAD_DIM, tq), lambda b, hp, qt: (b, hp, 0, 0, qt)),
            pl.BlockSpec((None, None, FOX_PAIR, t, LANES), lambda b, hp, qt: (b, fox_pair0 + hp, 0, 0, 0)),
            pl.BlockSpec((None, None, FOX_PAIR, VT_ROWS, t), lambda b, hp, qt: (b, hp, 0, 0, 0)),
            pl.BlockSpec((t, FOX_HEADS), lambda b, hp, qt: (b, 0)),
            pl.BlockSpec(tril.shape, lambda b, hp, qt: (0, 0)),
        ],
        out_specs=pl.BlockSpec((None, tq, FOX_PAIR * HEAD_DIM), lambda b, hp, qt: (b, qt, hp)),
        out_shape=jax.ShapeDtypeStruct((nb, t, FOX_HEADS * HEAD_DIM), BF16),
        scratch_shapes=[
            pltpu.VMEM((FOX_PAIR, t, LANES), BF16),
            pltpu.VMEM((FOX_PAIR, 1, tq), F32), pltpu.VMEM((FOX_PAIR, VT_ROWS, tq), F32),
        ],
        compiler_params=_cparams(("parallel", "parallel", "arbitrary")),
        name="fox_prompt",
    )(q5, k5, v5, logf, tril)


def _merge_ln_kernel(oa_ref, ob_ref, gm_ref, h_ref, wa_ref, wb_ref, wo_ref, g_ref, b_ref, o_ref, *, alpha):
    pa = _dot(oa_ref[...], wa_ref[...])
    pb = _dot(ob_ref[...], wb_ref[...])
    merged = gm_ref[:, :D_MODEL] * pa + gm_ref[:, D_MODEL:] * pb
    y = alpha * h_ref[...] + _dot(merged.astype(BF16), wo_ref[...])
    mu = jnp.mean(y, axis=-1, keepdims=True)
    d = y - mu
    var = jnp.mean(d * d, axis=-1, keepdims=True)
    o_ref[...] = d * lax.rsqrt(var + LN_EPS) * g_ref[...] + b_ref[...]


def _merge_ln(oa, ob, gm, h, wa, wb, wo, g, b, alpha, tm):
    n, d = h.shape
    row = lambda i: (i, 0)
    fixed = lambda i: (0, 0)
    return pl.pallas_call(
        functools.partial(_merge_ln_kernel, alpha=alpha),
        grid=(n // tm,),
        in_specs=[
            pl.BlockSpec((tm, 512), row), pl.BlockSpec((tm, 512), row),
            pl.BlockSpec((tm, 2 * d), row), pl.BlockSpec((tm, d), row),
            pl.BlockSpec((512, d), fixed), pl.BlockSpec((512, d), fixed), pl.BlockSpec((d, d), fixed),
            pl.BlockSpec((1, d), fixed), pl.BlockSpec((1, d), fixed),
        ],
        out_specs=pl.BlockSpec((tm, d), row),
        out_shape=jax.ShapeDtypeStruct((n, d), F32),
        compiler_params=_cparams(("parallel",)),
        name="merge_ln",
    )(oa, ob, gm, h, wa, wb, wo, g, b)


SAMPLE_ROWS = 32


def _row_slopes(rows, width, head0=0):
    row = lax.broadcasted_iota(jnp.int32, (rows, width), 0)
    head = head0 + (row >> 2)
    slope = lax.bitcast_convert_type((126 - head) << 23, F32)
    return slope, row & 3


def _sample_select_kernel(q_ref, kc_ref, vc_ref, ovl_ref, ocmp_ref, sel_ref, *, q_off, n_cmp, n_slc, t_new):
    n_key = kc_ref.shape[1]
    hrows = SAMPLE_ROWS // NSA_KV_HEADS
    lane = lax.broadcasted_iota(jnp.int32, (hrows, n_key), 1)
    ps, os_ = [], []
    for g in range(NSA_KV_HEADS):
        slope, tok = _row_slopes(hrows, n_key, head0=g * NSA_GROUP)
        dist = (q_off + tok) - (lane * CMP_STRIDE + (CMP_BLOCK - 1))
        valid = (dist >= 0) & (lane < n_cmp)
        s = _dot_nt(q_ref[g * hrows:(g + 1) * hrows], kc_ref[g]) - slope * dist.astype(F32)
        s = jnp.where(valid, s, NEG)
        m = jnp.max(s, axis=1, keepdims=True)
        p = jnp.where(valid, jnp.exp(s - m), 0.0)
        den = jnp.sum(p, axis=1, keepdims=True)
        p = p / jnp.where(den > 0.0, den, 1.0)
        ps.append(p)
        os_.append(_dot(p.astype(BF16), vc_ref[g]))
    ocmp_ref[...] = jnp.concatenate(os_, axis=0)

    p_all = jnp.concatenate(ps, axis=0)
    rr = lax.broadcasted_iota(jnp.int32, (2 * 8, SAMPLE_ROWS), 0)
    cc = lax.broadcasted_iota(jnp.int32, (2 * 8, SAMPLE_ROWS), 1)
    gather = jnp.where(((rr >> 3) == (cc >> 4)) & ((rr & 7) == (cc & 3)), 1.0, 0.0).astype(BF16)
    hi, mid, lo = _split3(p_all)
    imp = _dot(gather, hi) + _dot(gather, mid) + _dot(gather, lo)
    imp_slc = _dot3(imp, ovl_ref[...])
    nb_pad = imp_slc.shape[1]
    blk = lax.broadcasted_iota(jnp.int32, (16, nb_pad), 1)
    blk_f = blk.astype(F32)
    tpos = q_off + (lax.broadcasted_iota(jnp.int32, (16, nb_pad), 0) & 7)
    cur = tpos >> 6
    forced = (blk == 0) | (blk == cur) | (blk == cur - 1)
    score = jnp.where(forced, FORCE, jnp.where(blk * SEL_BLOCK <= tpos, imp_slc, -FORCE))
    score = jnp.where(blk < n_slc, score, -3.0 * FORCE)
    sel = jnp.zeros((16, nb_pad), F32)
    for _ in range(min(N_SEL, n_slc)):
        mx = jnp.max(score, axis=1, keepdims=True)
        first = jnp.min(jnp.where(score == mx, blk_f, float(nb_pad)), axis=1, keepdims=True)
        hit = blk_f == first
        sel = jnp.where(hit, 1.0, sel)
        score = jnp.where(hit, -4.0 * FORCE, score)
    sel_ref[...] = sel


def _sample_select(q32, kcvc, ovl, q_off, n_cmp, n_slc, t_new):
    nb = q32.shape[0]
    n_key = kcvc.shape[2]
    nb_pad = ovl.shape[1]
    return pl.pallas_call(
        functools.partial(_sample_select_kernel, q_off=q_off, n_cmp=n_cmp, n_slc=n_slc, t_new=t_new),
        grid=(nb,),
        in_specs=[
            pl.BlockSpec((None, SAMPLE_ROWS, LANES), lambda b: (b, 0, 0)),
            pl.BlockSpec((None, None, 2, n_key, LANES), lambda b: (b, 0, 0, 0, 0)),
            pl.BlockSpec((None, None, 2, n_key, LANES), lambda b: (b, 1, 0, 0, 0)),
            pl.BlockSpec((n_key, nb_pad), lambda b: (0, 0)),
        ],
        out_specs=[
            pl.BlockSpec((None, SAMPLE_ROWS, LANES), lambda b: (b, 0, 0)),
            pl.BlockSpec((None, 16, nb_pad), lambda b: (b, 0, 0)),
        ],
        out_shape=[
            jax.ShapeDtypeStruct((nb, SAMPLE_ROWS, LANES), F32),
            jax.ShapeDtypeStruct((nb, 16, nb_pad), F32),
        ],
        compiler_params=_cparams(("parallel",)),
        name="sample_select",
    )(q32, kcvc.reshape(nb, 2, 2, n_key, LANES), kcvc.reshape(nb, 2, 2, n_key, LANES), ovl)


TAIL = 128


def _softmax_update_pages(s, vts, m_ref, l_ref, acc_ref):
    m_prev = m_ref[...]
    m_new = jnp.maximum(m_prev, jnp.max(s, axis=1, keepdims=True))
    a = jnp.exp(m_prev - m_new)
    p = jnp.exp(s - m_new)
    l_ref[...] = a * l_ref[...] + jnp.sum(p, axis=1, keepdims=True)
    pb = p.astype(BF16)
    pv = None
    for j, vt in enumerate(vts):
        d = _dot_nt(pb[:, j * PAGE:(j + 1) * PAGE], vt)
        pv = d if pv is None else pv + d
    acc_ref[...] = a * acc_ref[...] + pv
    m_ref[...] = m_new


def _tail_scores(q, knew_t, t_new):
    s = _dot(q, knew_t)
    slope, tok = _row_slopes(SAMPLE_ROWS, TAIL)
    j = lax.broadcasted_iota(jnp.int32, s.shape, 1)
    d = tok - j
    s = s - slope * d.astype(F32)
    return jnp.where((d >= 0) & (j < t_new), s, NEG)


def _sample_nsa_kernel(pt_ref, *refs, n_pages, n_steps, q_off, t_new):
    k_refs = refs[:n_pages]
    v_refs = refs[n_pages:2 * n_pages]
    (q_ref, selx_ref, exp_ref, knew_ref, vnew_ref, wk_ref, wv_ref, wknew_ref, wvnew_ref, ocmp_ref, gate_ref,
     o_ref, m_ref, l_ref, acc_ref) = refs[2 * n_pages:]
    pg = pl.program_id(1)
    n_key = n_pages * PAGE
    q = q_ref[...]
    slope, tok = _row_slopes(SAMPLE_ROWS, n_key)

    @pl.when(pg == 0)
    def _():
        m_ref[...] = jnp.full_like(m_ref, NEG)
        l_ref[...] = jnp.zeros_like(l_ref)
        acc_ref[...] = jnp.zeros_like(acc_ref)

    s = jnp.concatenate([_dot(q, r[...].astype(BF16)) for r in k_refs], axis=1)
    kpos = pg * n_key + lax.broadcasted_iota(jnp.int32, (SAMPLE_ROWS, n_key), 1)
    dist = (q_off + tok) - kpos
    s = s - slope * dist.astype(F32)
    chosen = _dot(selx_ref[...], exp_ref[...]) > 0.5
    s = jnp.where(chosen & (dist >= 0), s, NEG)
    _softmax_update_pages(s, [r[...].astype(BF16) for r in v_refs], m_ref, l_ref, acc_ref)

    @pl.when(pg == n_steps - 1)
    def _():
        lane = lax.broadcasted_iota(jnp.int32, (SAMPLE_ROWS, LANES), 1)
        row = lax.broadcasted_iota(jnp.int32, (SAMPLE_ROWS, LANES), 0)

        def own_group(x):
            return jnp.where(lane < HEAD_DIM, jnp.where(row < 16, x, pltpu.roll(x, HEAD_DIM, axis=1)), 0.0)

        st = _tail_scores(q, knew_ref[...], t_new)
        _softmax_update(st, vnew_ref[...], m_ref, l_ref, acc_ref)
        o_sel = own_group(acc_ref[...] / l_ref[...])

        sw = _dot(q, wk_ref[...].astype(BF16))
        wslope, wtok = _row_slopes(SAMPLE_ROWS, WINDOW)
        wpos = (q_off - WINDOW) + lax.broadcasted_iota(jnp.int32, (SAMPLE_ROWS, WINDOW), 1)
        wd = (q_off + wtok) - wpos
        sw = sw - wslope * wd.astype(F32)
        sw = jnp.where((wd >= 0) & (wd < WINDOW) & (wpos >= 0), sw, NEG)
        swt = _tail_scores(q, wknew_ref[...], t_new)
        mw = jnp.maximum(jnp.max(sw, axis=1, keepdims=True), jnp.max(swt, axis=1, keepdims=True))
        pw = jnp.exp(sw - mw)
        pwt = jnp.exp(swt - mw)
        lw = jnp.sum(pw, axis=1, keepdims=True) + jnp.sum(pwt, axis=1, keepdims=True)
        ow = _dot_nt(pw.astype(BF16), wv_ref[...].astype(BF16)) + _dot(pwt.astype(BF16), wvnew_ref[...])
        o_win = own_group(ow / lw)

        gate = gate_ref[...]
        o_ref[...] = gate[:, 0:1] * ocmp_ref[...] + gate[:, 1:2] * o_sel + gate[:, 2:3] * o_win


def _sample_nsa(pool_t, page_table, q32, selx, expand, knew_t, vnew, wbuf_t, wknew_t, wvnew, ocmp, gate32,
                n_pages, q_off, t_new):
    nb, pages_per_seq = page_table.shape
    n_steps = pages_per_seq // n_pages
    page_specs = [
        pl.BlockSpec((None, None, LANES, PAGE), lambda b, pg, pt, j=j, c=c: (pt[b, pg * n_pages + j], c, 0, 0))
        for c in (2, 3) for j in range(n_pages)
    ]
    per_b = lambda shape: pl.BlockSpec((None,) + shape, lambda b, pg, pt: (b,) + (0,) * len(shape))
    win_spec = lambda c: pl.BlockSpec((None, None, LANES, WINDOW), lambda b, pg, pt: (b, c, 0, 0))
    grid_spec = pltpu.PrefetchScalarGridSpec(
        num_scalar_prefetch=1,
        grid=(nb, n_steps),
        in_specs=page_specs + [
            per_b((SAMPLE_ROWS, LANES)),
            pl.BlockSpec((None, None, SAMPLE_ROWS, LANES), lambda b, pg, pt: (b, pg, 0, 0)),
            pl.BlockSpec((LANES, n_pages * PAGE), lambda b, pg, pt: (0, 0)),
            per_b((LANES, TAIL)), per_b((TAIL, LANES)),
            win_spec(0), win_spec(1),
            per_b((LANES, TAIL)), per_b((TAIL, LANES)),
            per_b((SAMPLE_ROWS, LANES)), per_b((SAMPLE_ROWS, LANES)),
        ],
        out_specs=per_b((SAMPLE_ROWS, LANES)),
        scratch_shapes=[
            pltpu.VMEM((SAMPLE_ROWS, 1), F32), pltpu.VMEM((SAMPLE_ROWS, 1), F32),
            pltpu.VMEM((SAMPLE_ROWS, LANES), F32),
        ],
    )
    return pl.pallas_call(
        functools.partial(_sample_nsa_kernel, n_pages=n_pages, n_steps=n_steps, q_off=q_off, t_new=t_new),
        grid_spec=grid_spec,
        out_shape=jax.ShapeDtypeStruct((nb, SAMPLE_ROWS, LANES), F32),
        compiler_params=_cparams(("parallel", "arbitrary")),
        name="sample_nsa",
    )(page_table, *([pool_t] * (2 * n_pages)), q32, selx, expand, knew_t, vnew, wbuf_t, wbuf_t, wknew_t, wvnew,
      ocmp, gate32)


def _sample_fox_kernel(pt_ref, *refs, n_pages, n_steps, t_new):
    k_refs = refs[:n_pages]
    v_refs = refs[n_pages:2 * n_pages]
    lf_refs = refs[2 * n_pages:3 * n_pages]
    (q_ref, triu_ref, knew_ref, vnew_ref, lfnew_ref,
     o_ref, m_ref, l_ref, acc_ref, carry_ref) = refs[3 * n_pages:]
    pg = pl.program_id(1)
    q = q_ref[...]

    @pl.when(pg == 0)
    def _():
        m_ref[...] = jnp.full_like(m_ref, NEG)
        l_ref[...] = jnp.zeros_like(l_ref)
        acc_ref[...] = jnp.zeros_like(acc_ref)
        carry_ref[...] = jnp.zeros_like(carry_ref)

    triu = triu_ref[...]

    def page_bias(lf, before):
        c = _dot3(lf, triu) + before
        return jnp.concatenate([c] * (SAMPLE_ROWS // FOX_HEADS), axis=0), before + jnp.sum(lf, axis=1, keepdims=True)

    carry = carry_ref[...]
    scores = []
    for j in range(n_pages):
        bias, carry = page_bias(lf_refs[j][...], carry)
        scores.append(_dot(q, k_refs[j][...].astype(BF16)) - bias)
    carry_ref[...] = carry
    _softmax_update_pages(jnp.concatenate(scores, axis=1), [r[...].astype(BF16) for r in v_refs],
                          m_ref, l_ref, acc_ref)

    @pl.when(pg == n_steps - 1)
    def _():
        bias_n, _ = page_bias(lfnew_ref[...], carry)
        sn = _dot(q, knew_ref[...]) - bias_n
        j = lax.broadcasted_iota(jnp.int32, sn.shape, 1)
        tok = lax.broadcasted_iota(jnp.int32, sn.shape, 0) >> 3
        sn = jnp.where((j <= tok) & (j < t_new), sn, NEG)
        _softmax_update(sn, vnew_ref[...], m_ref, l_ref, acc_ref)
        o_ref[...] = acc_ref[...] / l_ref[...]


def _sample_fox(pool_t, lf_t, page_table, qbd, triu, knew_t, vnew, lfnew_t, n_pages, t_new):
    nb, pages_per_seq = page_table.shape
    n_steps = pages_per_seq // n_pages
    width = FOX_HEADS * HEAD_DIM
    kv_specs = [
        pl.BlockSpec((None, None, width, PAGE), lambda b, pg, pt, j=j, c=c: (pt[b, pg * n_pages + j], c, 0, 0))
        for c in range(2) for j in range(n_pages)
    ]
    lf_specs = [
        pl.BlockSpec((None, FOX_HEADS, PAGE), lambda b, pg, pt, j=j: (pt[b, pg * n_pages + j], 0, 0))
        for j in range(n_pages)
    ]
    per_b = lambda shape: pl.BlockSpec((None,) + shape, lambda b, pg, pt: (b,) + (0,) * len(shape))
    grid_spec = pltpu.PrefetchScalarGridSpec(
        num_scalar_prefetch=1,
        grid=(nb, n_steps),
        in_specs=kv_specs + lf_specs + [
            per_b((SAMPLE_ROWS, width)),
            pl.BlockSpec((PAGE, PAGE), lambda b, pg, pt: (0, 0)),
            per_b((width, TAIL)), per_b((TAIL, width)), per_b((FOX_HEADS, TAIL)),
        ],
        out_specs=per_b((SAMPLE_ROWS, width)),
        scratch_shapes=[
            pltpu.VMEM((SAMPLE_ROWS, 1), F32), pltpu.VMEM((SAMPLE_ROWS, 1), F32),
            pltpu.VMEM((SAMPLE_ROWS, width), F32), pltpu.VMEM((FOX_HEADS, 1), F32),
        ],
    )
    return pl.pallas_call(
        functools.partial(_sample_fox_kernel, n_pages=n_pages, n_steps=n_steps, t_new=t_new),
        grid_spec=grid_spec,
        out_shape=jax.ShapeDtypeStruct((nb, SAMPLE_ROWS, width), F32),
        compiler_params=_cparams(("parallel", "arbitrary")),
        name="sample_fox",
    )(page_table, *([pool_t] * (2 * n_pages)), *([lf_t] * n_pages), qbd, triu, knew_t, vnew, lfnew_t)


def _alibi_key_cols(pos):
    tab = np.zeros((len(pos), LANES), np.float32)
    tab[:, ALIBI0] = 1.0
    tab[:, ALIBI0 + 1] = 1.0
    tab[:, ALIBI0 + 2] = (pos // 128) * 128
    tab[:, ALIBI0 + 3] = pos % 128
    return tab


def _prompt_consts(t):
    pos = np.arange(t)
    kaug_win = _alibi_key_cols(pos)
    kaug_sel = kaug_win.copy()
    kaug_sel[pos, SELBIT0 + pos // SEL_BLOCK] = -MASK_BIG
    n_chunk = t // CMP_STRIDE
    kaug_cmp = _alibi_key_cols(np.arange(n_chunk) * CMP_STRIDE + CMP_BLOCK - 1)
    n_cmp = (t - CMP_BLOCK) // CMP_STRIDE + 1
    n_slc = -(-t // SEL_BLOCK)
    ovlt = np.zeros((n_slc, n_chunk), np.float32)
    cs = np.arange(n_cmp)[None, :] * CMP_STRIDE
    ss = np.arange(n_slc)[:, None] * SEL_BLOCK
    ovlt[:, :n_cmp] = (cs < ss + SEL_BLOCK) & (cs + CMP_BLOCK > ss)
    tril = np.tril(np.ones((512, 512), np.float32))
    as_bf = lambda a: jnp.asarray(a, BF16)
    return dict(kaug_sel=as_bf(kaug_sel), kaug_win=as_bf(kaug_win), kaug_cmp=as_bf(kaug_cmp),
                ovlt=as_bf(ovlt), tril=as_bf(tril))


def _sample_ovl(n_key, n_cmp, n_slc, nb_pad):
    ovl = np.zeros((n_key, nb_pad), np.float32)
    cs = np.arange(n_cmp)[:, None] * CMP_STRIDE
    ss = np.arange(n_slc)[None, :] * SEL_BLOCK
    ovl[:n_cmp, :n_slc] = (cs < ss + SEL_BLOCK) & (cs + CMP_BLOCK > ss)
    return jnp.asarray(ovl, BF16)


def _perm_in_proj(w_in, b_in):
    o_qa, o_kva, o_ga, o_fox, o_fb, o_gm = 0, 512, 1280, 1304, 2840, 2848
    d = w_in.shape[0]
    o_fk = o_fox + FOX_HEADS * HEAD_DIM
    o_fv = o_fk + FOX_HEADS * HEAD_DIM
    o_sk = o_kva + 4 * HEAD_DIM
    o_wk = o_kva + 8 * HEAD_DIM

    def cols(x, zeros):
        return jnp.concatenate([x[..., o_kva:o_sk], x[..., o_sk:o_sk + 2 * HEAD_DIM], x[..., o_wk:o_wk + 2 * HEAD_DIM],
                                x[..., o_fk:o_fv], x[..., o_gm:],
                                x[..., o_fb:o_gm], zeros(LANES - FOX_HEADS)], axis=-1)

    def rows(x, zeros):
        return jnp.concatenate([x[..., o_qa:o_ga], x[..., o_fox:o_fb],
                                x[..., o_ga:o_fox], zeros(R_FB - R_GA - 3 * NSA_HEADS),
                                x[..., o_fb:o_gm], zeros(R_TOT - R_FB - FOX_HEADS)], axis=-1)

    w = cols(w_in, lambda k: jnp.zeros((d, k), w_in.dtype)).astype(BF16)
    b = cols(b_in, lambda k: jnp.zeros((k,), b_in.dtype)).reshape(1, C_TOT)
    wt = rows(w_in, lambda k: jnp.zeros((d, k), w_in.dtype)).T.astype(BF16)
    bt = rows(b_in, lambda k: jnp.zeros((k,), b_in.dtype)).reshape(R_TOT, 1)
    return w, b, wt, bt


def _compress_weights(pe_k, w1_k, w2_k, pe_v, w1_v, w2_v):
    n_sub = CMP_BLOCK // CMP_STRIDE

    def per_row(w1):
        w = w1.reshape(n_sub, CMP_STRIDE, HEAD_DIM, CMP_HIDDEN)
        return w.transpose(1, 2, 0, 3).reshape(CMP_STRIDE, HEAD_DIM, n_sub * CMP_HIDDEN)

    def per_kind(w1):
        blk = per_row(w1)
        zero = jnp.zeros_like(blk)
        w = jnp.stack([jnp.concatenate([blk, zero], axis=-1), jnp.concatenate([zero, blk], axis=-1)], axis=1)
        return w.reshape(CMP_STRIDE * NSA_KV_HEADS * HEAD_DIM, NSA_KV_HEADS * n_sub * CMP_HIDDEN)

    wr = jnp.stack([per_kind(w1_k), per_kind(w1_v)]).astype(BF16)
    pe8 = jnp.concatenate([pe_k.reshape(1, -1), pe_v.reshape(1, -1), jnp.zeros((14, CMP_BLOCK * HEAD_DIM), F32)], axis=0)
    w1cat = jnp.concatenate([w1_k, w1_v], axis=1).astype(BF16)
    pad = jnp.zeros((CMP_HIDDEN, LANES - HEAD_DIM), F32)
    w2pad = jnp.stack([jnp.concatenate([w2_k, pad], axis=1), jnp.concatenate([w2_v, pad], axis=1)]).astype(BF16)
    return dict(wr=wr, pe8=pe8.astype(BF16), w1cat=w1cat, w2pad=w2pad)


def _prompt_mixer(h1, w, nb, t):
    cmp_rows, logf, kh, gm, nsat, wint, foxt, logft, qat, nvt, fqt, fvt, gat = _in_proj(
        h1, w["w_in"], w["b_in"], w["wt"], w["bt"], nb, t, tm=TM_PROJ)
    consts = _prompt_consts(t)
    kcvc, vct = _compress_prompt(cmp_rows, w["cmp"], nb, t)
    o_a = _nsa_prompt(qat, kh, nvt, kcvc, vct, gat, consts, nb, t, tq=NSA_TQ, tk=NSA_TK)
    o_b = _fox_prompt(fqt, kh, fvt, logf, consts["tril"], nb, t, tq=FOX_TQ, tk=FOX_TK)
    n = nb * t
    state = (nsat.reshape(nb, 4, NSA_KV_HEADS, HEAD_DIM, t).transpose(0, 4, 1, 2, 3),
             wint.reshape(nb, 2, NSA_KV_HEADS, HEAD_DIM, t).transpose(0, 4, 1, 2, 3),
             foxt.reshape(nb, 2, FOX_HEADS, HEAD_DIM, t).transpose(0, 4, 1, 2, 3),
             logft.transpose(0, 2, 1))
    return o_a.reshape(n, -1), o_b.reshape(n, -1), gm, state


def _sample_mixer(h1, w, nb, t_new, nsa_pool, fox_pool, logf_pool, win_buf, page_table):
    n = nb * t_new
    past = page_table.shape[1] * PAGE
    _, _, _, gm, nsat, wint, foxt, logft, qat, _, fqt, _, gat = _in_proj(
        h1, w["w_in"], w["b_in"], w["wt"], w["bt"], 1, n, tm=n)
    nsa, win, fox, logf = nsat[0].T, wint[0].T, foxt[0].T, logft[0].T
    ga = gat[:3 * NSA_HEADS].T
    n_pool = nsa_pool.shape[0]
    nsa_t = nsa_pool.transpose(0, 2, 3, 4, 1).reshape(n_pool, 4, NSA_KV_HEADS * HEAD_DIM, PAGE)
    fox_t = fox_pool.transpose(0, 2, 3, 4, 1).reshape(n_pool, 2, FOX_HEADS * HEAD_DIM, PAGE)
    lf_t = logf_pool.transpose(0, 2, 1)
    wbuf_t = win_buf.transpose(0, 2, 3, 4, 1).reshape(nb, 2, NSA_KV_HEADS * HEAD_DIM, win_buf.shape[1])
    assert win_buf.shape[1] == WINDOW and t_new == 4 and past % SEL_BLOCK == 0

    seq_len = past + t_new
    n_cmp = (seq_len - CMP_BLOCK) // CMP_STRIDE + 1
    n_slc = -(-seq_len // SEL_BLOCK)
    n_chunk = past // CMP_STRIDE
    assert n_cmp <= n_chunk
    kcvc = _compress_sample(nsa_t, page_table, w["cmp"], n_pages=CMP_PAGES)

    q = qat[0].reshape(NSA_HEADS, HEAD_DIM, nb, t_new).transpose(2, 0, 3, 1).reshape(nb, SAMPLE_ROWS, HEAD_DIM)
    q = jnp.pad(q, ((0, 0), (0, 0), (0, LANES - HEAD_DIM)))
    nb_pad = -(-n_slc // LANES) * LANES
    ocmp, sel = _sample_select(q, kcvc, _sample_ovl(n_chunk, n_cmp, n_slc, nb_pad), past, n_cmp, n_slc, t_new)

    n_pages = NSA_PAGES
    n_steps = past // (n_pages * PAGE)
    blk_per_step = n_pages * PAGE // SEL_BLOCK
    selg = sel.reshape(nb, 2, 8, nb_pad)[:, :, :t_new, :n_steps * blk_per_step]
    selg = selg.reshape(nb, 2, 1, t_new, n_steps, blk_per_step)
    selx = jnp.broadcast_to(selg, (nb, 2, NSA_GROUP, t_new, n_steps, blk_per_step))
    selx = selx.transpose(0, 4, 1, 2, 3, 5).reshape(nb, n_steps, SAMPLE_ROWS, blk_per_step)
    selx = jnp.pad(selx, ((0, 0), (0, 0), (0, 0), (0, LANES - blk_per_step))).astype(BF16)
    expand = np.zeros((LANES, n_pages * PAGE), np.float32)
    expand[np.arange(n_pages * PAGE) // SEL_BLOCK, np.arange(n_pages * PAGE)] = 1.0
    q_sel = jnp.concatenate([q[:, :16], jnp.roll(q[:, 16:], HEAD_DIM, axis=-1)], axis=1)

    def new_rows(x, c0, width=LANES):
        r = x.reshape(nb, t_new, -1)[:, :, c0:c0 + width]
        return jnp.pad(r, ((0, 0), (0, TAIL - t_new), (0, 0))).astype(BF16)

    def new_cols(x, c0, width=LANES):
        return new_rows(x, c0, width).transpose(0, 2, 1)

    gate32 = ga.reshape(nb, t_new, 3, NSA_HEADS).transpose(0, 3, 1, 2).reshape(nb, SAMPLE_ROWS, 3)
    gate32 = jnp.pad(gate32, ((0, 0), (0, 0), (0, LANES - 3)))
    o_a32 = _sample_nsa(nsa_t, page_table, q_sel, selx, jnp.asarray(expand, BF16),
                        new_cols(nsa, 256), new_rows(nsa, 384), wbuf_t, new_cols(win, 0), new_rows(win, 128),
                        ocmp, gate32, n_pages, past, t_new)
    o_a = o_a32[:, :, :HEAD_DIM].reshape(nb, NSA_HEADS, t_new, HEAD_DIM).transpose(0, 2, 1, 3)
    o_a = o_a.reshape(n, NSA_HEADS * HEAD_DIM).astype(BF16)

    width = FOX_HEADS * HEAD_DIM
    qf = fqt[0].reshape(FOX_HEADS, HEAD_DIM, nb, t_new).transpose(2, 3, 0, 1)
    eye = jnp.eye(FOX_HEADS, dtype=qf.dtype)
    qbd = (qf[:, :, :, None, :] * eye[None, None, :, :, None]).reshape(nb, SAMPLE_ROWS, width)
    triu = jnp.asarray(np.triu(np.ones((PAGE, PAGE), np.float32)), BF16)
    lfnew_t = jnp.pad(logf.reshape(nb, t_new, FOX_HEADS), ((0, 0), (0, TAIL - t_new), (0, 0))).transpose(0, 2, 1)
    o_b32 = _sample_fox(fox_t, lf_t, page_table, qbd, triu, new_cols(fox, 0, width), new_rows(fox, width, width),
                        lfnew_t, FOX_PAGES, t_new)
    o_b = o_b32.reshape(nb, t_new, FOX_HEADS, FOX_HEADS, HEAD_DIM)
    o_b = jnp.einsum("bthgd,hg->bthd", o_b, jnp.eye(FOX_HEADS, dtype=o_b.dtype))
    o_b = o_b.reshape(n, width).astype(BF16)
    state = (nsa.reshape(nb, t_new, 4, NSA_KV_HEADS, HEAD_DIM), win.reshape(nb, t_new, 2, NSA_KV_HEADS, HEAD_DIM),
             fox.reshape(nb, t_new, 2, FOX_HEADS, HEAD_DIM), logf.reshape(nb, t_new, FOX_HEADS))
    return o_a, o_b, gm, state


def _layer(x, w, alpha, mixer, tm):
    h1 = _ffn_ln(x, w["ffn1_wg"], w["ffn1_wu"], w["ffn1_wd"], w["ln1_g"], w["ln1_b"], alpha, tm)
    o_a, o_b, gm, state = mixer(h1, w)
    h2 = _merge_ln(o_a, o_b, gm, h1, w["w_proj_a"], w["w_proj_b"], w["w_out"], w["ln2_g"], w["ln2_b"], alpha, tm)
    y = _ffn_ln(h2, w["ffn2_wg"], w["ffn2_wu"], w["ffn2_wd"], w["ln3_g"], w["ln3_b"], alpha, tm)
    return y, state


def kernel(x_prompt, x_sample, cache_nsa_kv, cache_fox_kv, cache_fox_logf, state_win_kv, page_table,
           ln1_g, ln1_b, ffn1_w_gate, ffn1_w_up, ffn1_w_down, w_in, b_in,
           cmp_pe_k, cmp_w1_k, cmp_w2_k, cmp_pe_v, cmp_w1_v, cmp_w2_v,
           w_proj_a, w_proj_b, w_out, ln2_g, ln2_b,
           ffn2_w_gate, ffn2_w_up, ffn2_w_down, ln3_g, ln3_b):
    depth = w_in.shape[0]
    nb_p, t_p, d = x_prompt.shape
    nb_s, t_s, _ = x_sample.shape
    alpha = float((2.0 * depth) ** 0.25)
    h_p = x_prompt.reshape(nb_p * t_p, d)
    h_s = x_sample.reshape(nb_s * t_s, d)
    st_p, st_s = [], []
    for l in range(depth):
        wi, bi, wt, bt = _perm_in_proj(w_in[l], b_in[l])
        vec = lambda a: a[l].reshape(1, d)
        w = dict(
            ln1_g=vec(ln1_g), ln1_b=vec(ln1_b), ln2_g=vec(ln2_g), ln2_b=vec(ln2_b), ln3_g=vec(ln3_g), ln3_b=vec(ln3_b),
            ffn1_wg=ffn1_w_gate[l].astype(BF16), ffn1_wu=ffn1_w_up[l].astype(BF16), ffn1_wd=ffn1_w_down[l].astype(BF16),
            ffn2_wg=ffn2_w_gate[l].astype(BF16), ffn2_wu=ffn2_w_up[l].astype(BF16), ffn2_wd=ffn2_w_down[l].astype(BF16),
            w_in=wi, b_in=bi, wt=wt, bt=bt,
            cmp=_compress_weights(cmp_pe_k[l], cmp_w1_k[l], cmp_w2_k[l], cmp_pe_v[l], cmp_w1_v[l], cmp_w2_v[l]),
            w_proj_a=w_proj_a[l].astype(BF16), w_proj_b=w_proj_b[l].astype(BF16), w_out=w_out[l].astype(BF16),
        )
        h_p, s_p = _layer(h_p, w, alpha, functools.partial(_prompt_mixer, nb=nb_p, t=t_p), tm=TM_FFN)
        h_s, s_s = _layer(h_s, w, alpha, functools.partial(
            _sample_mixer, nb=nb_s, t_new=t_s, nsa_pool=cache_nsa_kv[l], fox_pool=cache_fox_kv[l],
            logf_pool=cache_fox_logf[l], win_buf=state_win_kv[l], page_table=page_table), tm=nb_s * t_s)
        st_p.append(s_p)
        st_s.append(s_s)

    def states(st, nb, t, win_prev):
        nsa = jnp.stack([s[0] for s in st])
        fox = jnp.stack([s[2] for s in st])
        logf = jnp.stack([s[3] for s in st])
        wins = []
        for l, s in enumerate(st):
            wr = s[1]
            if win_prev is None:
                wins.append(wr[:, t - min(WINDOW, t):])
            else:
                wins.append(jnp.concatenate([win_prev[l], wr], axis=1)[:, t:])
        return nsa, fox, logf, jnp.stack(wins)

    nsa_p, fox_p, logf_p, win_p = states(st_p, nb_p, t_p, None)
    nsa_s, fox_s, logf_s, win_s = states(st_s, nb_s, t_s, state_win_kv)
    return (h_p.reshape(nb_p, t_p, d), h_s.reshape(nb_s, t_s, d), nsa_p, fox_p, logf_p, win_p,
            nsa_s, fox_s, logf_s, win_s)
```

```python
import functools

import numpy as np
import jax
import jax.numpy as jnp
from jax import lax
from jax.experimental import pallas as pl
from jax.experimental.pallas import tpu as pltpu

F32 = jnp.float32
BF16 = jnp.bfloat16

LANES = 128
VMEM_LIMIT = 56 * 1024 * 1024

TM_FFN = 512
TM_PROJ = 256
NSA_TQ, NSA_TK = 512, 512
FOX_TQ, FOX_TK = 1024, 512
CMP_PAGES = 32
NSA_PAGES = 32
FOX_PAGES = 16

D_MODEL = 1024
HEAD_DIM = 64
NSA_HEADS = 8
NSA_KV_HEADS = 2
NSA_GROUP = NSA_HEADS // NSA_KV_HEADS
FOX_HEADS = 8
CMP_BLOCK = 32
CMP_STRIDE = 16
CMP_HIDDEN = 2 * HEAD_DIM
SEL_BLOCK = 64
N_SEL = 16
WINDOW = 512
PAGE = 128
LN_EPS = 1e-5
NEG = -1e30
FORCE = 1e9
SCALE = HEAD_DIM ** -0.5
MASK_BIG = 30000.0

SELBIT0 = 64
ALIBI0 = 96


def _dot(a, b):
    return jnp.dot(a, b, preferred_element_type=F32)


def _dot_nt(a, b):
    return lax.dot_general(a, b, (((1,), (1,)), ((), ())), preferred_element_type=F32)


def _split3(x):
    hi = x.astype(BF16)
    r1 = x - hi.astype(F32)
    mid = r1.astype(BF16)
    lo = (r1 - mid.astype(F32)).astype(BF16)
    return hi, mid, lo


def _dot3(x, w):
    hi, mid, lo = _split3(x)
    return _dot(hi, w) + _dot(mid, w) + _dot(lo, w)


def _dot3_nt(a, x):
    hi, mid, lo = _split3(x)
    return _dot_nt(a, hi) + _dot_nt(a, mid) + _dot_nt(a, lo)


def _sigmoid(x):
    return 1.0 / (1.0 + jnp.exp(-x))


def _log_sigmoid(x):
    return -(jnp.maximum(-x, 0.0) + jnp.log(1.0 + jnp.exp(-jnp.abs(x))))


def _gelu_tanh(x):
    c = np.float32(np.sqrt(2.0 / np.pi))
    return 0.5 * x * (1.0 + jnp.tanh(c * (x + np.float32(0.044715) * (x * x * x))))


def _cparams(sem):
    return pltpu.CompilerParams(dimension_semantics=sem, vmem_limit_bytes=VMEM_LIMIT)


def _ffn_ln_kernel(x_ref, wg_ref, wu_ref, wd_ref, g_ref, b_ref, o_ref, acc_ref, *, alpha, n_ff):
    j = pl.program_id(1)

    @pl.when(j == 0)
    def _():
        acc_ref[...] = jnp.zeros_like(acc_ref)

    xb = x_ref[...].astype(BF16)
    gate = _dot(xb, wg_ref[...])
    up = _dot(xb, wu_ref[...])
    mid = (gate * _sigmoid(gate) * up).astype(BF16)
    acc_ref[...] += _dot(mid, wd_ref[...])

    @pl.when(j == n_ff - 1)
    def _():
        y = alpha * x_ref[...] + 0.5 * acc_ref[...]
        mu = jnp.mean(y, axis=-1, keepdims=True)
        d = y - mu
        var = jnp.mean(d * d, axis=-1, keepdims=True)
        o_ref[...] = d * lax.rsqrt(var + LN_EPS) * g_ref[...] + b_ref[...]


def _ffn_ln(x, wg, wu, wd, g, b, alpha, tm):
    n, d = x.shape
    dff = wg.shape[1]
    tf = dff // 2
    n_ff = dff // tf
    return pl.pallas_call(
        functools.partial(_ffn_ln_kernel, alpha=alpha, n_ff=n_ff),
        grid=(n // tm, n_ff),
        in_specs=[
            pl.BlockSpec((tm, d), lambda i, j: (i, 0)),
            pl.BlockSpec((d, tf), lambda i, j: (0, j)),
            pl.BlockSpec((d, tf), lambda i, j: (0, j)),
            pl.BlockSpec((tf, d), lambda i, j: (j, 0)),
            pl.BlockSpec((1, d), lambda i, j: (0, 0)),
            pl.BlockSpec((1, d), lambda i, j: (0, 0)),
        ],
        out_specs=pl.BlockSpec((tm, d), lambda i, j: (i, 0)),
        out_shape=jax.ShapeDtypeStruct((n, d), F32),
        scratch_shapes=[pltpu.VMEM((tm, d), F32)],
        compiler_params=_cparams(("parallel", "arbitrary")),
        name="ffn_ln",
    )(x, wg, wu, wd, g, b)


C_GM = 0
C_FB = 2048
C_TOT = 2176
R_QA = 0
R_KVA = 512
R_FQ = 1280
R_FKV = 1792
R_GA = 2816
R_FB = 2848
R_TOT = 2864
N_KH = 12
VT_ROWS = 80


def _head_pad(z, h):
    col = z[:, LANES * (h // 2):LANES * (h // 2 + 1)]
    if h % 2:
        col = pltpu.roll(col, HEAD_DIM, axis=1)
    lane = lax.broadcasted_iota(jnp.int32, col.shape, 1)
    return jnp.where(lane < HEAD_DIM, col, 0.0)


def _ones_row_block(width):
    sub = lax.broadcasted_iota(jnp.int32, (VT_ROWS - HEAD_DIM, width), 0)
    return jnp.where(sub == 0, 1.0, 0.0)


def _in_proj_kernel(h_ref, w_ref, b_ref, wt_ref, bt_ref,
                    cmp_ref, logf_ref, kh_ref, gm_ref,
                    nsat_ref, wint_ref, foxt_ref, logft_ref,
                    qat_ref, nvt_ref, fqt_ref, fvt_ref, gat_ref):
    hb = h_ref[...].astype(BF16)
    tm = hb.shape[0]

    def proj(c0, c1):
        return _dot(hb, w_ref[:, c0:c1]) + b_ref[:, c0:c1]

    def proj_t(r0, r1):
        return _dot_nt(wt_ref[r0:r1, :], hb) + bt_ref[r0:r1, :]

    gm_ref[...] = _sigmoid(proj(C_GM, C_FB))
    logf_ref[...] = _log_sigmoid(proj(C_FB, C_TOT))[:, :FOX_HEADS]

    ones_blk = _ones_row_block(tm)

    def value_head(zt, r0):
        return jnp.concatenate([zt[r0:r0 + HEAD_DIM], ones_blk], axis=0).astype(BF16)

    qat_ref[...] = (proj_t(R_QA, R_KVA) * SCALE).astype(BF16).reshape(qat_ref.shape)

    zkv = proj_t(R_KVA, R_FQ)
    nsat_ref[...] = zkv[:512]
    wint_ref[...] = zkv[512:]
    for j, head in enumerate((6, 7, 10, 11)):
        nvt_ref[j] = value_head(zkv, head * HEAD_DIM)

    fqt_ref[...] = (proj_t(R_FQ, R_FKV) * SCALE).astype(BF16).reshape(fqt_ref.shape)
    zf = proj_t(R_FKV, R_GA)
    foxt_ref[...] = zf
    for h in range(FOX_HEADS):
        fvt_ref[h] = value_head(zf, (FOX_HEADS + h) * HEAD_DIM)

    cmp_ref[...] = zkv[:4 * HEAD_DIM].T
    keys = jnp.concatenate([zkv[4 * HEAD_DIM:6 * HEAD_DIM], zkv[8 * HEAD_DIM:10 * HEAD_DIM],
                            zf[:FOX_HEADS * HEAD_DIM]], axis=0).T
    for h in range(N_KH):
        kh_ref[h] = _head_pad(keys, h).astype(BF16)

    gat_ref[...] = _sigmoid(proj_t(R_GA, R_FB))
    logft_ref[...] = _log_sigmoid(proj_t(R_FB, R_TOT))[:FOX_HEADS]


def _in_proj(h, w, b, wt, bt, nseq, tseq, tm):
    n, d = h.shape
    tps = tseq // tm
    row = lambda i: (i, 0)
    headmaj = lambda i: (i // tps, 0, i % tps, 0)
    featmaj = lambda i: (i // tps, 0, 0, i % tps)
    feat3 = lambda i: (i // tps, 0, i % tps)
    n_ga = R_FB - R_GA
    out_shape = [
        jax.ShapeDtypeStruct((n, 256), F32),
        jax.ShapeDtypeStruct((n, FOX_HEADS), F32),
        jax.ShapeDtypeStruct((nseq, N_KH, tseq, LANES), BF16),
        jax.ShapeDtypeStruct((n, 2 * D_MODEL), F32),
        jax.ShapeDtypeStruct((nseq, 512, tseq), F32),
        jax.ShapeDtypeStruct((nseq, 256, tseq), F32),
        jax.ShapeDtypeStruct((nseq, 1024, tseq), F32),
        jax.ShapeDtypeStruct((nseq, FOX_HEADS, tseq), F32),
        jax.ShapeDtypeStruct((nseq, NSA_HEADS, HEAD_DIM, tseq), BF16),
        jax.ShapeDtypeStruct((nseq, 4, VT_ROWS, tseq), BF16),
        jax.ShapeDtypeStruct((nseq, FOX_HEADS, HEAD_DIM, tseq), BF16),
        jax.ShapeDtypeStruct((nseq, FOX_HEADS, VT_ROWS, tseq), BF16),
        jax.ShapeDtypeStruct((n_ga, n), F32),
    ]
    out_specs = [
        pl.BlockSpec((tm, 256), row),
        pl.BlockSpec((tm, FOX_HEADS), row),
        pl.BlockSpec((None, N_KH, tm, LANES), headmaj),
        pl.BlockSpec((tm, 2 * D_MODEL), row),
        pl.BlockSpec((None, 512, tm), feat3),
        pl.BlockSpec((None, 256, tm), feat3),
        pl.BlockSpec((None, 1024, tm), feat3),
        pl.BlockSpec((None, FOX_HEADS, tm), feat3),
        pl.BlockSpec((None, NSA_HEADS, HEAD_DIM, tm), featmaj),
        pl.BlockSpec((None, 4, VT_ROWS, tm), featmaj),
        pl.BlockSpec((None, FOX_HEADS, HEAD_DIM, tm), featmaj),
        pl.BlockSpec((None, FOX_HEADS, VT_ROWS, tm), featmaj),
        pl.BlockSpec((n_ga, tm), lambda i: (0, i)),
    ]
    return pl.pallas_call(
        _in_proj_kernel,
        grid=(n // tm,),
        in_specs=[
            pl.BlockSpec((tm, d), row),
            pl.BlockSpec((d, C_TOT), lambda i: (0, 0)),
            pl.BlockSpec((1, C_TOT), lambda i: (0, 0)),
            pl.BlockSpec((R_TOT, d), lambda i: (0, 0)),
            pl.BlockSpec((R_TOT, 1), lambda i: (0, 0)),
        ],
        out_specs=out_specs,
        out_shape=out_shape,
        compiler_params=_cparams(("parallel",)),
        name="in_proj",
    )(h, w, b, wt, bt)


def _cmp_parts(k_ref, v_ref, wr_ref, n_rows):
    n_chunk = n_rows // CMP_STRIDE

    def kind(ref, w):
        x = jnp.concatenate([ref[pl.ds(r, n_chunk, stride=CMP_STRIDE), :] for r in range(CMP_STRIDE)], axis=1)
        return _dot(x.astype(BF16), w)

    return jnp.concatenate([kind(k_ref, wr_ref[0]), kind(v_ref, wr_ref[1])], axis=1)


def _cmp_finish(parts, bias8, w2_ref, out_ref, vt_ref=None):
    n = parts.shape[0]
    for s in range(4):
        kind = s // 2
        pa = parts[:, 256 * s:256 * s + CMP_HIDDEN]
        pb = parts[:, 256 * s + CMP_HIDDEN:256 * (s + 1)]
        pb = pltpu.roll(pb, n - 1, axis=0)
        bias = bias8[kind:kind + 1, CMP_HIDDEN * kind:CMP_HIDDEN * (kind + 1)]
        hid = bias + pa + pb
        c = _dot(_gelu_tanh(hid).astype(BF16), w2_ref[kind])
        out_ref[s] = c.astype(BF16)
        if vt_ref is not None and kind == 1:
            vt = jnp.concatenate([c.T[:HEAD_DIM], _ones_row_block(n)], axis=0)
            vt_ref[s - 2] = vt.astype(BF16)


def _cmp_bias(pe_ref, w1_ref):
    return _dot(pe_ref[...], w1_ref[...])


def _compress_prompt_kernel(k_ref, v_ref, wr_ref, pe_ref, w1_ref, w2_ref, out_ref, vt_ref, *, t):
    parts = _cmp_parts(k_ref, v_ref, wr_ref, t)
    _cmp_finish(parts, _cmp_bias(pe_ref, w1_ref), w2_ref, out_ref, vt_ref)


def _compress_prompt(nsa_rows, cw, nb, t):
    n_chunk = t // CMP_STRIDE
    return pl.pallas_call(
        functools.partial(_compress_prompt_kernel, t=t),
        grid=(nb,),
        in_specs=[
            pl.BlockSpec((t, LANES), lambda b: (b, 0)),
            pl.BlockSpec((t, LANES), lambda b: (b, 1)),
            pl.BlockSpec((2, CMP_STRIDE * LANES, 512), lambda b: (0, 0, 0)),
            pl.BlockSpec((16, 2048), lambda b: (0, 0)),
            pl.BlockSpec((2048, 256), lambda b: (0, 0)),
            pl.BlockSpec((2, CMP_HIDDEN, LANES), lambda b: (0, 0, 0)),
        ],
        out_specs=[
            pl.BlockSpec((None, 4, n_chunk, LANES), lambda b: (b, 0, 0, 0)),
            pl.BlockSpec((None, 2, VT_ROWS, n_chunk), lambda b: (b, 0, 0, 0)),
        ],
        out_shape=[
            jax.ShapeDtypeStruct((nb, 4, n_chunk, LANES), BF16),
            jax.ShapeDtypeStruct((nb, 2, VT_ROWS, n_chunk), BF16),
        ],
        compiler_params=_cparams(("parallel",)),
        name="compress_prompt",
    )(nsa_rows, nsa_rows, cw["wr"], cw["pe8"], cw["w1cat"], cw["w2pad"])


def _compress_sample_kernel(pt_ref, *refs, n_pages, n_steps):
    k_refs = refs[:n_pages]
    v_refs = refs[n_pages:2 * n_pages]
    wr_ref, pe_ref, w1_ref, w2_ref, out_ref, parts_ref, krows_ref, vrows_ref = refs[2 * n_pages:]
    pg = pl.program_id(1)
    rows = n_pages * (PAGE // CMP_STRIDE)
    for j in range(n_pages):
        krows_ref[j * PAGE:(j + 1) * PAGE, :] = k_refs[j][...].T
        vrows_ref[j * PAGE:(j + 1) * PAGE, :] = v_refs[j][...].T
    parts = _cmp_parts(krows_ref, vrows_ref, wr_ref, n_pages * PAGE)
    parts_ref[pl.ds(pl.multiple_of(pg * rows, rows), rows), :] = parts

    @pl.when(pg == n_steps - 1)
    def _():
        _cmp_finish(parts_ref[...], _cmp_bias(pe_ref, w1_ref), w2_ref, out_ref)


def _compress_sample(pool_t, page_table, cw, n_pages):
    nb, pages_per_seq = page_table.shape
    n_steps = pages_per_seq // n_pages
    n_chunk = pages_per_seq * (PAGE // CMP_STRIDE)
    page_specs = [
        pl.BlockSpec((None, None, LANES, PAGE), lambda b, pg, pt, j=j, c=c: (pt[b, pg * n_pages + j], c, 0, 0))
        for c in range(2) for j in range(n_pages)
    ]
    grid_spec = pltpu.PrefetchScalarGridSpec(
        num_scalar_prefetch=1,
        grid=(nb, n_steps),
        in_specs=page_specs + [
            pl.BlockSpec((2, CMP_STRIDE * LANES, 512), lambda b, pg, pt: (0, 0, 0)),
            pl.BlockSpec((16, 2048), lambda b, pg, pt: (0, 0)),
            pl.BlockSpec((2048, 256), lambda b, pg, pt: (0, 0)),
            pl.BlockSpec((2, CMP_HIDDEN, LANES), lambda b, pg, pt: (0, 0, 0)),
        ],
        out_specs=pl.BlockSpec((None, 4, n_chunk, LANES), lambda b, pg, pt: (b, 0, 0, 0)),
        scratch_shapes=[pltpu.VMEM((n_chunk, 1024), F32),
                        pltpu.VMEM((n_pages * PAGE, LANES), F32), pltpu.VMEM((n_pages * PAGE, LANES), F32)],
    )
    return pl.pallas_call(
        functools.partial(_compress_sample_kernel, n_pages=n_pages, n_steps=n_steps),
        grid_spec=grid_spec,
        out_shape=jax.ShapeDtypeStruct((nb, 4, n_chunk, LANES), BF16),
        compiler_params=_cparams(("parallel", "arbitrary")),
        name="compress_sample",
    )(page_table, *([pool_t] * (2 * n_pages)), cw["wr"], cw["pe8"], cw["w1cat"], cw["w2pad"])


def _softmax_update(s, v, m_ref, l_ref, acc_ref):
    m_prev = m_ref[...]
    m_new = jnp.maximum(m_prev, jnp.max(s, axis=1, keepdims=True))
    a = jnp.exp(m_prev - m_new)
    p = jnp.exp(s - m_new)
    l_ref[...] = a * l_ref[...] + jnp.sum(p, axis=1, keepdims=True)
    acc_ref[...] = a * acc_ref[...] + _dot(p.astype(BF16), v)
    m_ref[...] = m_new


def _softmax_update_t(s, vt, m_ref, acc_ref):
    m_prev = m_ref[...]
    m_new = jnp.maximum(m_prev, jnp.max(s, axis=0, keepdims=True))
    a = jnp.exp(m_prev - m_new)
    p = jnp.exp(s - m_new).astype(BF16)
    acc_ref[...] = a * acc_ref[...] + _dot(vt, p)
    m_ref[...] = m_new


def _softmax_reset(m_ref, acc_ref):
    m_ref[...] = jnp.full_like(m_ref, NEG)
    acc_ref[...] = jnp.zeros_like(acc_ref)


def _softmax_result(acc_ref):
    acc = acc_ref[...]
    return acc[:HEAD_DIM] / acc[HEAD_DIM:HEAD_DIM + 1]


def _nsa_prompt_kernel(q_ref, selk_ref, selvt_ref, wink_ref, winvt_ref, kc_ref, vct_ref, gat_ref,
                       kaug_sel_ref, kaug_win_ref, kaug_cmp_ref, ovlt_ref,
                       o_ref, ksel_s, kwin_s, m_ref, acc_ref, *, tq, tk, n_cmp, n_slc):
    g = pl.program_id(1)
    qt = pl.program_id(2)
    cols = NSA_GROUP * tq
    q0 = qt * tq

    @pl.when(qt == 0)
    def _():
        ksel_s[...] = selk_ref[...] + kaug_sel_ref[...]
        kwin_s[...] = wink_ref[...] + kaug_win_ref[...]

    qt4 = jnp.concatenate([q_ref[r] for r in range(NSA_GROUP)], axis=1)

    def head_and_pos(shape):
        lane = lax.broadcasted_iota(jnp.int32, shape, 1)
        return lane >> (tq.bit_length() - 1), q0 + (lane & (tq - 1))

    sub = lax.broadcasted_iota(jnp.int32, (16, cols), 0)
    r, t = head_and_pos((16, cols))
    sl0 = jnp.where(r == 0, 0.5, jnp.where(r == 1, 0.25, jnp.where(r == 2, 0.125, 0.0625))).astype(F32)
    slope = jnp.where(g == 0, sl0, sl0 * 0.0625)
    t_hi = ((t >> 7) << 7).astype(F32)
    t_lo = (t & 127).astype(F32)
    al = jnp.where(sub == 0, -slope * t_hi,
                   jnp.where(sub == 1, -slope * t_lo,
                             jnp.where((sub == 2) | (sub == 3), slope, 0.0))).astype(BF16)
    zeros16 = jnp.zeros((16, cols), BF16)

    def query_cols(selbits):
        return jnp.concatenate([qt4, selbits, al, zeros16], axis=0)

    qa = query_cols(jnp.zeros((n_slc, cols), BF16))

    n_chunk = kc_ref.shape[0]
    kc = kc_ref[...] + kaug_cmp_ref[...]
    s = _dot(kc, qa)
    nidx = lax.broadcasted_iota(jnp.int32, (n_chunk, cols), 0)
    _, t_c = head_and_pos((n_chunk, cols))
    valid = (t_c >= nidx * CMP_STRIDE + (CMP_BLOCK - 1)) & (nidx < n_cmp)
    s = jnp.where(valid, s, NEG)
    m = jnp.max(s, axis=0, keepdims=True)
    p = jnp.where(valid, jnp.exp(s - m), 0.0)
    den = jnp.sum(p, axis=0, keepdims=True)
    p = p / jnp.where(den > 0.0, den, 1.0)
    o_cmp = _dot(vct_ref[...], p.astype(BF16))[:HEAD_DIM]

    imp = p[:, 0:tq] + p[:, tq:2 * tq] + p[:, 2 * tq:3 * tq] + p[:, 3 * tq:4 * tq]
    hi, mid, lo = _split3(imp)
    ovlt = ovlt_ref[...]
    imp_slc = _dot(ovlt, hi) + _dot(ovlt, mid) + _dot(ovlt, lo)
    blk = lax.broadcasted_iota(jnp.int32, (n_slc, tq), 0)
    t1 = q0 + lax.broadcasted_iota(jnp.int32, (n_slc, tq), 1)
    cur = t1 >> 6
    forced = (blk == 0) | (blk == cur) | (blk == cur - 1)
    score = jnp.where(forced, FORCE, jnp.where(blk * SEL_BLOCK <= t1, imp_slc, -FORCE))
    rank = jnp.zeros((n_slc, tq), F32)
    for i in range(n_slc):
        row = score[i:i + 1, :]
        beats = (row > score) | ((row == score) & (blk > i))
        rank = rank + jnp.where(beats, 1.0, 0.0)
    notsel = jnp.where(rank >= float(N_SEL), 1.0, 0.0).astype(BF16)
    qs = query_cols(jnp.concatenate([notsel] * NSA_GROUP, axis=1))

    kpos_iota = lax.broadcasted_iota(jnp.int32, (tk, cols), 0)
    _, t_row = head_and_pos((1, cols))
    n_full = q0 // tk

    def branch(qx, k_s, vt_ref, kt_lo, window):
        _softmax_reset(m_ref, acc_ref)

        def tile(k0, causal):
            sc = _dot(k_s[pl.ds(k0, tk), :], qx)
            rel = t_row - k0
            if causal:
                sc = jnp.where(kpos_iota <= rel, sc, NEG)
            elif window:
                sc = jnp.where(kpos_iota > rel - WINDOW, sc, NEG)
            _softmax_update_t(sc, vt_ref[:, pl.ds(k0, tk)], m_ref, acc_ref)

        def body(kt, carry):
            tile(pl.multiple_of(kt * tk, tk), False)
            return carry

        lax.fori_loop(kt_lo, n_full, body, 0)
        for j in range(tq // tk):
            tile(pl.multiple_of(q0 + j * tk, tk), True)
        return _softmax_result(acc_ref)

    o_sel = branch(qs, ksel_s, selvt_ref, 0, False)
    o_win = branch(qa, kwin_s, winvt_ref, jnp.maximum(q0 - WINDOW, 0) // tk, True)

    outs = []
    for rr in range(NSA_GROUP):
        def gate(br):
            return gat_ref[pl.ds(br * NSA_HEADS + g * NSA_GROUP + rr, 1), :]
        sl = slice(rr * tq, (rr + 1) * tq)
        outs.append(gate(0) * o_cmp[:, sl] + gate(1) * o_sel[:, sl] + gate(2) * o_win[:, sl])
    o_ref[...] = jnp.concatenate(outs, axis=0).T.astype(BF16)


def _nsa_prompt(qat, kh, nvt, kcvc, vct, gat, consts, nb, t, tq, tk):
    n_cmp = (t - CMP_BLOCK) // CMP_STRIDE + 1
    n_slc = -(-t // SEL_BLOCK)
    n_chunk = t // CMP_STRIDE
    assert n_slc == 32 and tq % tk == 0 and tq <= WINDOW
    q5 = qat.reshape(nb, NSA_KV_HEADS, NSA_GROUP, HEAD_DIM, t)
    k_spec = lambda off: pl.BlockSpec((None, None, t, LANES), lambda b, g, qt: (b, off + g, 0, 0))
    vt_spec = lambda off: pl.BlockSpec((None, None, VT_ROWS, t), lambda b, g, qt: (b, off + g, 0, 0))
    const2 = lambda shape: pl.BlockSpec(shape, lambda b, g, qt: (0, 0))
    cols = NSA_GROUP * tq
    return pl.pallas_call(
        functools.partial(_nsa_prompt_kernel, tq=tq, tk=tk, n_cmp=n_cmp, n_slc=n_slc),
        grid=(nb, NSA_KV_HEADS, t // tq),
        in_specs=[
            pl.BlockSpec((None, None, NSA_GROUP, HEAD_DIM, tq), lambda b, g, qt: (b, g, 0, 0, qt)),
            k_spec(0), vt_spec(0), k_spec(2), vt_spec(2),
            pl.BlockSpec((None, None, n_chunk, LANES), lambda b, g, qt: (b, g, 0, 0)),
            pl.BlockSpec((None, None, VT_ROWS, n_chunk), lambda b, g, qt: (b, g, 0, 0)),
            pl.BlockSpec((R_FB - R_GA, tq), lambda b, g, qt: (0, b * (t // tq) + qt)),
            const2((t, LANES)), const2((t, LANES)), const2((n_chunk, LANES)), const2((n_slc, n_chunk)),
        ],
        out_specs=pl.BlockSpec((None, tq, NSA_GROUP * HEAD_DIM), lambda b, g, qt: (b, qt, g)),
        out_shape=jax.ShapeDtypeStruct((nb, t, NSA_HEADS * HEAD_DIM), BF16),
        scratch_shapes=[
            pltpu.VMEM((t, LANES), BF16), pltpu.VMEM((t, LANES), BF16),
            pltpu.VMEM((1, cols), F32), pltpu.VMEM((VT_ROWS, cols), F32),
        ],
        compiler_params=_cparams(("parallel", "parallel", "arbitrary")),
        name="nsa_prompt",
    )(q5, kh, nvt, kh, nvt, kcvc, vct, gat,
      consts["kaug_sel"], consts["kaug_win"], consts["kaug_cmp"], consts["ovlt"])


FOX_BIAS0 = 64
FOX_PAIR = 2


def _fox_prompt_kernel(q_ref, k_ref, vt_ref, lf_ref, tril_ref, o_ref, c_s, kaug_s, m_ref, acc_ref, *, tq, tk):
    hp = pl.program_id(1)
    qt = pl.program_id(2)
    q0 = qt * tq
    t = k_ref.shape[1]

    @pl.when((hp == 0) & (qt == 0))
    def _():
        tb = tril_ref.shape[0]
        tril = tril_ref[...]
        carry = jnp.zeros((1, FOX_HEADS), F32)
        cs = []
        for i in range(t // tb):
            hi, mid, lo = _split3(lf_ref[i * tb:(i + 1) * tb, :])
            c = _dot(tril, hi) + _dot(tril, mid) + _dot(tril, lo) + carry
            cs.append(c)
            carry = c[tb - 1:tb, :]
        for j, piece in enumerate(_split3(-jnp.concatenate(cs, axis=0))):
            c_s[j] = piece

    @pl.when(qt == 0)
    def _():
        hi, mid, lo = c_s[0], c_s[1], c_s[2]
        hrow = lax.broadcasted_iota(jnp.int32, (FOX_HEADS, LANES), 0)
        lane = lax.broadcasted_iota(jnp.int32, (FOX_HEADS, LANES), 1)
        for i in range(FOX_PAIR):
            head = FOX_PAIR * hp + i

            def place(j):
                return jnp.where((hrow == head) & (lane == FOX_BIAS0 + j), 1.0, 0.0).astype(BF16)

            aug = _dot(hi, place(0)) + _dot(mid, place(1)) + _dot(lo, place(2))
            kaug_s[i] = k_ref[i] + aug.astype(BF16)

    sub = lax.broadcasted_iota(jnp.int32, (16, tq), 0)
    ones3 = jnp.where(sub < 3, 1.0, 0.0).astype(BF16)
    zeros = jnp.zeros((LANES - HEAD_DIM - 16, tq), BF16)
    qx = [jnp.concatenate([q_ref[i], ones3, zeros], axis=0) for i in range(FOX_PAIR)]

    for i in range(FOX_PAIR):
        _softmax_reset(m_ref.at[i], acc_ref.at[i])

    def tile(k0, masked, c0=0):
        for i in range(FOX_PAIR):
            sc = _dot(kaug_s[i, pl.ds(k0, tk), :], qx[i][:, c0:])
            if masked:
                kpos = lax.broadcasted_iota(jnp.int32, (tk, tq - c0), 0)
                rel = (q0 + c0 - k0) + lax.broadcasted_iota(jnp.int32, (1, tq - c0), 1)
                sc = jnp.where(kpos <= rel, sc, NEG)
            _softmax_update_t(sc, vt_ref[i, :, pl.ds(k0, tk)],
                              m_ref.at[i, :, pl.ds(c0, tq - c0)], acc_ref.at[i, :, pl.ds(c0, tq - c0)])

    def body(kt, carry):
        tile(pl.multiple_of(kt * tk, tk), False)
        return carry

    lax.fori_loop(0, q0 // tk, body, 0)
    for j in range(tq // tk):
        tile(pl.multiple_of(q0 + j * tk, tk), True, c0=j * tk)
    o_t = jnp.concatenate([_softmax_result(acc_ref.at[i]) for i in range(FOX_PAIR)], axis=0)
    o_ref[...] = o_t.T.astype(BF16)


def _fox_prompt(fqt, kh, fvt, logf, tril, nb, t, tq, tk):
    n_pair = FOX_HEADS // FOX_PAIR
    q5 = fqt.reshape(nb, n_pair, FOX_PAIR, HEAD_DIM, t)
    k5 = kh.reshape(nb, N_KH // FOX_PAIR, FOX_PAIR, t, LANES)
    fox_pair0 = (N_KH - FOX_HEADS) // FOX_PAIR
    v5 = fvt.reshape(nb, n_pair, FOX_PAIR, VT_ROWS, t)
    return pl.pallas_call(
        functools.partial(_fox_prompt_kernel, tq=tq, tk=tk),
        grid=(nb, n_pair, t // tq),
        in_specs=[
            pl.BlockSpec((None, None, FOX_PAIR, HEAD_DIM, tq), lambda b, hp, qt: (b, hp, 0, 0, qt)),
            pl.BlockSpec((None, None, FOX_PAIR, t, LANES), lambda b, hp, qt: (b, fox_pair0 + hp, 0, 0, 0)),
            pl.BlockSpec((None, None, FOX_PAIR, VT_ROWS, t), lambda b, hp, qt: (b, hp, 0, 0, 0)),
            pl.BlockSpec((t, FOX_HEADS), lambda b, hp, qt: (b, 0)),
            pl.BlockSpec(tril.shape, lambda b, hp, qt: (0, 0)),
        ],
        out_specs=pl.BlockSpec((None, tq, FOX_PAIR * HEAD_DIM), lambda b, hp, qt: (b, qt, hp)),
        out_shape=jax.ShapeDtypeStruct((nb, t, FOX_HEADS * HEAD_DIM), BF16),
        scratch_shapes=[
            pltpu.VMEM((3, t, FOX_HEADS), BF16),
            pltpu.VMEM((FOX_PAIR, t, LANES), BF16),
            pltpu.VMEM((FOX_PAIR, 1, tq), F32), pltpu.VMEM((FOX_PAIR, VT_ROWS, tq), F32),
        ],
        compiler_params=_cparams(("parallel", "arbitrary", "arbitrary")),
        name="fox_prompt",
    )(q5, k5, v5, logf, tril)


def _merge_ln_kernel(oa_ref, ob_ref, gm_ref, h_ref, wa_ref, wb_ref, wo_ref, g_ref, b_ref, o_ref, *, alpha):
    pa = _dot(oa_ref[...], wa_ref[...])
    pb = _dot(ob_ref[...], wb_ref[...])
    merged = gm_ref[:, :D_MODEL] * pa + gm_ref[:, D_MODEL:] * pb
    y = alpha * h_ref[...] + _dot(merged.astype(BF16), wo_ref[...])
    mu = jnp.mean(y, axis=-1, keepdims=True)
    d = y - mu
    var = jnp.mean(d * d, axis=-1, keepdims=True)
    o_ref[...] = d * lax.rsqrt(var + LN_EPS) * g_ref[...] + b_ref[...]


def _merge_ln(oa, ob, gm, h, wa, wb, wo, g, b, alpha, tm):
    n, d = h.shape
    row = lambda i: (i, 0)
    fixed = lambda i: (0, 0)
    return pl.pallas_call(
        functools.partial(_merge_ln_kernel, alpha=alpha),
        grid=(n // tm,),
        in_specs=[
            pl.BlockSpec((tm, 512), row), pl.BlockSpec((tm, 512), row),
            pl.BlockSpec((tm, 2 * d), row), pl.BlockSpec((tm, d), row),
            pl.BlockSpec((512, d), fixed), pl.BlockSpec((512, d), fixed), pl.BlockSpec((d, d), fixed),
            pl.BlockSpec((1, d), fixed), pl.BlockSpec((1, d), fixed),
        ],
        out_specs=pl.BlockSpec((tm, d), row),
        out_shape=jax.ShapeDtypeStruct((n, d), F32),
        compiler_params=_cparams(("parallel",)),
        name="merge_ln",
    )(oa, ob, gm, h, wa, wb, wo, g, b)


SAMPLE_ROWS = 32


def _row_slopes(rows, width, head0=0):
    row = lax.broadcasted_iota(jnp.int32, (rows, width), 0)
    head = head0 + (row >> 2)
    slope = lax.bitcast_convert_type((126 - head) << 23, F32)
    return slope, row & 3


def _sample_select_kernel(q_ref, kc_ref, vc_ref, ovl_ref, ocmp_ref, sel_ref, *, q_off, n_cmp, n_slc, t_new):
    n_key = kc_ref.shape[1]
    hrows = SAMPLE_ROWS // NSA_KV_HEADS
    lane = lax.broadcasted_iota(jnp.int32, (hrows, n_key), 1)
    ps, os_ = [], []
    for g in range(NSA_KV_HEADS):
        slope, tok = _row_slopes(hrows, n_key, head0=g * NSA_GROUP)
        dist = (q_off + tok) - (lane * CMP_STRIDE + (CMP_BLOCK - 1))
        valid = (dist >= 0) & (lane < n_cmp)
        s = _dot_nt(q_ref[g * hrows:(g + 1) * hrows], kc_ref[g]) - slope * dist.astype(F32)
        s = jnp.where(valid, s, NEG)
        m = jnp.max(s, axis=1, keepdims=True)
        p = jnp.where(valid, jnp.exp(s - m), 0.0)
        den = jnp.sum(p, axis=1, keepdims=True)
        p = p / jnp.where(den > 0.0, den, 1.0)
        ps.append(p)
        os_.append(_dot(p.astype(BF16), vc_ref[g]))
    ocmp_ref[...] = jnp.concatenate(os_, axis=0)

    p_all = jnp.concatenate(ps, axis=0)
    rr = lax.broadcasted_iota(jnp.int32, (2 * 8, SAMPLE_ROWS), 0)
    cc = lax.broadcasted_iota(jnp.int32, (2 * 8, SAMPLE_ROWS), 1)
    gather = jnp.where(((rr >> 3) == (cc >> 4)) & ((rr & 7) == (cc & 3)), 1.0, 0.0).astype(BF16)
    hi, mid, lo = _split3(p_all)
    imp = _dot(gather, hi) + _dot(gather, mid) + _dot(gather, lo)
    imp_slc = _dot3(imp, ovl_ref[...])
    nb_pad = imp_slc.shape[1]
    blk = lax.broadcasted_iota(jnp.int32, (16, nb_pad), 1)
    blk_f = blk.astype(F32)
    tpos = q_off + (lax.broadcasted_iota(jnp.int32, (16, nb_pad), 0) & 7)
    cur = tpos >> 6
    forced = (blk == 0) | (blk == cur) | (blk == cur - 1)
    score = jnp.where(forced, FORCE, jnp.where(blk * SEL_BLOCK <= tpos, imp_slc, -FORCE))
    score = jnp.where(blk < n_slc, score, -3.0 * FORCE)
    sel = jnp.zeros((16, nb_pad), F32)
    for _ in range(min(N_SEL, n_slc)):
        mx = jnp.max(score, axis=1, keepdims=True)
        first = jnp.min(jnp.where(score == mx, blk_f, float(nb_pad)), axis=1, keepdims=True)
        hit = blk_f == first
        sel = jnp.where(hit, 1.0, sel)
        score = jnp.where(hit, -4.0 * FORCE, score)
    sel_ref[...] = sel


def _sample_select(q32, kcvc, ovl, q_off, n_cmp, n_slc, t_new):
    nb = q32.shape[0]
    n_key = kcvc.shape[2]
    nb_pad = ovl.shape[1]
    return pl.pallas_call(
        functools.partial(_sample_select_kernel, q_off=q_off, n_cmp=n_cmp, n_slc=n_slc, t_new=t_new),
        grid=(nb,),
        in_specs=[
            pl.BlockSpec((None, SAMPLE_ROWS, LANES), lambda b: (b, 0, 0)),
            pl.BlockSpec((None, None, 2, n_key, LANES), lambda b: (b, 0, 0, 0, 0)),
            pl.BlockSpec((None, None, 2, n_key, LANES), lambda b: (b, 1, 0, 0, 0)),
            pl.BlockSpec((n_key, nb_pad), lambda b: (0, 0)),
        ],
        out_specs=[
            pl.BlockSpec((None, SAMPLE_ROWS, LANES), lambda b: (b, 0, 0)),
            pl.BlockSpec((None, 16, nb_pad), lambda b: (b, 0, 0)),
        ],
        out_shape=[
            jax.ShapeDtypeStruct((nb, SAMPLE_ROWS, LANES), F32),
            jax.ShapeDtypeStruct((nb, 16, nb_pad), F32),
        ],
        compiler_params=_cparams(("parallel",)),
        name="sample_select",
    )(q32, kcvc.reshape(nb, 2, 2, n_key, LANES), kcvc.reshape(nb, 2, 2, n_key, LANES), ovl)


TAIL = 128


def _softmax_update_pages(s, vts, m_ref, l_ref, acc_ref):
    m_prev = m_ref[...]
    m_new = jnp.maximum(m_prev, jnp.max(s, axis=1, keepdims=True))
    a = jnp.exp(m_prev - m_new)
    p = jnp.exp(s - m_new)
    l_ref[...] = a * l_ref[...] + jnp.sum(p, axis=1, keepdims=True)
    pb = p.astype(BF16)
    pv = None
    for j, vt in enumerate(vts):
        d = _dot_nt(pb[:, j * PAGE:(j + 1) * PAGE], vt)
        pv = d if pv is None else pv + d
    acc_ref[...] = a * acc_ref[...] + pv
    m_ref[...] = m_new


def _tail_scores(q, knew_t, t_new):
    s = _dot(q, knew_t)
    slope, tok = _row_slopes(SAMPLE_ROWS, TAIL)
    j = lax.broadcasted_iota(jnp.int32, s.shape, 1)
    d = tok - j
    s = s - slope * d.astype(F32)
    return jnp.where((d >= 0) & (j < t_new), s, NEG)


def _sample_nsa_kernel(pt_ref, *refs, n_pages, n_steps, q_off, t_new):
    k_refs = refs[:n_pages]
    v_refs = refs[n_pages:2 * n_pages]
    (q_ref, selx_ref, exp_ref, knew_ref, vnew_ref, wk_ref, wv_ref, wknew_ref, wvnew_ref, ocmp_ref, gate_ref,
     o_ref, m_ref, l_ref, acc_ref) = refs[2 * n_pages:]
    pg = pl.program_id(1)
    n_key = n_pages * PAGE
    q = q_ref[...]
    slope, tok = _row_slopes(SAMPLE_ROWS, n_key)

    @pl.when(pg == 0)
    def _():
        m_ref[...] = jnp.full_like(m_ref, NEG)
        l_ref[...] = jnp.zeros_like(l_ref)
        acc_ref[...] = jnp.zeros_like(acc_ref)

    s = jnp.concatenate([_dot(q, r[...].astype(BF16)) for r in k_refs], axis=1)
    kpos = pg * n_key + lax.broadcasted_iota(jnp.int32, (SAMPLE_ROWS, n_key), 1)
    dist = (q_off + tok) - kpos
    s = s - slope * dist.astype(F32)
    chosen = _dot(selx_ref[...], exp_ref[...]) > 0.5
    s = jnp.where(chosen & (dist >= 0), s, NEG)
    _softmax_update_pages(s, [r[...].astype(BF16) for r in v_refs], m_ref, l_ref, acc_ref)

    @pl.when(pg == n_steps - 1)
    def _():
        lane = lax.broadcasted_iota(jnp.int32, (SAMPLE_ROWS, LANES), 1)
        row = lax.broadcasted_iota(jnp.int32, (SAMPLE_ROWS, LANES), 0)

        def own_group(x):
            return jnp.where(lane < HEAD_DIM, jnp.where(row < 16, x, pltpu.roll(x, HEAD_DIM, axis=1)), 0.0)

        st = _tail_scores(q, knew_ref[...], t_new)
        _softmax_update(st, vnew_ref[...], m_ref, l_ref, acc_ref)
        o_sel = own_group(acc_ref[...] / l_ref[...])

        sw = _dot(q, wk_ref[...].astype(BF16))
        wslope, wtok = _row_slopes(SAMPLE_ROWS, WINDOW)
        wpos = (q_off - WINDOW) + lax.broadcasted_iota(jnp.int32, (SAMPLE_ROWS, WINDOW), 1)
        wd = (q_off + wtok) - wpos
        sw = sw - wslope * wd.astype(F32)
        sw = jnp.where((wd >= 0) & (wd < WINDOW) & (wpos >= 0), sw, NEG)
        swt = _tail_scores(q, wknew_ref[...], t_new)
        mw = jnp.maximum(jnp.max(sw, axis=1, keepdims=True), jnp.max(swt, axis=1, keepdims=True))
        pw = jnp.exp(sw - mw)
        pwt = jnp.exp(swt - mw)
        lw = jnp.sum(pw, axis=1, keepdims=True) + jnp.sum(pwt, axis=1, keepdims=True)
        ow = _dot_nt(pw.astype(BF16), wv_ref[...].astype(BF16)) + _dot(pwt.astype(BF16), wvnew_ref[...])
        o_win = own_group(ow / lw)

        gate = gate_ref[...]
        o_ref[...] = gate[:, 0:1] * ocmp_ref[...] + gate[:, 1:2] * o_sel + gate[:, 2:3] * o_win


def _sample_nsa(pool_t, page_table, q32, selx, expand, knew_t, vnew, wbuf_t, wknew_t, wvnew, ocmp, gate32,
                n_pages, q_off, t_new):
    nb, pages_per_seq = page_table.shape
    n_steps = pages_per_seq // n_pages
    page_specs = [
        pl.BlockSpec((None, None, LANES, PAGE), lambda b, pg, pt, j=j, c=c: (pt[b, pg * n_pages + j], c, 0, 0))
        for c in (2, 3) for j in range(n_pages)
    ]
    per_b = lambda shape: pl.BlockSpec((None,) + shape, lambda b, pg, pt: (b,) + (0,) * len(shape))
    win_spec = lambda c: pl.BlockSpec((None, None, LANES, WINDOW), lambda b, pg, pt: (b, c, 0, 0))
    grid_spec = pltpu.PrefetchScalarGridSpec(
        num_scalar_prefetch=1,
        grid=(nb, n_steps),
        in_specs=page_specs + [
            per_b((SAMPLE_ROWS, LANES)),
            pl.BlockSpec((None, None, SAMPLE_ROWS, LANES), lambda b, pg, pt: (b, pg, 0, 0)),
            pl.BlockSpec((LANES, n_pages * PAGE), lambda b, pg, pt: (0, 0)),
            per_b((LANES, TAIL)), per_b((TAIL, LANES)),
            win_spec(0), win_spec(1),
            per_b((LANES, TAIL)), per_b((TAIL, LANES)),
            per_b((SAMPLE_ROWS, LANES)), per_b((SAMPLE_ROWS, LANES)),
        ],
        out_specs=per_b((SAMPLE_ROWS, LANES)),
        scratch_shapes=[
            pltpu.VMEM((SAMPLE_ROWS, 1), F32), pltpu.VMEM((SAMPLE_ROWS, 1), F32),
            pltpu.VMEM((SAMPLE_ROWS, LANES), F32),
        ],
    )
    return pl.pallas_call(
        functools.partial(_sample_nsa_kernel, n_pages=n_pages, n_steps=n_steps, q_off=q_off, t_new=t_new),
        grid_spec=grid_spec,
        out_shape=jax.ShapeDtypeStruct((nb, SAMPLE_ROWS, LANES), F32),
        compiler_params=_cparams(("parallel", "arbitrary")),
        name="sample_nsa",
    )(page_table, *([pool_t] * (2 * n_pages)), q32, selx, expand, knew_t, vnew, wbuf_t, wbuf_t, wknew_t, wvnew,
      ocmp, gate32)


def _sample_fox_kernel(pt_ref, *refs, n_pages, n_steps, t_new):
    k_refs = refs[:n_pages]
    v_refs = refs[n_pages:2 * n_pages]
    lf_refs = refs[2 * n_pages:3 * n_pages]
    (q_ref, triu_ref, knew_ref, vnew_ref, lfnew_ref,
     o_ref, m_ref, l_ref, acc_ref, carry_ref) = refs[3 * n_pages:]
    pg = pl.program_id(1)
    q = q_ref[...]

    @pl.when(pg == 0)
    def _():
        m_ref[...] = jnp.full_like(m_ref, NEG)
        l_ref[...] = jnp.zeros_like(l_ref)
        acc_ref[...] = jnp.zeros_like(acc_ref)
        carry_ref[...] = jnp.zeros_like(carry_ref)

    triu = triu_ref[...]

    def page_bias(lf, before):
        c = _dot3(lf, triu) + before
        return jnp.concatenate([c] * (SAMPLE_ROWS // FOX_HEADS), axis=0), before + jnp.sum(lf, axis=1, keepdims=True)

    carry = carry_ref[...]
    scores = []
    for j in range(n_pages):
        bias, carry = page_bias(lf_refs[j][...], carry)
        scores.append(_dot(q, k_refs[j][...].astype(BF16)) - bias)
    carry_ref[...] = carry
    _softmax_update_pages(jnp.concatenate(scores, axis=1), [r[...].astype(BF16) for r in v_refs],
                          m_ref, l_ref, acc_ref)

    @pl.when(pg == n_steps - 1)
    def _():
        bias_n, _ = page_bias(lfnew_ref[...], carry)
        sn = _dot(q, knew_ref[...]) - bias_n
        j = lax.broadcasted_iota(jnp.int32, sn.shape, 1)
        tok = lax.broadcasted_iota(jnp.int32, sn.shape, 0) >> 3
        sn = jnp.where((j <= tok) & (j < t_new), sn, NEG)
        _softmax_update(sn, vnew_ref[...], m_ref, l_ref, acc_ref)
        o_ref[...] = acc_ref[...] / l_ref[...]


def _sample_fox(pool_t, lf_t, page_table, qbd, triu, knew_t, vnew, lfnew_t, n_pages, t_new):
    nb, pages_per_seq = page_table.shape
    n_steps = pages_per_seq // n_pages
    width = FOX_HEADS * HEAD_DIM
    kv_specs = [
        pl.BlockSpec((None, None, width, PAGE), lambda b, pg, pt, j=j, c=c: (pt[b, pg * n_pages + j], c, 0, 0))
        for c in range(2) for j in range(n_pages)
    ]
    lf_specs = [
        pl.BlockSpec((None, FOX_HEADS, PAGE), lambda b, pg, pt, j=j: (pt[b, pg * n_pages + j], 0, 0))
        for j in range(n_pages)
    ]
    per_b = lambda shape: pl.BlockSpec((None,) + shape, lambda b, pg, pt: (b,) + (0,) * len(shape))
    grid_spec = pltpu.PrefetchScalarGridSpec(
        num_scalar_prefetch=1,
        grid=(nb, n_steps),
        in_specs=kv_specs + lf_specs + [
            per_b((SAMPLE_ROWS, width)),
            pl.BlockSpec((PAGE, PAGE), lambda b, pg, pt: (0, 0)),
            per_b((width, TAIL)), per_b((TAIL, width)), per_b((FOX_HEADS, TAIL)),
        ],
        out_specs=per_b((SAMPLE_ROWS, width)),
        scratch_shapes=[
            pltpu.VMEM((SAMPLE_ROWS, 1), F32), pltpu.VMEM((SAMPLE_ROWS, 1), F32),
            pltpu.VMEM((SAMPLE_ROWS, width), F32), pltpu.VMEM((FOX_HEADS, 1), F32),
        ],
    )
    return pl.pallas_call(
        functools.partial(_sample_fox_kernel, n_pages=n_pages, n_steps=n_steps, t_new=t_new),
        grid_spec=grid_spec,
        out_shape=jax.ShapeDtypeStruct((nb, SAMPLE_ROWS, width), F32),
        compiler_params=_cparams(("parallel", "arbitrary")),
        name="sample_fox",
    )(page_table, *([pool_t] * (2 * n_pages)), *([lf_t] * n_pages), qbd, triu, knew_t, vnew, lfnew_t)


def _alibi_key_cols(pos):
    tab = np.zeros((len(pos), LANES), np.float32)
    tab[:, ALIBI0] = 1.0
    tab[:, ALIBI0 + 1] = 1.0
    tab[:, ALIBI0 + 2] = (pos // 128) * 128
    tab[:, ALIBI0 + 3] = pos % 128
    return tab


def _prompt_consts(t):
    pos = np.arange(t)
    kaug_win = _alibi_key_cols(pos)
    kaug_sel = kaug_win.copy()
    kaug_sel[pos, SELBIT0 + pos // SEL_BLOCK] = -MASK_BIG
    n_chunk = t // CMP_STRIDE
    kaug_cmp = _alibi_key_cols(np.arange(n_chunk) * CMP_STRIDE + CMP_BLOCK - 1)
    n_cmp = (t - CMP_BLOCK) // CMP_STRIDE + 1
    n_slc = -(-t // SEL_BLOCK)
    ovlt = np.zeros((n_slc, n_chunk), np.float32)
    cs = np.arange(n_cmp)[None, :] * CMP_STRIDE
    ss = np.arange(n_slc)[:, None] * SEL_BLOCK
    ovlt[:, :n_cmp] = (cs < ss + SEL_BLOCK) & (cs + CMP_BLOCK > ss)
    tril = np.tril(np.ones((512, 512), np.float32))
    as_bf = lambda a: jnp.asarray(a, BF16)
    return dict(kaug_sel=as_bf(kaug_sel), kaug_win=as_bf(kaug_win), kaug_cmp=as_bf(kaug_cmp),
                ovlt=as_bf(ovlt), tril=as_bf(tril))


def _sample_ovl(n_key, n_cmp, n_slc, nb_pad):
    ovl = np.zeros((n_key, nb_pad), np.float32)
    cs = np.arange(n_cmp)[:, None] * CMP_STRIDE
    ss = np.arange(n_slc)[None, :] * SEL_BLOCK
    ovl[:n_cmp, :n_slc] = (cs < ss + SEL_BLOCK) & (cs + CMP_BLOCK > ss)
    return jnp.asarray(ovl, BF16)


def _perm_in_proj(w_in, b_in):
    o_qa, o_kva, o_ga, o_fox, o_fb, o_gm = 0, 512, 1280, 1304, 2840, 2848
    d = w_in.shape[0]

    def cols(x, zeros):
        return jnp.concatenate([x[..., o_gm:], x[..., o_fb:o_gm], zeros(LANES - FOX_HEADS)], axis=-1)

    def rows(x, zeros):
        return jnp.concatenate([x[..., o_qa:o_ga], x[..., o_fox:o_fb],
                                x[..., o_ga:o_fox], zeros(R_FB - R_GA - 3 * NSA_HEADS),
                                x[..., o_fb:o_gm], zeros(R_TOT - R_FB - FOX_HEADS)], axis=-1)

    w = cols(w_in, lambda k: jnp.zeros((d, k), w_in.dtype)).astype(BF16)
    b = cols(b_in, lambda k: jnp.zeros((k,), b_in.dtype)).reshape(1, C_TOT)
    wt = rows(w_in, lambda k: jnp.zeros((d, k), w_in.dtype)).T.astype(BF16)
    bt = rows(b_in, lambda k: jnp.zeros((k,), b_in.dtype)).reshape(R_TOT, 1)
    return w, b, wt, bt


def _compress_weights(pe_k, w1_k, w2_k, pe_v, w1_v, w2_v):
    n_sub = CMP_BLOCK // CMP_STRIDE

    def per_row(w1):
        w = w1.reshape(n_sub, CMP_STRIDE, HEAD_DIM, CMP_HIDDEN)
        return w.transpose(1, 2, 0, 3).reshape(CMP_STRIDE, HEAD_DIM, n_sub * CMP_HIDDEN)

    def per_kind(w1):
        blk = per_row(w1)
        zero = jnp.zeros_like(blk)
        w = jnp.stack([jnp.concatenate([blk, zero], axis=-1), jnp.concatenate([zero, blk], axis=-1)], axis=1)
        return w.reshape(CMP_STRIDE * NSA_KV_HEADS * HEAD_DIM, NSA_KV_HEADS * n_sub * CMP_HIDDEN)

    wr = jnp.stack([per_kind(w1_k), per_kind(w1_v)]).astype(BF16)
    pe8 = jnp.concatenate([pe_k.reshape(1, -1), pe_v.reshape(1, -1), jnp.zeros((14, CMP_BLOCK * HEAD_DIM), F32)], axis=0)
    w1cat = jnp.concatenate([w1_k, w1_v], axis=1).astype(BF16)
    pad = jnp.zeros((CMP_HIDDEN, LANES - HEAD_DIM), F32)
    w2pad = jnp.stack([jnp.concatenate([w2_k, pad], axis=1), jnp.concatenate([w2_v, pad], axis=1)]).astype(BF16)
    return dict(wr=wr, pe8=pe8.astype(BF16), w1cat=w1cat, w2pad=w2pad)


def _prompt_mixer(h1, w, nb, t):
    cmp_rows, logf, kh, gm, nsat, wint, foxt, logft, qat, nvt, fqt, fvt, gat = _in_proj(
        h1, w["w_in"], w["b_in"], w["wt"], w["bt"], nb, t, tm=TM_PROJ)
    consts = _prompt_consts(t)
    kcvc, vct = _compress_prompt(cmp_rows, w["cmp"], nb, t)
    o_a = _nsa_prompt(qat, kh, nvt, kcvc, vct, gat, consts, nb, t, tq=NSA_TQ, tk=NSA_TK)
    o_b = _fox_prompt(fqt, kh, fvt, logf, consts["tril"], nb, t, tq=FOX_TQ, tk=FOX_TK)
    n = nb * t
    state = (nsat.reshape(nb, 4, NSA_KV_HEADS, HEAD_DIM, t).transpose(0, 4, 1, 2, 3),
             wint.reshape(nb, 2, NSA_KV_HEADS, HEAD_DIM, t).transpose(0, 4, 1, 2, 3),
             foxt.reshape(nb, 2, FOX_HEADS, HEAD_DIM, t).transpose(0, 4, 1, 2, 3),
             logft.transpose(0, 2, 1))
    return o_a.reshape(n, -1), o_b.reshape(n, -1), gm, state


def _sample_mixer(h1, w, nb, t_new, nsa_pool, fox_pool, logf_pool, win_buf, page_table):
    n = nb * t_new
    past = page_table.shape[1] * PAGE
    _, _, _, gm, nsat, wint, foxt, logft, qat, _, fqt, _, gat = _in_proj(
        h1, w["w_in"], w["b_in"], w["wt"], w["bt"], 1, n, tm=n)
    nsa, win, fox, logf = nsat[0].T, wint[0].T, foxt[0].T, logft[0].T
    ga = gat[:3 * NSA_HEADS].T
    n_pool = nsa_pool.shape[0]
    nsa_t = nsa_pool.transpose(0, 2, 3, 4, 1).reshape(n_pool, 4, NSA_KV_HEADS * HEAD_DIM, PAGE)
    fox_t = fox_pool.transpose(0, 2, 3, 4, 1).reshape(n_pool, 2, FOX_HEADS * HEAD_DIM, PAGE)
    lf_t = logf_pool.transpose(0, 2, 1)
    wbuf_t = win_buf.transpose(0, 2, 3, 4, 1).reshape(nb, 2, NSA_KV_HEADS * HEAD_DIM, win_buf.shape[1])
    assert win_buf.shape[1] == WINDOW and t_new == 4 and past % SEL_BLOCK == 0

    seq_len = past + t_new
    n_cmp = (seq_len - CMP_BLOCK) // CMP_STRIDE + 1
    n_slc = -(-seq_len // SEL_BLOCK)
    n_chunk = past // CMP_STRIDE
    assert n_cmp <= n_chunk
    kcvc = _compress_sample(nsa_t, page_table, w["cmp"], n_pages=CMP_PAGES)

    q = qat[0].reshape(NSA_HEADS, HEAD_DIM, nb, t_new).transpose(2, 0, 3, 1).reshape(nb, SAMPLE_ROWS, HEAD_DIM)
    q = jnp.pad(q, ((0, 0), (0, 0), (0, LANES - HEAD_DIM)))
    nb_pad = -(-n_slc // LANES) * LANES
    ocmp, sel = _sample_select(q, kcvc, _sample_ovl(n_chunk, n_cmp, n_slc, nb_pad), past, n_cmp, n_slc, t_new)

    n_pages = NSA_PAGES
    n_steps = past // (n_pages * PAGE)
    blk_per_step = n_pages * PAGE // SEL_BLOCK
    selg = sel.reshape(nb, 2, 8, nb_pad)[:, :, :t_new, :n_steps * blk_per_step]
    selg = selg.reshape(nb, 2, 1, t_new, n_steps, blk_per_step)
    selx = jnp.broadcast_to(selg, (nb, 2, NSA_GROUP, t_new, n_steps, blk_per_step))
    selx = selx.transpose(0, 4, 1, 2, 3, 5).reshape(nb, n_steps, SAMPLE_ROWS, blk_per_step)
    selx = jnp.pad(selx, ((0, 0), (0, 0), (0, 0), (0, LANES - blk_per_step))).astype(BF16)
    expand = np.zeros((LANES, n_pages * PAGE), np.float32)
    expand[np.arange(n_pages * PAGE) // SEL_BLOCK, np.arange(n_pages * PAGE)] = 1.0
    q_sel = jnp.concatenate([q[:, :16], jnp.roll(q[:, 16:], HEAD_DIM, axis=-1)], axis=1)

    def new_rows(x, c0, width=LANES):
        r = x.reshape(nb, t_new, -1)[:, :, c0:c0 + width]
        return jnp.pad(r, ((0, 0), (0, TAIL - t_new), (0, 0))).astype(BF16)

    def new_cols(x, c0, width=LANES):
        return new_rows(x, c0, width).transpose(0, 2, 1)

    gate32 = ga.reshape(nb, t_new, 3, NSA_HEADS).transpose(0, 3, 1, 2).reshape(nb, SAMPLE_ROWS, 3)
    gate32 = jnp.pad(gate32, ((0, 0), (0, 0), (0, LANES - 3)))
    o_a32 = _sample_nsa(nsa_t, page_table, q_sel, selx, jnp.asarray(expand, BF16),
                        new_cols(nsa, 256), new_rows(nsa, 384), wbuf_t, new_cols(win, 0), new_rows(win, 128),
                        ocmp, gate32, n_pages, past, t_new)
    o_a = o_a32[:, :, :HEAD_DIM].reshape(nb, NSA_HEADS, t_new, HEAD_DIM).transpose(0, 2, 1, 3)
    o_a = o_a.reshape(n, NSA_HEADS * HEAD_DIM).astype(BF16)

    width = FOX_HEADS * HEAD_DIM
    qf = fqt[0].reshape(FOX_HEADS, HEAD_DIM, nb, t_new).transpose(2, 3, 0, 1)
    eye = jnp.eye(FOX_HEADS, dtype=qf.dtype)
    qbd = (qf[:, :, :, None, :] * eye[None, None, :, :, None]).reshape(nb, SAMPLE_ROWS, width)
    triu = jnp.asarray(np.triu(np.ones((PAGE, PAGE), np.float32)), BF16)
    lfnew_t = jnp.pad(logf.reshape(nb, t_new, FOX_HEADS), ((0, 0), (0, TAIL - t_new), (0, 0))).transpose(0, 2, 1)
    o_b32 = _sample_fox(fox_t, lf_t, page_table, qbd, triu, new_cols(fox, 0, width), new_rows(fox, width, width),
                        lfnew_t, FOX_PAGES, t_new)
    o_b = o_b32.reshape(nb, t_new, FOX_HEADS, FOX_HEADS, HEAD_DIM)
    o_b = jnp.einsum("bthgd,hg->bthd", o_b, jnp.eye(FOX_HEADS, dtype=o_b.dtype))
    o_b = o_b.reshape(n, width).astype(BF16)
    state = (nsa.reshape(nb, t_new, 4, NSA_KV_HEADS, HEAD_DIM), win.reshape(nb, t_new, 2, NSA_KV_HEADS, HEAD_DIM),
             fox.reshape(nb, t_new, 2, FOX_HEADS, HEAD_DIM), logf.reshape(nb, t_new, FOX_HEADS))
    return o_a, o_b, gm, state


def _layer(x, w, alpha, mixer, tm):
    h1 = _ffn_ln(x, w["ffn1_wg"], w["ffn1_wu"], w["ffn1_wd"], w["ln1_g"], w["ln1_b"], alpha, tm)
    o_a, o_b, gm, state = mixer(h1, w)
    h2 = _merge_ln(o_a, o_b, gm, h1, w["w_proj_a"], w["w_proj_b"], w["w_out"], w["ln2_g"], w["ln2_b"], alpha, tm)
    y = _ffn_ln(h2, w["ffn2_wg"], w["ffn2_wu"], w["ffn2_wd"], w["ln3_g"], w["ln3_b"], alpha, tm)
    return y, state


def kernel(x_prompt, x_sample, cache_nsa_kv, cache_fox_kv, cache_fox_logf, state_win_kv, page_table,
           ln1_g, ln1_b, ffn1_w_gate, ffn1_w_up, ffn1_w_down, w_in, b_in,
           cmp_pe_k, cmp_w1_k, cmp_w2_k, cmp_pe_v, cmp_w1_v, cmp_w2_v,
           w_proj_a, w_proj_b, w_out, ln2_g, ln2_b,
           ffn2_w_gate, ffn2_w_up, ffn2_w_down, ln3_g, ln3_b):
    depth = w_in.shape[0]
    nb_p, t_p, d = x_prompt.shape
    nb_s, t_s, _ = x_sample.shape
    alpha = float((2.0 * depth) ** 0.25)
    h_p = x_prompt.reshape(nb_p * t_p, d)
    h_s = x_sample.reshape(nb_s * t_s, d)
    st_p, st_s = [], []
    for l in range(depth):
        wi, bi, wt, bt = _perm_in_proj(w_in[l], b_in[l])
        vec = lambda a: a[l].reshape(1, d)
        w = dict(
            ln1_g=vec(ln1_g), ln1_b=vec(ln1_b), ln2_g=vec(ln2_g), ln2_b=vec(ln2_b), ln3_g=vec(ln3_g), ln3_b=vec(ln3_b),
            ffn1_wg=ffn1_w_gate[l].astype(BF16), ffn1_wu=ffn1_w_up[l].astype(BF16), ffn1_wd=ffn1_w_down[l].astype(BF16),
            ffn2_wg=ffn2_w_gate[l].astype(BF16), ffn2_wu=ffn2_w_up[l].astype(BF16), ffn2_wd=ffn2_w_down[l].astype(BF16),
            w_in=wi, b_in=bi, wt=wt, bt=bt,
            cmp=_compress_weights(cmp_pe_k[l], cmp_w1_k[l], cmp_w2_k[l], cmp_pe_v[l], cmp_w1_v[l], cmp_w2_v[l]),
            w_proj_a=w_proj_a[l].astype(BF16), w_proj_b=w_proj_b[l].astype(BF16), w_out=w_out[l].astype(BF16),
        )
        h_p, s_p = _layer(h_p, w, alpha, functools.partial(_prompt_mixer, nb=nb_p, t=t_p), tm=TM_FFN)
        h_s, s_s = _layer(h_s, w, alpha, functools.partial(
            _sample_mixer, nb=nb_s, t_new=t_s, nsa_pool=cache_nsa_kv[l], fox_pool=cache_fox_kv[l],
            logf_pool=cache_fox_logf[l], win_buf=state_win_kv[l], page_table=page_table), tm=nb_s * t_s)
        st_p.append(s_p)
        st_s.append(s_s)

    def states(st, nb, t, win_prev):
        nsa = jnp.stack([s[0] for s in st])
        fox = jnp.stack([s[2] for s in st])
        logf = jnp.stack([s[3] for s in st])
        wins = []
        for l, s in enumerate(st):
            wr = s[1]
            if win_prev is None:
                wins.append(wr[:, t - min(WINDOW, t):])
            else:
                wins.append(jnp.concatenate([win_prev[l], wr], axis=1)[:, t:])
        return nsa, fox, logf, jnp.stack(wins)

    nsa_p, fox_p, logf_p, win_p = states(st_p, nb_p, t_p, None)
    nsa_s, fox_s, logf_s, win_s = states(st_s, nb_s, t_s, state_win_kv)
    return (h_p.reshape(nb_p, t_p, d), h_s.reshape(nb_s, t_s, d), nsa_p, fox_p, logf_p, win_p,
            nsa_s, fox_s, logf_s, win_s)
```

```python
import functools

import numpy as np
import jax
import jax.numpy as jnp
from jax import lax
from jax.experimental import pallas as pl
from jax.experimental.pallas import tpu as pltpu

F32 = jnp.float32
BF16 = jnp.bfloat16

LANES = 128
VMEM_LIMIT = 56 * 1024 * 1024

TM_FFN = 512
TM_PROJ = 256
NSA_TQ, NSA_TK = 512, 512
FOX_TQ, FOX_TK = 1024, 512
CMP_PAGES = 32
NSA_PAGES = 32
FOX_PAGES = 16

D_MODEL = 1024
HEAD_DIM = 64
NSA_HEADS = 8
NSA_KV_HEADS = 2
NSA_GROUP = NSA_HEADS // NSA_KV_HEADS
FOX_HEADS = 8
CMP_BLOCK = 32
CMP_STRIDE = 16
CMP_HIDDEN = 2 * HEAD_DIM
SEL_BLOCK = 64
N_SEL = 16
WINDOW = 512
PAGE = 128
LN_EPS = 1e-5
NEG = -1e30
FORCE = 1e9
SCALE = HEAD_DIM ** -0.5
MASK_BIG = 30000.0

SELBIT0 = 64
ALIBI0 = 96


def _dot(a, b):
    return jnp.dot(a, b, preferred_element_type=F32)


def _dot_nt(a, b):
    return lax.dot_general(a, b, (((1,), (1,)), ((), ())), preferred_element_type=F32)


def _split3(x):
    hi = x.astype(BF16)
    r1 = x - hi.astype(F32)
    mid = r1.astype(BF16)
    lo = (r1 - mid.astype(F32)).astype(BF16)
    return hi, mid, lo


def _dot3(x, w):
    hi, mid, lo = _split3(x)
    return _dot(hi, w) + _dot(mid, w) + _dot(lo, w)


def _dot3_nt(a, x):
    hi, mid, lo = _split3(x)
    return _dot_nt(a, hi) + _dot_nt(a, mid) + _dot_nt(a, lo)


def _sigmoid(x):
    return 1.0 / (1.0 + jnp.exp(-x))


def _log_sigmoid(x):
    return -(jnp.maximum(-x, 0.0) + jnp.log(1.0 + jnp.exp(-jnp.abs(x))))


def _gelu_tanh(x):
    c = np.float32(np.sqrt(2.0 / np.pi))
    return 0.5 * x * (1.0 + jnp.tanh(c * (x + np.float32(0.044715) * (x * x * x))))


def _cparams(sem):
    return pltpu.CompilerParams(dimension_semantics=sem, vmem_limit_bytes=VMEM_LIMIT)


def _ffn_ln_kernel(x_ref, wg_ref, wu_ref, wd_ref, g_ref, b_ref, o_ref, acc_ref, *, alpha, n_ff):
    j = pl.program_id(1)

    @pl.when(j == 0)
    def _():
        acc_ref[...] = jnp.zeros_like(acc_ref)

    xb = x_ref[...].astype(BF16)
    gate = _dot(xb, wg_ref[...])
    up = _dot(xb, wu_ref[...])
    mid = (gate * _sigmoid(gate) * up).astype(BF16)
    acc_ref[...] += _dot(mid, wd_ref[...])

    @pl.when(j == n_ff - 1)
    def _():
        y = alpha * x_ref[...] + 0.5 * acc_ref[...]
        mu = jnp.mean(y, axis=-1, keepdims=True)
        d = y - mu
        var = jnp.mean(d * d, axis=-1, keepdims=True)
        o_ref[...] = d * lax.rsqrt(var + LN_EPS) * g_ref[...] + b_ref[...]


def _ffn_ln(x, wg, wu, wd, g, b, alpha, tm):
    n, d = x.shape
    dff = wg.shape[1]
    tf = dff // 2
    n_ff = dff // tf
    return pl.pallas_call(
        functools.partial(_ffn_ln_kernel, alpha=alpha, n_ff=n_ff),
        grid=(n // tm, n_ff),
        in_specs=[
            pl.BlockSpec((tm, d), lambda i, j: (i, 0)),
            pl.BlockSpec((d, tf), lambda i, j: (0, j)),
            pl.BlockSpec((d, tf), lambda i, j: (0, j)),
            pl.BlockSpec((tf, d), lambda i, j: (j, 0)),
            pl.BlockSpec((1, d), lambda i, j: (0, 0)),
            pl.BlockSpec((1, d), lambda i, j: (0, 0)),
        ],
        out_specs=pl.BlockSpec((tm, d), lambda i, j: (i, 0)),
        out_shape=jax.ShapeDtypeStruct((n, d), F32),
        scratch_shapes=[pltpu.VMEM((tm, d), F32)],
        compiler_params=_cparams(("parallel", "arbitrary")),
        name="ffn_ln",
    )(x, wg, wu, wd, g, b)


C_GM = 0
C_FB = 2048
C_TOT = 2176
R_QA = 0
R_KVA = 512
R_FQ = 1280
R_FKV = 1792
R_GA = 2816
R_FB = 2848
R_TOT = 2864
N_KH = 12
VT_ROWS = 80


def _head_pad(z, h):
    col = z[:, LANES * (h // 2):LANES * (h // 2 + 1)]
    if h % 2:
        col = pltpu.roll(col, HEAD_DIM, axis=1)
    lane = lax.broadcasted_iota(jnp.int32, col.shape, 1)
    return jnp.where(lane < HEAD_DIM, col, 0.0)


def _ones_row_block(width):
    sub = lax.broadcasted_iota(jnp.int32, (VT_ROWS - HEAD_DIM, width), 0)
    return jnp.where(sub == 0, 1.0, 0.0)


def _in_proj_kernel(h_ref, w_ref, b_ref, wt_ref, bt_ref,
                    cmp_ref, logf_ref, kh_ref, gm_ref,
                    nsat_ref, wint_ref, foxt_ref, logft_ref,
                    qat_ref, nvt_ref, fqt_ref, fvt_ref, gat_ref):
    hb = h_ref[...].astype(BF16)
    tm = hb.shape[0]

    def proj(c0, c1):
        return _dot(hb, w_ref[:, c0:c1]) + b_ref[:, c0:c1]

    def proj_t(r0, r1):
        return _dot_nt(wt_ref[r0:r1, :], hb) + bt_ref[r0:r1, :]

    gm_ref[...] = _sigmoid(proj(C_GM, C_FB))
    logf_ref[...] = _log_sigmoid(proj(C_FB, C_TOT))[:, :FOX_HEADS]

    ones_blk = _ones_row_block(tm)

    def value_head(zt, r0):
        return jnp.concatenate([zt[r0:r0 + HEAD_DIM], ones_blk], axis=0).astype(BF16)

    qat_ref[...] = (proj_t(R_QA, R_KVA) * SCALE).astype(BF16).reshape(qat_ref.shape)

    zkv = proj_t(R_KVA, R_FQ)
    nsat_ref[...] = zkv[:512]
    wint_ref[...] = zkv[512:]
    for j, head in enumerate((6, 7, 10, 11)):
        nvt_ref[j] = value_head(zkv, head * HEAD_DIM)

    fqt_ref[...] = (proj_t(R_FQ, R_FKV) * SCALE).astype(BF16).reshape(fqt_ref.shape)
    zf = proj_t(R_FKV, R_GA)
    foxt_ref[...] = zf
    for h in range(FOX_HEADS):
        fvt_ref[h] = value_head(zf, (FOX_HEADS + h) * HEAD_DIM)

    cmp_ref[...] = zkv[:4 * HEAD_DIM].T
    keys = jnp.concatenate([zkv[4 * HEAD_DIM:6 * HEAD_DIM], zkv[8 * HEAD_DIM:10 * HEAD_DIM],
                            zf[:FOX_HEADS * HEAD_DIM]], axis=0).T
    for h in range(N_KH):
        kh_ref[h] = _head_pad(keys, h).astype(BF16)

    gat_ref[...] = _sigmoid(proj_t(R_GA, R_FB))
    logft_ref[...] = _log_sigmoid(proj_t(R_FB, R_TOT))[:FOX_HEADS]


def _in_proj(h, w, b, wt, bt, nseq, tseq, tm):
    n, d = h.shape
    tps = tseq // tm
    row = lambda i: (i, 0)
    headmaj = lambda i: (i // tps, 0, i % tps, 0)
    featmaj = lambda i: (i // tps, 0, 0, i % tps)
    feat3 = lambda i: (i // tps, 0, i % tps)
    n_ga = R_FB - R_GA
    out_shape = [
        jax.ShapeDtypeStruct((n, 256), F32),
        jax.ShapeDtypeStruct((n, FOX_HEADS), F32),
        jax.ShapeDtypeStruct((nseq, N_KH, tseq, LANES), BF16),
        jax.ShapeDtypeStruct((n, 2 * D_MODEL), F32),
        jax.ShapeDtypeStruct((nseq, 512, tseq), F32),
        jax.ShapeDtypeStruct((nseq, 256, tseq), F32),
        jax.ShapeDtypeStruct((nseq, 1024, tseq), F32),
        jax.ShapeDtypeStruct((nseq, FOX_HEADS, tseq), F32),
        jax.ShapeDtypeStruct((nseq, NSA_HEADS, HEAD_DIM, tseq), BF16),
        jax.ShapeDtypeStruct((nseq, 4, VT_ROWS, tseq), BF16),
        jax.ShapeDtypeStruct((nseq, FOX_HEADS, HEAD_DIM, tseq), BF16),
        jax.ShapeDtypeStruct((nseq, FOX_HEADS, VT_ROWS, tseq), BF16),
        jax.ShapeDtypeStruct((n_ga, n), F32),
    ]
    out_specs = [
        pl.BlockSpec((tm, 256), row),
        pl.BlockSpec((tm, FOX_HEADS), row),
        pl.BlockSpec((None, N_KH, tm, LANES), headmaj),
        pl.BlockSpec((tm, 2 * D_MODEL), row),
        pl.BlockSpec((None, 512, tm), feat3),
        pl.BlockSpec((None, 256, tm), feat3),
        pl.BlockSpec((None, 1024, tm), feat3),
        pl.BlockSpec((None, FOX_HEADS, tm), feat3),
        pl.BlockSpec((None, NSA_HEADS, HEAD_DIM, tm), featmaj),
        pl.BlockSpec((None, 4, VT_ROWS, tm), featmaj),
        pl.BlockSpec((None, FOX_HEADS, HEAD_DIM, tm), featmaj),
        pl.BlockSpec((None, FOX_HEADS, VT_ROWS, tm), featmaj),
        pl.BlockSpec((n_ga, tm), lambda i: (0, i)),
    ]
    return pl.pallas_call(
        _in_proj_kernel,
        grid=(n // tm,),
        in_specs=[
            pl.BlockSpec((tm, d), row),
            pl.BlockSpec((d, C_TOT), lambda i: (0, 0)),
            pl.BlockSpec((1, C_TOT), lambda i: (0, 0)),
            pl.BlockSpec((R_TOT, d), lambda i: (0, 0)),
            pl.BlockSpec((R_TOT, 1), lambda i: (0, 0)),
        ],
        out_specs=out_specs,
        out_shape=out_shape,
        compiler_params=_cparams(("parallel",)),
        name="in_proj",
    )(h, w, b, wt, bt)


def _cmp_parts(k_ref, v_ref, wr_ref, n_rows):
    n_chunk = n_rows // CMP_STRIDE

    def kind(ref, w):
        x = jnp.concatenate([ref[pl.ds(r, n_chunk, stride=CMP_STRIDE), :] for r in range(CMP_STRIDE)], axis=1)
        return _dot(x.astype(BF16), w)

    return jnp.concatenate([kind(k_ref, wr_ref[0]), kind(v_ref, wr_ref[1])], axis=1)


def _cmp_finish(parts, bias8, w2_ref, out_ref, vt_ref=None):
    n = parts.shape[0]
    for s in range(4):
        kind = s // 2
        pa = parts[:, 256 * s:256 * s + CMP_HIDDEN]
        pb = parts[:, 256 * s + CMP_HIDDEN:256 * (s + 1)]
        pb = pltpu.roll(pb, n - 1, axis=0)
        bias = bias8[kind:kind + 1, CMP_HIDDEN * kind:CMP_HIDDEN * (kind + 1)]
        hid = bias + pa + pb
        c = _dot(_gelu_tanh(hid).astype(BF16), w2_ref[kind])
        out_ref[s] = c.astype(BF16)
        if vt_ref is not None and kind == 1:
            vt = jnp.concatenate([c.T[:HEAD_DIM], _ones_row_block(n)], axis=0)
            vt_ref[s - 2] = vt.astype(BF16)


def _cmp_bias(pe_ref, w1_ref):
    return _dot(pe_ref[...], w1_ref[...])


def _compress_prompt_kernel(k_ref, v_ref, wr_ref, pe_ref, w1_ref, w2_ref, out_ref, vt_ref, *, t):
    parts = _cmp_parts(k_ref, v_ref, wr_ref, t)
    _cmp_finish(parts, _cmp_bias(pe_ref, w1_ref), w2_ref, out_ref, vt_ref)


def _compress_prompt(nsa_rows, cw, nb, t):
    n_chunk = t // CMP_STRIDE
    return pl.pallas_call(
        functools.partial(_compress_prompt_kernel, t=t),
        grid=(nb,),
        in_specs=[
            pl.BlockSpec((t, LANES), lambda b: (b, 0)),
            pl.BlockSpec((t, LANES), lambda b: (b, 1)),
            pl.BlockSpec((2, CMP_STRIDE * LANES, 512), lambda b: (0, 0, 0)),
            pl.BlockSpec((16, 2048), lambda b: (0, 0)),
            pl.BlockSpec((2048, 256), lambda b: (0, 0)),
            pl.BlockSpec((2, CMP_HIDDEN, LANES), lambda b: (0, 0, 0)),
        ],
        out_specs=[
            pl.BlockSpec((None, 4, n_chunk, LANES), lambda b: (b, 0, 0, 0)),
            pl.BlockSpec((None, 2, VT_ROWS, n_chunk), lambda b: (b, 0, 0, 0)),
        ],
        out_shape=[
            jax.ShapeDtypeStruct((nb, 4, n_chunk, LANES), BF16),
            jax.ShapeDtypeStruct((nb, 2, VT_ROWS, n_chunk), BF16),
        ],
        compiler_params=_cparams(("parallel",)),
        name="compress_prompt",
    )(nsa_rows, nsa_rows, cw["wr"], cw["pe8"], cw["w1cat"], cw["w2pad"])


def _compress_sample_kernel(pt_ref, *refs, n_pages, n_steps):
    k_refs = refs[:n_pages]
    v_refs = refs[n_pages:2 * n_pages]
    wr_ref, pe_ref, w1_ref, w2_ref, out_ref, parts_ref, krows_ref, vrows_ref = refs[2 * n_pages:]
    pg = pl.program_id(1)
    rows = n_pages * (PAGE // CMP_STRIDE)
    for j in range(n_pages):
        krows_ref[j * PAGE:(j + 1) * PAGE, :] = k_refs[j][...].T
        vrows_ref[j * PAGE:(j + 1) * PAGE, :] = v_refs[j][...].T
    parts = _cmp_parts(krows_ref, vrows_ref, wr_ref, n_pages * PAGE)
    parts_ref[pl.ds(pl.multiple_of(pg * rows, rows), rows), :] = parts

    @pl.when(pg == n_steps - 1)
    def _():
        _cmp_finish(parts_ref[...], _cmp_bias(pe_ref, w1_ref), w2_ref, out_ref)


def _compress_sample(pool_t, page_table, cw, n_pages):
    nb, pages_per_seq = page_table.shape
    n_steps = pages_per_seq // n_pages
    n_chunk = pages_per_seq * (PAGE // CMP_STRIDE)
    page_specs = [
        pl.BlockSpec((None, None, LANES, PAGE), lambda b, pg, pt, j=j, c=c: (pt[b, pg * n_pages + j], c, 0, 0))
        for c in range(2) for j in range(n_pages)
    ]
    grid_spec = pltpu.PrefetchScalarGridSpec(
        num_scalar_prefetch=1,
        grid=(nb, n_steps),
        in_specs=page_specs + [
            pl.BlockSpec((2, CMP_STRIDE * LANES, 512), lambda b, pg, pt: (0, 0, 0)),
            pl.BlockSpec((16, 2048), lambda b, pg, pt: (0, 0)),
            pl.BlockSpec((2048, 256), lambda b, pg, pt: (0, 0)),
            pl.BlockSpec((2, CMP_HIDDEN, LANES), lambda b, pg, pt: (0, 0, 0)),
        ],
        out_specs=pl.BlockSpec((None, 4, n_chunk, LANES), lambda b, pg, pt: (b, 0, 0, 0)),
        scratch_shapes=[pltpu.VMEM((n_chunk, 1024), F32),
                        pltpu.VMEM((n_pages * PAGE, LANES), F32), pltpu.VMEM((n_pages * PAGE, LANES), F32)],
    )
    return pl.pallas_call(
        functools.partial(_compress_sample_kernel, n_pages=n_pages, n_steps=n_steps),
        grid_spec=grid_spec,
        out_shape=jax.ShapeDtypeStruct((nb, 4, n_chunk, LANES), BF16),
        compiler_params=_cparams(("parallel", "arbitrary")),
        name="compress_sample",
    )(page_table, *([pool_t] * (2 * n_pages)), cw["wr"], cw["pe8"], cw["w1cat"], cw["w2pad"])


def _softmax_update(s, v, m_ref, l_ref, acc_ref):
    m_prev = m_ref[...]
    m_new = jnp.maximum(m_prev, jnp.max(s, axis=1, keepdims=True))
    a = jnp.exp(m_prev - m_new)
    p = jnp.exp(s - m_new)
    l_ref[...] = a * l_ref[...] + jnp.sum(p, axis=1, keepdims=True)
    acc_ref[...] = a * acc_ref[...] + _dot(p.astype(BF16), v)
    m_ref[...] = m_new


def _softmax_update_t(s, vt, m_ref, acc_ref):
    m_prev = m_ref[...]
    m_new = jnp.maximum(m_prev, jnp.max(s, axis=0, keepdims=True))
    a = jnp.exp(m_prev - m_new)
    p = jnp.exp(s - m_new).astype(BF16)
    acc_ref[...] = a * acc_ref[...] + _dot(vt, p)
    m_ref[...] = m_new


def _softmax_reset(m_ref, acc_ref):
    m_ref[...] = jnp.full_like(m_ref, NEG)
    acc_ref[...] = jnp.zeros_like(acc_ref)


def _softmax_result(acc_ref):
    acc = acc_ref[...]
    return acc[:HEAD_DIM] / acc[HEAD_DIM:HEAD_DIM + 1]


def _nsa_prompt_kernel(q_ref, selk_ref, selvt_ref, wink_ref, winvt_ref, kc_ref, vct_ref, gat_ref,
                       kaug_sel_ref, kaug_win_ref, kaug_cmp_ref, ovlt_ref,
                       o_ref, ksel_s, kwin_s, m_ref, acc_ref, *, tq, tk, n_cmp, n_slc):
    g = pl.program_id(1)
    qt = pl.program_id(2)
    cols = NSA_GROUP * tq
    q0 = qt * tq

    @pl.when(qt == 0)
    def _():
        ksel_s[...] = selk_ref[...] + kaug_sel_ref[...]
        kwin_s[...] = wink_ref[...] + kaug_win_ref[...]

    qt4 = jnp.concatenate([q_ref[r] for r in range(NSA_GROUP)], axis=1)

    def head_and_pos(shape):
        lane = lax.broadcasted_iota(jnp.int32, shape, 1)
        return lane >> (tq.bit_length() - 1), q0 + (lane & (tq - 1))

    sub = lax.broadcasted_iota(jnp.int32, (16, cols), 0)
    r, t = head_and_pos((16, cols))
    sl0 = jnp.where(r == 0, 0.5, jnp.where(r == 1, 0.25, jnp.where(r == 2, 0.125, 0.0625))).astype(F32)
    slope = jnp.where(g == 0, sl0, sl0 * 0.0625)
    t_hi = ((t >> 7) << 7).astype(F32)
    t_lo = (t & 127).astype(F32)
    al = jnp.where(sub == 0, -slope * t_hi,
                   jnp.where(sub == 1, -slope * t_lo,
                             jnp.where((sub == 2) | (sub == 3), slope, 0.0))).astype(BF16)
    zeros16 = jnp.zeros((16, cols), BF16)

    def query_cols(selbits):
        return jnp.concatenate([qt4, selbits, al, zeros16], axis=0)

    qa = query_cols(jnp.zeros((n_slc, cols), BF16))

    n_chunk = kc_ref.shape[0]
    kc = kc_ref[...] + kaug_cmp_ref[...]
    s = _dot(kc, qa)
    nidx = lax.broadcasted_iota(jnp.int32, (n_chunk, cols), 0)
    _, t_c = head_and_pos((n_chunk, cols))
    valid = (t_c >= nidx * CMP_STRIDE + (CMP_BLOCK - 1)) & (nidx < n_cmp)
    s = jnp.where(valid, s, NEG)
    m = jnp.max(s, axis=0, keepdims=True)
    p = jnp.where(valid, jnp.exp(s - m), 0.0)
    den = jnp.sum(p, axis=0, keepdims=True)
    p = p / jnp.where(den > 0.0, den, 1.0)
    o_cmp = _dot(vct_ref[...], p.astype(BF16))[:HEAD_DIM]

    imp = p[:, 0:tq] + p[:, tq:2 * tq] + p[:, 2 * tq:3 * tq] + p[:, 3 * tq:4 * tq]
    hi, mid, lo = _split3(imp)
    ovlt = ovlt_ref[...]
    imp_slc = _dot(ovlt, hi) + _dot(ovlt, mid) + _dot(ovlt, lo)
    blk = lax.broadcasted_iota(jnp.int32, (n_slc, tq), 0)
    t1 = q0 + lax.broadcasted_iota(jnp.int32, (n_slc, tq), 1)
    cur = t1 >> 6
    forced = (blk == 0) | (blk == cur) | (blk == cur - 1)
    score = jnp.where(forced, FORCE, jnp.where(blk * SEL_BLOCK <= t1, imp_slc, -FORCE))
    rank = jnp.zeros((n_slc, tq), F32)
    for i in range(n_slc):
        row = score[i:i + 1, :]
        beats = (row > score) | ((row == score) & (blk > i))
        rank = rank + jnp.where(beats, 1.0, 0.0)
    notsel = jnp.where(rank >= float(N_SEL), 1.0, 0.0).astype(BF16)
    qs = query_cols(jnp.concatenate([notsel] * NSA_GROUP, axis=1))

    kpos_iota = lax.broadcasted_iota(jnp.int32, (tk, cols), 0)
    _, t_row = head_and_pos((1, cols))
    n_full = q0 // tk

    def branch(qx, k_s, vt_ref, kt_lo, window):
        _softmax_reset(m_ref, acc_ref)

        def tile(k0, causal):
            sc = _dot(k_s[pl.ds(k0, tk), :], qx)
            rel = t_row - k0
            if causal:
                sc = jnp.where(kpos_iota <= rel, sc, NEG)
            elif window:
                sc = jnp.where(kpos_iota > rel - WINDOW, sc, NEG)
            _softmax_update_t(sc, vt_ref[:, pl.ds(k0, tk)], m_ref, acc_ref)

        def body(kt, carry):
            tile(pl.multiple_of(kt * tk, tk), False)
            return carry

        lax.fori_loop(kt_lo, n_full, body, 0)
        for j in range(tq // tk):
            tile(pl.multiple_of(q0 + j * tk, tk), True)
        return _softmax_result(acc_ref)

    o_sel = branch(qs, ksel_s, selvt_ref, 0, False)
    o_win = branch(qa, kwin_s, winvt_ref, jnp.maximum(q0 - WINDOW, 0) // tk, True)

    outs = []
    for rr in range(NSA_GROUP):
        def gate(br):
            return gat_ref[pl.ds(br * NSA_HEADS + g * NSA_GROUP + rr, 1), :]
        sl = slice(rr * tq, (rr + 1) * tq)
        outs.append(gate(0) * o_cmp[:, sl] + gate(1) * o_sel[:, sl] + gate(2) * o_win[:, sl])
    o_ref[...] = jnp.concatenate(outs, axis=0).T.astype(BF16)


def _nsa_prompt(qat, kh, nvt, kcvc, vct, gat, consts, nb, t, tq, tk):
    n_cmp = (t - CMP_BLOCK) // CMP_STRIDE + 1
    n_slc = -(-t // SEL_BLOCK)
    n_chunk = t // CMP_STRIDE
    assert n_slc == 32 and tq % tk == 0 and tq <= WINDOW
    q5 = qat.reshape(nb, NSA_KV_HEADS, NSA_GROUP, HEAD_DIM, t)
    k_spec = lambda off: pl.BlockSpec((None, None, t, LANES), lambda b, g, qt: (b, off + g, 0, 0))
    vt_spec = lambda off: pl.BlockSpec((None, None, VT_ROWS, t), lambda b, g, qt: (b, off + g, 0, 0))
    const2 = lambda shape: pl.BlockSpec(shape, lambda b, g, qt: (0, 0))
    cols = NSA_GROUP * tq
    return pl.pallas_call(
        functools.partial(_nsa_prompt_kernel, tq=tq, tk=tk, n_cmp=n_cmp, n_slc=n_slc),
        grid=(nb, NSA_KV_HEADS, t // tq),
        in_specs=[
            pl.BlockSpec((None, None, NSA_GROUP, HEAD_DIM, tq), lambda b, g, qt: (b, g, 0, 0, qt)),
            k_spec(0), vt_spec(0), k_spec(2), vt_spec(2),
            pl.BlockSpec((None, None, n_chunk, LANES), lambda b, g, qt: (b, g, 0, 0)),
            pl.BlockSpec((None, None, VT_ROWS, n_chunk), lambda b, g, qt: (b, g, 0, 0)),
            pl.BlockSpec((R_FB - R_GA, tq), lambda b, g, qt: (0, b * (t // tq) + qt)),
            const2((t, LANES)), const2((t, LANES)), const2((n_chunk, LANES)), const2((n_slc, n_chunk)),
        ],
        out_specs=pl.BlockSpec((None, tq, NSA_GROUP * HEAD_DIM), lambda b, g, qt: (b, qt, g)),
        out_shape=jax.ShapeDtypeStruct((nb, t, NSA_HEADS * HEAD_DIM), BF16),
        scratch_shapes=[
            pltpu.VMEM((t, LANES), BF16), pltpu.VMEM((t, LANES), BF16),
            pltpu.VMEM((1, cols), F32), pltpu.VMEM((VT_ROWS, cols), F32),
        ],
        compiler_params=_cparams(("parallel", "parallel", "arbitrary")),
        name="nsa_prompt",
    )(q5, kh, nvt, kh, nvt, kcvc, vct, gat,
      consts["kaug_sel"], consts["kaug_win"], consts["kaug_cmp"], consts["ovlt"])


FOX_BIAS0 = 64
FOX_PAIR = 2


def _fox_prompt_kernel(q_ref, k_ref, vt_ref, lf_ref, tril_ref, o_ref, c_s, kaug_s, m_ref, acc_ref, *, tq, tk):
    hp = pl.program_id(1)
    qt = pl.program_id(2)
    q0 = qt * tq
    t = k_ref.shape[1]

    @pl.when((hp == 0) & (qt == 0))
    def _():
        tb = tril_ref.shape[0]
        tril = tril_ref[...]
        carry = jnp.zeros((1, FOX_HEADS), F32)
        cs = []
        for i in range(t // tb):
            hi, mid, lo = _split3(lf_ref[i * tb:(i + 1) * tb, :])
            c = _dot(tril, hi) + _dot(tril, mid) + _dot(tril, lo) + carry
            cs.append(c)
            carry = c[tb - 1:tb, :]
        for j, piece in enumerate(_split3(-jnp.concatenate(cs, axis=0))):
            c_s[j] = piece

    @pl.when(qt == 0)
    def _():
        hi, mid, lo = c_s[0], c_s[1], c_s[2]
        hrow = lax.broadcasted_iota(jnp.int32, (FOX_HEADS, LANES), 0)
        lane = lax.broadcasted_iota(jnp.int32, (FOX_HEADS, LANES), 1)
        for i in range(FOX_PAIR):
            head = FOX_PAIR * hp + i

            def place(j):
                return jnp.where((hrow == head) & (lane == FOX_BIAS0 + j), 1.0, 0.0).astype(BF16)

            aug = _dot(hi, place(0)) + _dot(mid, place(1)) + _dot(lo, place(2))
            kaug_s[i] = k_ref[i] + aug.astype(BF16)

    sub = lax.broadcasted_iota(jnp.int32, (16, tq), 0)
    ones3 = jnp.where(sub < 3, 1.0, 0.0).astype(BF16)
    zeros = jnp.zeros((LANES - HEAD_DIM - 16, tq), BF16)
    qx = [jnp.concatenate([q_ref[i], ones3, zeros], axis=0) for i in range(FOX_PAIR)]

    for i in range(FOX_PAIR):
        _softmax_reset(m_ref.at[i], acc_ref.at[i])

    def tile(k0, masked, c0=0):
        for i in range(FOX_PAIR):
            sc = _dot(kaug_s[i, pl.ds(k0, tk), :], qx[i][:, c0:])
            if masked:
                kpos = lax.broadcasted_iota(jnp.int32, (tk, tq - c0), 0)
                rel = (q0 + c0 - k0) + lax.broadcasted_iota(jnp.int32, (1, tq - c0), 1)
                sc = jnp.where(kpos <= rel, sc, NEG)
            _softmax_update_t(sc, vt_ref[i, :, pl.ds(k0, tk)],
                              m_ref.at[i, :, pl.ds(c0, tq - c0)], acc_ref.at[i, :, pl.ds(c0, tq - c0)])

    def body(kt, carry):
        tile(pl.multiple_of(kt * tk, tk), False)
        return carry

    lax.fori_loop(0, q0 // tk, body, 0)
    for j in range(tq // tk):
        tile(pl.multiple_of(q0 + j * tk, tk), True, c0=j * tk)
    o_t = jnp.concatenate([_softmax_result(acc_ref.at[i]) for i in range(FOX_PAIR)], axis=0)
    o_ref[...] = o_t.T.astype(BF16)


def _fox_prompt(fqt, kh, fvt, logf, tril, nb, t, tq, tk):
    n_pair = FOX_HEADS // FOX_PAIR
    q5 = fqt.reshape(nb, n_pair, FOX_PAIR, HEAD_DIM, t)
    k5 = kh.reshape(nb, N_KH // FOX_PAIR, FOX_PAIR, t, LANES)
    fox_pair0 = (N_KH - FOX_HEADS) // FOX_PAIR
    v5 = fvt.reshape(nb, n_pair, FOX_PAIR, VT_ROWS, t)
    return pl.pallas_call(
        functools.partial(_fox_prompt_kernel, tq=tq, tk=tk),
        grid=(nb, n_pair, t // tq),
        in_specs=[
            pl.BlockSpec((None, None, FOX_PAIR, HEAD_DIM, tq), lambda b, hp, qt: (b, hp, 0, 0, qt)),
            pl.BlockSpec((None, None, FOX_PAIR, t, LANES), lambda b, hp, qt: (b, fox_pair0 + hp, 0, 0, 0)),
            pl.BlockSpec((None, None, FOX_PAIR, VT_ROWS, t), lambda b, hp, qt: (b, hp, 0, 0, 0)),
            pl.BlockSpec((t, FOX_HEADS), lambda b, hp, qt: (b, 0)),
            pl.BlockSpec(tril.shape, lambda b, hp, qt: (0, 0)),
        ],
        out_specs=pl.BlockSpec((None, tq, FOX_PAIR * HEAD_DIM), lambda b, hp, qt: (b, qt, hp)),
        out_shape=jax.ShapeDtypeStruct((nb, t, FOX_HEADS * HEAD_DIM), BF16),
        scratch_shapes=[
            pltpu.VMEM((3, t, FOX_HEADS), BF16),
            pltpu.VMEM((FOX_PAIR, t, LANES), BF16),
            pltpu.VMEM((FOX_PAIR, 1, tq), F32), pltpu.VMEM((FOX_PAIR, VT_ROWS, tq), F32),
        ],
        compiler_params=_cparams(("parallel", "arbitrary", "arbitrary")),
        name="fox_prompt",
    )(q5, k5, v5, logf, tril)


def _merge_ln_kernel(oa_ref, ob_ref, gm_ref, h_ref, wa_ref, wb_ref, wo_ref, g_ref, b_ref, o_ref, *, alpha):
    pa = _dot(oa_ref[...], wa_ref[...])
    pb = _dot(ob_ref[...], wb_ref[...])
    merged = gm_ref[:, :D_MODEL] * pa + gm_ref[:, D_MODEL:] * pb
    y = alpha * h_ref[...] + _dot(merged.astype(BF16), wo_ref[...])
    mu = jnp.mean(y, axis=-1, keepdims=True)
    d = y - mu
    var = jnp.mean(d * d, axis=-1, keepdims=True)
    o_ref[...] = d * lax.rsqrt(var + LN_EPS) * g_ref[...] + b_ref[...]


def _merge_ln(oa, ob, gm, h, wa, wb, wo, g, b, alpha, tm):
    n, d = h.shape
    row = lambda i: (i, 0)
    fixed = lambda i: (0, 0)
    return pl.pallas_call(
        functools.partial(_merge_ln_kernel, alpha=alpha),
        grid=(n // tm,),
        in_specs=[
            pl.BlockSpec((tm, 512), row), pl.BlockSpec((tm, 512), row),
            pl.BlockSpec((tm, 2 * d), row), pl.BlockSpec((tm, d), row),
            pl.BlockSpec((512, d), fixed), pl.BlockSpec((512, d), fixed), pl.BlockSpec((d, d), fixed),
            pl.BlockSpec((1, d), fixed), pl.BlockSpec((1, d), fixed),
        ],
        out_specs=pl.BlockSpec((tm, d), row),
        out_shape=jax.ShapeDtypeStruct((n, d), F32),
        compiler_params=_cparams(("parallel",)),
        name="merge_ln",
    )(oa, ob, gm, h, wa, wb, wo, g, b)


SAMPLE_ROWS = 32


def _row_slopes(rows, width, head0=0):
    row = lax.broadcasted_iota(jnp.int32, (rows, width), 0)
    head = head0 + (row >> 2)
    slope = lax.bitcast_convert_type((126 - head) << 23, F32)
    return slope, row & 3


def _sample_select_kernel(q_ref, kc_ref, vc_ref, ovl_ref, ocmp_ref, sel_ref, *, q_off, n_cmp, n_slc, t_new):
    n_key = kc_ref.shape[1]
    hrows = SAMPLE_ROWS // NSA_KV_HEADS
    lane = lax.broadcasted_iota(jnp.int32, (hrows, n_key), 1)
    ps, os_ = [], []
    for g in range(NSA_KV_HEADS):
        slope, tok = _row_slopes(hrows, n_key, head0=g * NSA_GROUP)
        dist = (q_off + tok) - (lane * CMP_STRIDE + (CMP_BLOCK - 1))
        valid = (dist >= 0) & (lane < n_cmp)
        s = _dot_nt(q_ref[g * hrows:(g + 1) * hrows], kc_ref[g]) - slope * dist.astype(F32)
        s = jnp.where(valid, s, NEG)
        m = jnp.max(s, axis=1, keepdims=True)
        p = jnp.where(valid, jnp.exp(s - m), 0.0)
        den = jnp.sum(p, axis=1, keepdims=True)
        p = p / jnp.where(den > 0.0, den, 1.0)
        ps.append(p)
        os_.append(_dot(p.astype(BF16), vc_ref[g]))
    ocmp_ref[...] = jnp.concatenate(os_, axis=0)

    p_all = jnp.concatenate(ps, axis=0)
    rr = lax.broadcasted_iota(jnp.int32, (2 * 8, SAMPLE_ROWS), 0)
    cc = lax.broadcasted_iota(jnp.int32, (2 * 8, SAMPLE_ROWS), 1)
    gather = jnp.where(((rr >> 3) == (cc >> 4)) & ((rr & 7) == (cc & 3)), 1.0, 0.0).astype(BF16)
    hi, mid, lo = _split3(p_all)
    imp = _dot(gather, hi) + _dot(gather, mid) + _dot(gather, lo)
    imp_slc = _dot3(imp, ovl_ref[...])
    nb_pad = imp_slc.shape[1]
    blk = lax.broadcasted_iota(jnp.int32, (16, nb_pad), 1)
    blk_f = blk.astype(F32)
    tpos = q_off + (lax.broadcasted_iota(jnp.int32, (16, nb_pad), 0) & 7)
    cur = tpos >> 6
    forced = (blk == 0) | (blk == cur) | (blk == cur - 1)
    score = jnp.where(forced, FORCE, jnp.where(blk * SEL_BLOCK <= tpos, imp_slc, -FORCE))
    score = jnp.where(blk < n_slc, score, -3.0 * FORCE)
    sel = jnp.zeros((16, nb_pad), F32)
    for _ in range(min(N_SEL, n_slc)):
        mx = jnp.max(score, axis=1, keepdims=True)
        first = jnp.min(jnp.where(score == mx, blk_f, float(nb_pad)), axis=1, keepdims=True)
        hit = blk_f == first
        sel = jnp.where(hit, 1.0, sel)
        score = jnp.where(hit, -4.0 * FORCE, score)
    sel_ref[...] = sel


def _sample_select(q32, kcvc, ovl, q_off, n_cmp, n_slc, t_new):
    nb = q32.shape[0]
    n_key = kcvc.shape[2]
    nb_pad = ovl.shape[1]
    return pl.pallas_call(
        functools.partial(_sample_select_kernel, q_off=q_off, n_cmp=n_cmp, n_slc=n_slc, t_new=t_new),
        grid=(nb,),
        in_specs=[
            pl.BlockSpec((None, SAMPLE_ROWS, LANES), lambda b: (b, 0, 0)),
            pl.BlockSpec((None, None, 2, n_key, LANES), lambda b: (b, 0, 0, 0, 0)),
            pl.BlockSpec((None, None, 2, n_key, LANES), lambda b: (b, 1, 0, 0, 0)),
            pl.BlockSpec((n_key, nb_pad), lambda b: (0, 0)),
        ],
        out_specs=[
            pl.BlockSpec((None, SAMPLE_ROWS, LANES), lambda b: (b, 0, 0)),
            pl.BlockSpec((None, 16, nb_pad), lambda b: (b, 0, 0)),
        ],
        out_shape=[
            jax.ShapeDtypeStruct((nb, SAMPLE_ROWS, LANES), F32),
            jax.ShapeDtypeStruct((nb, 16, nb_pad), F32),
        ],
        compiler_params=_cparams(("parallel",)),
        name="sample_select",
    )(q32, kcvc.reshape(nb, 2, 2, n_key, LANES), kcvc.reshape(nb, 2, 2, n_key, LANES), ovl)


TAIL = 128


def _softmax_update_pages(s, vts, m_ref, l_ref, acc_ref):
    m_prev = m_ref[...]
    m_new = jnp.maximum(m_prev, jnp.max(s, axis=1, keepdims=True))
    a = jnp.exp(m_prev - m_new)
    p = jnp.exp(s - m_new)
    l_ref[...] = a * l_ref[...] + jnp.sum(p, axis=1, keepdims=True)
    pb = p.astype(BF16)
    pv = None
    for j, vt in enumerate(vts):
        d = _dot_nt(pb[:, j * PAGE:(j + 1) * PAGE], vt)
        pv = d if pv is None else pv + d
    acc_ref[...] = a * acc_ref[...] + pv
    m_ref[...] = m_new


def _tail_scores(q, knew_t, t_new):
    s = _dot(q, knew_t)
    slope, tok = _row_slopes(SAMPLE_ROWS, TAIL)
    j = lax.broadcasted_iota(jnp.int32, s.shape, 1)
    d = tok - j
    s = s - slope * d.astype(F32)
    return jnp.where((d >= 0) & (j < t_new), s, NEG)


def _sample_nsa_kernel(pt_ref, *refs, n_pages, n_steps, q_off, t_new):
    k_refs = refs[:n_pages]
    v_refs = refs[n_pages:2 * n_pages]
    (q_ref, selx_ref, exp_ref, knew_ref, vnew_ref, wk_ref, wv_ref, wknew_ref, wvnew_ref, ocmp_ref, gate_ref,
     o_ref, m_ref, l_ref, acc_ref) = refs[2 * n_pages:]
    pg = pl.program_id(1)
    n_key = n_pages * PAGE
    q = q_ref[...]
    slope, tok = _row_slopes(SAMPLE_ROWS, n_key)

    @pl.when(pg == 0)
    def _():
        m_ref[...] = jnp.full_like(m_ref, NEG)
        l_ref[...] = jnp.zeros_like(l_ref)
        acc_ref[...] = jnp.zeros_like(acc_ref)

    s = jnp.concatenate([_dot(q, r[...].astype(BF16)) for r in k_refs], axis=1)
    kpos = pg * n_key + lax.broadcasted_iota(jnp.int32, (SAMPLE_ROWS, n_key), 1)
    dist = (q_off + tok) - kpos
    s = s - slope * dist.astype(F32)
    chosen = _dot(selx_ref[...], exp_ref[...]) > 0.5
    s = jnp.where(chosen & (dist >= 0), s, NEG)
    _softmax_update_pages(s, [r[...].astype(BF16) for r in v_refs], m_ref, l_ref, acc_ref)

    @pl.when(pg == n_steps - 1)
    def _():
        lane = lax.broadcasted_iota(jnp.int32, (SAMPLE_ROWS, LANES), 1)
        row = lax.broadcasted_iota(jnp.int32, (SAMPLE_ROWS, LANES), 0)

        def own_group(x):
            return jnp.where(lane < HEAD_DIM, jnp.where(row < 16, x, pltpu.roll(x, HEAD_DIM, axis=1)), 0.0)

        st = _tail_scores(q, knew_ref[...], t_new)
        _softmax_update(st, vnew_ref[...], m_ref, l_ref, acc_ref)
        o_sel = own_group(acc_ref[...] / l_ref[...])

        sw = _dot(q, wk_ref[...].astype(BF16))
        wslope, wtok = _row_slopes(SAMPLE_ROWS, WINDOW)
        wpos = (q_off - WINDOW) + lax.broadcasted_iota(jnp.int32, (SAMPLE_ROWS, WINDOW), 1)
        wd = (q_off + wtok) - wpos
        sw = sw - wslope * wd.astype(F32)
        sw = jnp.where((wd >= 0) & (wd < WINDOW) & (wpos >= 0), sw, NEG)
        swt = _tail_scores(q, wknew_ref[...], t_new)
        mw = jnp.maximum(jnp.max(sw, axis=1, keepdims=True), jnp.max(swt, axis=1, keepdims=True))
        pw = jnp.exp(sw - mw)
        pwt = jnp.exp(swt - mw)
        lw = jnp.sum(pw, axis=1, keepdims=True) + jnp.sum(pwt, axis=1, keepdims=True)
        ow = _dot_nt(pw.astype(BF16), wv_ref[...].astype(BF16)) + _dot(pwt.astype(BF16), wvnew_ref[...])
        o_win = own_group(ow / lw)

        gate = gate_ref[...]
        o_ref[...] = gate[:, 0:1] * ocmp_ref[...] + gate[:, 1:2] * o_sel + gate[:, 2:3] * o_win


def _sample_nsa(pool_t, page_table, q32, selx, expand, knew_t, vnew, wbuf_t, wknew_t, wvnew, ocmp, gate32,
                n_pages, q_off, t_new):
    nb, pages_per_seq = page_table.shape
    n_steps = pages_per_seq // n_pages
    page_specs = [
        pl.BlockSpec((None, None, LANES, PAGE), lambda b, pg, pt, j=j, c=c: (pt[b, pg * n_pages + j], c, 0, 0))
        for c in (2, 3) for j in range(n_pages)
    ]
    per_b = lambda shape: pl.BlockSpec((None,) + shape, lambda b, pg, pt: (b,) + (0,) * len(shape))
    win_spec = lambda c: pl.BlockSpec((None, None, LANES, WINDOW), lambda b, pg, pt: (b, c, 0, 0))
    grid_spec = pltpu.PrefetchScalarGridSpec(
        num_scalar_prefetch=1,
        grid=(nb, n_steps),
        in_specs=page_specs + [
            per_b((SAMPLE_ROWS, LANES)),
            pl.BlockSpec((None, None, SAMPLE_ROWS, LANES), lambda b, pg, pt: (b, pg, 0, 0)),
            pl.BlockSpec((LANES, n_pages * PAGE), lambda b, pg, pt: (0, 0)),
            per_b((LANES, TAIL)), per_b((TAIL, LANES)),
            win_spec(0), win_spec(1),
            per_b((LANES, TAIL)), per_b((TAIL, LANES)),
            per_b((SAMPLE_ROWS, LANES)), per_b((SAMPLE_ROWS, LANES)),
        ],
        out_specs=per_b((SAMPLE_ROWS, LANES)),
        scratch_shapes=[
            pltpu.VMEM((SAMPLE_ROWS, 1), F32), pltpu.VMEM((SAMPLE_ROWS, 1), F32),
            pltpu.VMEM((SAMPLE_ROWS, LANES), F32),
        ],
    )
    return pl.pallas_call(
        functools.partial(_sample_nsa_kernel, n_pages=n_pages, n_steps=n_steps, q_off=q_off, t_new=t_new),
        grid_spec=grid_spec,
        out_shape=jax.ShapeDtypeStruct((nb, SAMPLE_ROWS, LANES), F32),
        compiler_params=_cparams(("parallel", "arbitrary")),
        name="sample_nsa",
    )(page_table, *([pool_t] * (2 * n_pages)), q32, selx, expand, knew_t, vnew, wbuf_t, wbuf_t, wknew_t, wvnew,
      ocmp, gate32)


def _sample_fox_kernel(pt_ref, *refs, n_pages, n_steps, t_new):
    k_refs = refs[:n_pages]
    v_refs = refs[n_pages:2 * n_pages]
    lf_refs = refs[2 * n_pages:3 * n_pages]
    (q_ref, triu_ref, knew_ref, vnew_ref, lfnew_ref,
     o_ref, m_ref, l_ref, acc_ref, carry_ref) = refs[3 * n_pages:]
    pg = pl.program_id(1)
    q = q_ref[...]

    @pl.when(pg == 0)
    def _():
        m_ref[...] = jnp.full_like(m_ref, NEG)
        l_ref[...] = jnp.zeros_like(l_ref)
        acc_ref[...] = jnp.zeros_like(acc_ref)
        carry_ref[...] = jnp.zeros_like(carry_ref)

    triu = triu_ref[...]

    def per_token(c):
        return jnp.concatenate([c] * (SAMPLE_ROWS // FOX_HEADS), axis=0)

    lf_all = jnp.concatenate([r[...] for r in lf_refs], axis=0)
    prefix = _dot3(lf_all, triu)
    totals = jnp.sum(lf_all, axis=1, keepdims=True)
    carry = carry_ref[...]
    scores = []
    for j in range(n_pages):
        rows = slice(j * FOX_HEADS, (j + 1) * FOX_HEADS)
        bias = per_token(prefix[rows] + carry)
        carry = carry + totals[rows]
        scores.append(_dot(q, k_refs[j][...].astype(BF16)) - bias)
    carry_ref[...] = carry
    _softmax_update_pages(jnp.concatenate(scores, axis=1), [r[...].astype(BF16) for r in v_refs],
                          m_ref, l_ref, acc_ref)

    @pl.when(pg == n_steps - 1)
    def _():
        bias_n = per_token(_dot3(lfnew_ref[...], triu) + carry)
        sn = _dot(q, knew_ref[...]) - bias_n
        j = lax.broadcasted_iota(jnp.int32, sn.shape, 1)
        tok = lax.broadcasted_iota(jnp.int32, sn.shape, 0) >> 3
        sn = jnp.where((j <= tok) & (j < t_new), sn, NEG)
        _softmax_update(sn, vnew_ref[...], m_ref, l_ref, acc_ref)
        o_ref[...] = acc_ref[...] / l_ref[...]


def _sample_fox(pool_t, lf_t, page_table, qbd, triu, knew_t, vnew, lfnew_t, n_pages, t_new):
    nb, pages_per_seq = page_table.shape
    n_steps = pages_per_seq // n_pages
    width = FOX_HEADS * HEAD_DIM
    kv_specs = [
        pl.BlockSpec((None, None, width, PAGE), lambda b, pg, pt, j=j, c=c: (pt[b, pg * n_pages + j], c, 0, 0))
        for c in range(2) for j in range(n_pages)
    ]
    lf_specs = [
        pl.BlockSpec((None, FOX_HEADS, PAGE), lambda b, pg, pt, j=j: (pt[b, pg * n_pages + j], 0, 0))
        for j in range(n_pages)
    ]
    per_b = lambda shape: pl.BlockSpec((None,) + shape, lambda b, pg, pt: (b,) + (0,) * len(shape))
    grid_spec = pltpu.PrefetchScalarGridSpec(
        num_scalar_prefetch=1,
        grid=(nb, n_steps),
        in_specs=kv_specs + lf_specs + [
            per_b((SAMPLE_ROWS, width)),
            pl.BlockSpec((PAGE, PAGE), lambda b, pg, pt: (0, 0)),
            per_b((width, TAIL)), per_b((TAIL, width)), per_b((FOX_HEADS, TAIL)),
        ],
        out_specs=per_b((SAMPLE_ROWS, width)),
        scratch_shapes=[
            pltpu.VMEM((SAMPLE_ROWS, 1), F32), pltpu.VMEM((SAMPLE_ROWS, 1), F32),
            pltpu.VMEM((SAMPLE_ROWS, width), F32), pltpu.VMEM((FOX_HEADS, 1), F32),
        ],
    )
    return pl.pallas_call(
        functools.partial(_sample_fox_kernel, n_pages=n_pages, n_steps=n_steps, t_new=t_new),
        grid_spec=grid_spec,
        out_shape=jax.ShapeDtypeStruct((nb, SAMPLE_ROWS, width), F32),
        compiler_params=_cparams(("parallel", "arbitrary")),
        name="sample_fox",
    )(page_table, *([pool_t] * (2 * n_pages)), *([lf_t] * n_pages), qbd, triu, knew_t, vnew, lfnew_t)


def _alibi_key_cols(pos):
    tab = np.zeros((len(pos), LANES), np.float32)
    tab[:, ALIBI0] = 1.0
    tab[:, ALIBI0 + 1] = 1.0
    tab[:, ALIBI0 + 2] = (pos // 128) * 128
    tab[:, ALIBI0 + 3] = pos % 128
    return tab


def _prompt_consts(t):
    pos = np.arange(t)
    kaug_win = _alibi_key_cols(pos)
    kaug_sel = kaug_win.copy()
    kaug_sel[pos, SELBIT0 + pos // SEL_BLOCK] = -MASK_BIG
    n_chunk = t // CMP_STRIDE
    kaug_cmp = _alibi_key_cols(np.arange(n_chunk) * CMP_STRIDE + CMP_BLOCK - 1)
    n_cmp = (t - CMP_BLOCK) // CMP_STRIDE + 1
    n_slc = -(-t // SEL_BLOCK)
    ovlt = np.zeros((n_slc, n_chunk), np.float32)
    cs = np.arange(n_cmp)[None, :] * CMP_STRIDE
    ss = np.arange(n_slc)[:, None] * SEL_BLOCK
    ovlt[:, :n_cmp] = (cs < ss + SEL_BLOCK) & (cs + CMP_BLOCK > ss)
    tril = np.tril(np.ones((512, 512), np.float32))
    as_bf = lambda a: jnp.asarray(a, BF16)
    return dict(kaug_sel=as_bf(kaug_sel), kaug_win=as_bf(kaug_win), kaug_cmp=as_bf(kaug_cmp),
                ovlt=as_bf(ovlt), tril=as_bf(tril))


def _sample_ovl(n_key, n_cmp, n_slc, nb_pad):
    ovl = np.zeros((n_key, nb_pad), np.float32)
    cs = np.arange(n_cmp)[:, None] * CMP_STRIDE
    ss = np.arange(n_slc)[None, :] * SEL_BLOCK
    ovl[:n_cmp, :n_slc] = (cs < ss + SEL_BLOCK) & (cs + CMP_BLOCK > ss)
    return jnp.asarray(ovl, BF16)


def _perm_in_proj(w_in, b_in):
    o_qa, o_kva, o_ga, o_fox, o_fb, o_gm = 0, 512, 1280, 1304, 2840, 2848
    d = w_in.shape[0]

    def cols(x, zeros):
        return jnp.concatenate([x[..., o_gm:], x[..., o_fb:o_gm], zeros(LANES - FOX_HEADS)], axis=-1)

    def rows(x, zeros):
        return jnp.concatenate([x[..., o_qa:o_ga], x[..., o_fox:o_fb],
                                x[..., o_ga:o_fox], zeros(R_FB - R_GA - 3 * NSA_HEADS),
                                x[..., o_fb:o_gm], zeros(R_TOT - R_FB - FOX_HEADS)], axis=-1)

    w = cols(w_in, lambda k: jnp.zeros((d, k), w_in.dtype)).astype(BF16)
    b = cols(b_in, lambda k: jnp.zeros((k,), b_in.dtype)).reshape(1, C_TOT)
    wt = rows(w_in, lambda k: jnp.zeros((d, k), w_in.dtype)).T.astype(BF16)
    bt = rows(b_in, lambda k: jnp.zeros((k,), b_in.dtype)).reshape(R_TOT, 1)
    return w, b, wt, bt


def _compress_weights(pe_k, w1_k, w2_k, pe_v, w1_v, w2_v):
    n_sub = CMP_BLOCK // CMP_STRIDE

    def per_row(w1):
        w = w1.reshape(n_sub, CMP_STRIDE, HEAD_DIM, CMP_HIDDEN)
        return w.transpose(1, 2, 0, 3).reshape(CMP_STRIDE, HEAD_DIM, n_sub * CMP_HIDDEN)

    def per_kind(w1):
        blk = per_row(w1)
        zero = jnp.zeros_like(blk)
        w = jnp.stack([jnp.concatenate([blk, zero], axis=-1), jnp.concatenate([zero, blk], axis=-1)], axis=1)
        return w.reshape(CMP_STRIDE * NSA_KV_HEADS * HEAD_DIM, NSA_KV_HEADS * n_sub * CMP_HIDDEN)

    wr = jnp.stack([per_kind(w1_k), per_kind(w1_v)]).astype(BF16)
    pe8 = jnp.concatenate([pe_k.reshape(1, -1), pe_v.reshape(1, -1), jnp.zeros((14, CMP_BLOCK * HEAD_DIM), F32)], axis=0)
    w1cat = jnp.concatenate([w1_k, w1_v], axis=1).astype(BF16)
    pad = jnp.zeros((CMP_HIDDEN, LANES - HEAD_DIM), F32)
    w2pad = jnp.stack([jnp.concatenate([w2_k, pad], axis=1), jnp.concatenate([w2_v, pad], axis=1)]).astype(BF16)
    return dict(wr=wr, pe8=pe8.astype(BF16), w1cat=w1cat, w2pad=w2pad)


def _prompt_mixer(h1, w, nb, t):
    cmp_rows, logf, kh, gm, nsat, wint, foxt, logft, qat, nvt, fqt, fvt, gat = _in_proj(
        h1, w["w_in"], w["b_in"], w["wt"], w["bt"], nb, t, tm=TM_PROJ)
    consts = _prompt_consts(t)
    kcvc, vct = _compress_prompt(cmp_rows, w["cmp"], nb, t)
    o_a = _nsa_prompt(qat, kh, nvt, kcvc, vct, gat, consts, nb, t, tq=NSA_TQ, tk=NSA_TK)
    o_b = _fox_prompt(fqt, kh, fvt, logf, consts["tril"], nb, t, tq=FOX_TQ, tk=FOX_TK)
    n = nb * t
    state = (nsat.reshape(nb, 4, NSA_KV_HEADS, HEAD_DIM, t).transpose(0, 4, 1, 2, 3),
             wint.reshape(nb, 2, NSA_KV_HEADS, HEAD_DIM, t).transpose(0, 4, 1, 2, 3),
             foxt.reshape(nb, 2, FOX_HEADS, HEAD_DIM, t).transpose(0, 4, 1, 2, 3),
             logft.transpose(0, 2, 1))
    return o_a.reshape(n, -1), o_b.reshape(n, -1), gm, state


def _sample_mixer(h1, w, nb, t_new, nsa_pool, fox_pool, logf_pool, win_buf, page_table):
    n = nb * t_new
    past = page_table.shape[1] * PAGE
    _, _, _, gm, nsat, wint, foxt, logft, qat, _, fqt, _, gat = _in_proj(
        h1, w["w_in"], w["b_in"], w["wt"], w["bt"], 1, n, tm=n)
    nsa, win, fox, logf = nsat[0].T, wint[0].T, foxt[0].T, logft[0].T
    ga = gat[:3 * NSA_HEADS].T
    n_pool = nsa_pool.shape[0]
    nsa_t = nsa_pool.transpose(0, 2, 3, 4, 1).reshape(n_pool, 4, NSA_KV_HEADS * HEAD_DIM, PAGE)
    fox_t = fox_pool.transpose(0, 2, 3, 4, 1).reshape(n_pool, 2, FOX_HEADS * HEAD_DIM, PAGE)
    lf_t = logf_pool.transpose(0, 2, 1)
    wbuf_t = win_buf.transpose(0, 2, 3, 4, 1).reshape(nb, 2, NSA_KV_HEADS * HEAD_DIM, win_buf.shape[1])
    assert win_buf.shape[1] == WINDOW and t_new == 4 and past % SEL_BLOCK == 0

    seq_len = past + t_new
    n_cmp = (seq_len - CMP_BLOCK) // CMP_STRIDE + 1
    n_slc = -(-seq_len // SEL_BLOCK)
    n_chunk = past // CMP_STRIDE
    assert n_cmp <= n_chunk
    kcvc = _compress_sample(nsa_t, page_table, w["cmp"], n_pages=CMP_PAGES)

    q = qat[0].reshape(NSA_HEADS, HEAD_DIM, nb, t_new).transpose(2, 0, 3, 1).reshape(nb, SAMPLE_ROWS, HEAD_DIM)
    q = jnp.pad(q, ((0, 0), (0, 0), (0, LANES - HEAD_DIM)))
    nb_pad = -(-n_slc // LANES) * LANES
    ocmp, sel = _sample_select(q, kcvc, _sample_ovl(n_chunk, n_cmp, n_slc, nb_pad), past, n_cmp, n_slc, t_new)

    n_pages = NSA_PAGES
    n_steps = past // (n_pages * PAGE)
    blk_per_step = n_pages * PAGE // SEL_BLOCK
    selg = sel.reshape(nb, 2, 8, nb_pad)[:, :, :t_new, :n_steps * blk_per_step]
    selg = selg.reshape(nb, 2, 1, t_new, n_steps, blk_per_step)
    selx = jnp.broadcast_to(selg, (nb, 2, NSA_GROUP, t_new, n_steps, blk_per_step))
    selx = selx.transpose(0, 4, 1, 2, 3, 5).reshape(nb, n_steps, SAMPLE_ROWS, blk_per_step)
    selx = jnp.pad(selx, ((0, 0), (0, 0), (0, 0), (0, LANES - blk_per_step))).astype(BF16)
    expand = np.zeros((LANES, n_pages * PAGE), np.float32)
    expand[np.arange(n_pages * PAGE) // SEL_BLOCK, np.arange(n_pages * PAGE)] = 1.0
    q_sel = jnp.concatenate([q[:, :16], jnp.roll(q[:, 16:], HEAD_DIM, axis=-1)], axis=1)

    def new_rows(x, c0, width=LANES):
        r = x.reshape(nb, t_new, -1)[:, :, c0:c0 + width]
        return jnp.pad(r, ((0, 0), (0, TAIL - t_new), (0, 0))).astype(BF16)

    def new_cols(x, c0, width=LANES):
        return new_rows(x, c0, width).transpose(0, 2, 1)

    gate32 = ga.reshape(nb, t_new, 3, NSA_HEADS).transpose(0, 3, 1, 2).reshape(nb, SAMPLE_ROWS, 3)
    gate32 = jnp.pad(gate32, ((0, 0), (0, 0), (0, LANES - 3)))
    o_a32 = _sample_nsa(nsa_t, page_table, q_sel, selx, jnp.asarray(expand, BF16),
                        new_cols(nsa, 256), new_rows(nsa, 384), wbuf_t, new_cols(win, 0), new_rows(win, 128),
                        ocmp, gate32, n_pages, past, t_new)
    o_a = o_a32[:, :, :HEAD_DIM].reshape(nb, NSA_HEADS, t_new, HEAD_DIM).transpose(0, 2, 1, 3)
    o_a = o_a.reshape(n, NSA_HEADS * HEAD_DIM).astype(BF16)

    width = FOX_HEADS * HEAD_DIM
    qf = fqt[0].reshape(FOX_HEADS, HEAD_DIM, nb, t_new).transpose(2, 3, 0, 1)
    eye = jnp.eye(FOX_HEADS, dtype=qf.dtype)
    qbd = (qf[:, :, :, None, :] * eye[None, None, :, :, None]).reshape(nb, SAMPLE_ROWS, width)
    triu = jnp.asarray(np.triu(np.ones((PAGE, PAGE), np.float32)), BF16)
    lfnew_t = jnp.pad(logf.reshape(nb, t_new, FOX_HEADS), ((0, 0), (0, TAIL - t_new), (0, 0))).transpose(0, 2, 1)
    o_b32 = _sample_fox(fox_t, lf_t, page_table, qbd, triu, new_cols(fox, 0, width), new_rows(fox, width, width),
                        lfnew_t, FOX_PAGES, t_new)
    o_b = o_b32.reshape(nb, t_new, FOX_HEADS, FOX_HEADS, HEAD_DIM)
    o_b = jnp.einsum("bthgd,hg->bthd", o_b, jnp.eye(FOX_HEADS, dtype=o_b.dtype))
    o_b = o_b.reshape(n, width).astype(BF16)
    state = (nsa.reshape(nb, t_new, 4, NSA_KV_HEADS, HEAD_DIM), win.reshape(nb, t_new, 2, NSA_KV_HEADS, HEAD_DIM),
             fox.reshape(nb, t_new, 2, FOX_HEADS, HEAD_DIM), logf.reshape(nb, t_new, FOX_HEADS))
    return o_a, o_b, gm, state


def _layer(x, w, alpha, mixer, tm):
    h1 = _ffn_ln(x, w["ffn1_wg"], w["ffn1_wu"], w["ffn1_wd"], w["ln1_g"], w["ln1_b"], alpha, tm)
    o_a, o_b, gm, state = mixer(h1, w)
    h2 = _merge_ln(o_a, o_b, gm, h1, w["w_proj_a"], w["w_proj_b"], w["w_out"], w["ln2_g"], w["ln2_b"], alpha, tm)
    y = _ffn_ln(h2, w["ffn2_wg"], w["ffn2_wu"], w["ffn2_wd"], w["ln3_g"], w["ln3_b"], alpha, tm)
    return y, state


def kernel(x_prompt, x_sample, cache_nsa_kv, cache_fox_kv, cache_fox_logf, state_win_kv, page_table,
           ln1_g, ln1_b, ffn1_w_gate, ffn1_w_up, ffn1_w_down, w_in, b_in,
           cmp_pe_k, cmp_w1_k, cmp_w2_k, cmp_pe_v, cmp_w1_v, cmp_w2_v,
           w_proj_a, w_proj_b, w_out, ln2_g, ln2_b,
           ffn2_w_gate, ffn2_w_up, ffn2_w_down, ln3_g, ln3_b):
    depth = w_in.shape[0]
    nb_p, t_p, d = x_prompt.shape
    nb_s, t_s, _ = x_sample.shape
    alpha = float((2.0 * depth) ** 0.25)
    h_p = x_prompt.reshape(nb_p * t_p, d)
    h_s = x_sample.reshape(nb_s * t_s, d)
    st_p, st_s = [], []
    for l in range(depth):
        wi, bi, wt, bt = _perm_in_proj(w_in[l], b_in[l])
        vec = lambda a: a[l].reshape(1, d)
        w = dict(
            ln1_g=vec(ln1_g), ln1_b=vec(ln1_b), ln2_g=vec(ln2_g), ln2_b=vec(ln2_b), ln3_g=vec(ln3_g), ln3_b=vec(ln3_b),
            ffn1_wg=ffn1_w_gate[l].astype(BF16), ffn1_wu=ffn1_w_up[l].astype(BF16), ffn1_wd=ffn1_w_down[l].astype(BF16),
            ffn2_wg=ffn2_w_gate[l].astype(BF16), ffn2_wu=ffn2_w_up[l].astype(BF16), ffn2_wd=ffn2_w_down[l].astype(BF16),
            w_in=wi, b_in=bi, wt=wt, bt=bt,
            cmp=_compress_weights(cmp_pe_k[l], cmp_w1_k[l], cmp_w2_k[l], cmp_pe_v[l], cmp_w1_v[l], cmp_w2_v[l]),
            w_proj_a=w_proj_a[l].astype(BF16), w_proj_b=w_proj_b[l].astype(BF16), w_out=w_out[l].astype(BF16),
        )
        h_p, s_p = _layer(h_p, w, alpha, functools.partial(_prompt_mixer, nb=nb_p, t=t_p), tm=TM_FFN)
        h_s, s_s = _layer(h_s, w, alpha, functools.partial(
            _sample_mixer, nb=nb_s, t_new=t_s, nsa_pool=cache_nsa_kv[l], fox_pool=cache_fox_kv[l],
            logf_pool=cache_fox_logf[l], win_buf=state_win_kv[l], page_table=page_table), tm=nb_s * t_s)
        st_p.append(s_p)
        st_s.append(s_s)

    def states(st, nb, t, win_prev):
        nsa = jnp.stack([s[0] for s in st])
        fox = jnp.stack([s[2] for s in st])
        logf = jnp.stack([s[3] for s in st])
        wins = []
        for l, s in enumerate(st):
            wr = s[1]
            if win_prev is None:
                wins.append(wr[:, t - min(WINDOW, t):])
            else:
                wins.append(jnp.concatenate([win_prev[l], wr], axis=1)[:, t:])
        return nsa, fox, logf, jnp.stack(wins)

    nsa_p, fox_p, logf_p, win_p = states(st_p, nb_p, t_p, None)
    nsa_s, fox_s, logf_s, win_s = states(st_s, nb_s, t_s, state_win_kv)
    return (h_p.reshape(nb_p, t_p, d), h_s.reshape(nb_s, t_s, d), nsa_p, fox_p, logf_p, win_p,
            nsa_s, fox_s, logf_s, win_s)
```

```python
import functools

import numpy as np
import jax
import jax.numpy as jnp
from jax import lax
from jax.experimental import pallas as pl
from jax.experimental.pallas import tpu as pltpu

F32 = jnp.float32
BF16 = jnp.bfloat16

LANES = 128
VMEM_LIMIT = 56 * 1024 * 1024

TM_FFN = 256
FFN_CHUNKS = 1
TM_PROJ = 256
NSA_TQ, NSA_TK = 512, 512
FOX_TQ, FOX_TK = 1024, 512
CMP_PAGES = 32
NSA_PAGES = 32
FOX_PAGES = 16

D_MODEL = 1024
HEAD_DIM = 64
NSA_HEADS = 8
NSA_KV_HEADS = 2
NSA_GROUP = NSA_HEADS // NSA_KV_HEADS
FOX_HEADS = 8
CMP_BLOCK = 32
CMP_STRIDE = 16
CMP_HIDDEN = 2 * HEAD_DIM
SEL_BLOCK = 64
N_SEL = 16
WINDOW = 512
PAGE = 128
LN_EPS = 1e-5
NEG = -1e30
FORCE = 1e9
SCALE = HEAD_DIM ** -0.5
MASK_BIG = 30000.0

SELBIT0 = 64
ALIBI0 = 96


def _dot(a, b):
    return jnp.dot(a, b, preferred_element_type=F32)


def _dot_nt(a, b):
    return lax.dot_general(a, b, (((1,), (1,)), ((), ())), preferred_element_type=F32)


def _split3(x):
    hi = x.astype(BF16)
    r1 = x - hi.astype(F32)
    mid = r1.astype(BF16)
    lo = (r1 - mid.astype(F32)).astype(BF16)
    return hi, mid, lo


def _dot3(x, w):
    hi, mid, lo = _split3(x)
    return _dot(hi, w) + _dot(mid, w) + _dot(lo, w)


def _dot3_nt(a, x):
    hi, mid, lo = _split3(x)
    return _dot_nt(a, hi) + _dot_nt(a, mid) + _dot_nt(a, lo)


def _sigmoid(x):
    return 1.0 / (1.0 + jnp.exp(-x))


def _log_sigmoid(x):
    return -(jnp.maximum(-x, 0.0) + jnp.log(1.0 + jnp.exp(-jnp.abs(x))))


def _gelu_tanh(x):
    c = np.float32(np.sqrt(2.0 / np.pi))
    return 0.5 * x * (1.0 + jnp.tanh(c * (x + np.float32(0.044715) * (x * x * x))))


def _cparams(sem):
    return pltpu.CompilerParams(dimension_semantics=sem, vmem_limit_bytes=VMEM_LIMIT)


def _ffn_ln_kernel(x_ref, wg_ref, wu_ref, wd_ref, g_ref, b_ref, o_ref, acc_ref, *, alpha, n_ff):
    j = pl.program_id(1)

    @pl.when(j == 0)
    def _():
        acc_ref[...] = jnp.zeros_like(acc_ref)

    xb = x_ref[...].astype(BF16)
    gate = _dot(xb, wg_ref[...])
    up = _dot(xb, wu_ref[...])
    mid = (gate * _sigmoid(gate) * up).astype(BF16)
    acc_ref[...] += _dot(mid, wd_ref[...])

    @pl.when(j == n_ff - 1)
    def _():
        y = alpha * x_ref[...] + 0.5 * acc_ref[...]
        mu = jnp.mean(y, axis=-1, keepdims=True)
        d = y - mu
        var = jnp.mean(d * d, axis=-1, keepdims=True)
        o_ref[...] = d * lax.rsqrt(var + LN_EPS) * g_ref[...] + b_ref[...]


def _ffn_ln(x, wg, wu, wd, g, b, alpha, tm):
    n, d = x.shape
    dff = wg.shape[1]
    tf = dff // FFN_CHUNKS
    n_ff = dff // tf
    return pl.pallas_call(
        functools.partial(_ffn_ln_kernel, alpha=alpha, n_ff=n_ff),
        grid=(n // tm, n_ff),
        in_specs=[
            pl.BlockSpec((tm, d), lambda i, j: (i, 0)),
            pl.BlockSpec((d, tf), lambda i, j: (0, j)),
            pl.BlockSpec((d, tf), lambda i, j: (0, j)),
            pl.BlockSpec((tf, d), lambda i, j: (j, 0)),
            pl.BlockSpec((1, d), lambda i, j: (0, 0)),
            pl.BlockSpec((1, d), lambda i, j: (0, 0)),
        ],
        out_specs=pl.BlockSpec((tm, d), lambda i, j: (i, 0)),
        out_shape=jax.ShapeDtypeStruct((n, d), F32),
        scratch_shapes=[pltpu.VMEM((tm, d), F32)],
        compiler_params=_cparams(("parallel", "arbitrary")),
        name="ffn_ln",
    )(x, wg, wu, wd, g, b)


C_GM = 0
C_FB = 2048
C_TOT = 2176
R_QA = 0
R_KVA = 512
R_FQ = 1280
R_FKV = 1792
R_GA = 2816
R_FB = 2848
R_TOT = 2864
N_KH = 12
VT_ROWS = 80


def _head_pad(z, h):
    col = z[:, LANES * (h // 2):LANES * (h // 2 + 1)]
    if h % 2:
        col = pltpu.roll(col, HEAD_DIM, axis=1)
    lane = lax.broadcasted_iota(jnp.int32, col.shape, 1)
    return jnp.where(lane < HEAD_DIM, col, 0.0)


def _ones_row_block(width):
    sub = lax.broadcasted_iota(jnp.int32, (VT_ROWS - HEAD_DIM, width), 0)
    return jnp.where(sub == 0, 1.0, 0.0)


def _in_proj_kernel(h_ref, w_ref, b_ref, wt_ref, bt_ref,
                    cmp_ref, logf_ref, kh_ref, gm_ref,
                    nsat_ref, wint_ref, foxt_ref, logft_ref,
                    qat_ref, nvt_ref, fqt_ref, fvt_ref, gat_ref):
    hb = h_ref[...].astype(BF16)
    tm = hb.shape[0]

    def proj(c0, c1):
        return _dot(hb, w_ref[:, c0:c1]) + b_ref[:, c0:c1]

    def proj_t(r0, r1):
        return _dot_nt(wt_ref[r0:r1, :], hb) + bt_ref[r0:r1, :]

    gm_ref[...] = _sigmoid(proj(C_GM, C_FB))
    logf_ref[...] = _log_sigmoid(proj(C_FB, C_TOT))[:, :FOX_HEADS]

    ones_blk = _ones_row_block(tm)

    def value_head(zt, r0):
        return jnp.concatenate([zt[r0:r0 + HEAD_DIM], ones_blk], axis=0).astype(BF16)

    qat_ref[...] = (proj_t(R_QA, R_KVA) * SCALE).astype(BF16).reshape(qat_ref.shape)

    zkv = proj_t(R_KVA, R_FQ)
    nsat_ref[...] = zkv[:512]
    wint_ref[...] = zkv[512:]
    for j, head in enumerate((6, 7, 10, 11)):
        nvt_ref[j] = value_head(zkv, head * HEAD_DIM)

    fqt_ref[...] = (proj_t(R_FQ, R_FKV) * SCALE).astype(BF16).reshape(fqt_ref.shape)
    zf = proj_t(R_FKV, R_GA)
    foxt_ref[...] = zf
    for h in range(FOX_HEADS):
        fvt_ref[h] = value_head(zf, (FOX_HEADS + h) * HEAD_DIM)

    cmp_ref[...] = zkv[:4 * HEAD_DIM].T
    keys = jnp.concatenate([zkv[4 * HEAD_DIM:6 * HEAD_DIM], zkv[8 * HEAD_DIM:10 * HEAD_DIM],
                            zf[:FOX_HEADS * HEAD_DIM]], axis=0).T
    for h in range(N_KH):
        kh_ref[h] = _head_pad(keys, h).astype(BF16)

    gat_ref[...] = _sigmoid(proj_t(R_GA, R_FB))
    logft_ref[...] = _log_sigmoid(proj_t(R_FB, R_TOT))[:FOX_HEADS]


def _in_proj(h, w, b, wt, bt, nseq, tseq, tm):
    n, d = h.shape
    tps = tseq // tm
    row = lambda i: (i, 0)
    headmaj = lambda i: (i // tps, 0, i % tps, 0)
    featmaj = lambda i: (i // tps, 0, 0, i % tps)
    feat3 = lambda i: (i // tps, 0, i % tps)
    n_ga = R_FB - R_GA
    out_shape = [
        jax.ShapeDtypeStruct((n, 256), F32),
        jax.ShapeDtypeStruct((n, FOX_HEADS), F32),
        jax.ShapeDtypeStruct((nseq, N_KH, tseq, LANES), BF16),
        jax.ShapeDtypeStruct((n, 2 * D_MODEL), F32),
        jax.ShapeDtypeStruct((nseq, 512, tseq), F32),
        jax.ShapeDtypeStruct((nseq, 256, tseq), F32),
        jax.ShapeDtypeStruct((nseq, 1024, tseq), F32),
        jax.ShapeDtypeStruct((nseq, FOX_HEADS, tseq), F32),
        jax.ShapeDtypeStruct((nseq, NSA_HEADS, HEAD_DIM, tseq), BF16),
        jax.ShapeDtypeStruct((nseq, 4, VT_ROWS, tseq), BF16),
        jax.ShapeDtypeStruct((nseq, FOX_HEADS, HEAD_DIM, tseq), BF16),
        jax.ShapeDtypeStruct((nseq, FOX_HEADS, VT_ROWS, tseq), BF16),
        jax.ShapeDtypeStruct((n_ga, n), F32),
    ]
    out_specs = [
        pl.BlockSpec((tm, 256), row),
        pl.BlockSpec((tm, FOX_HEADS), row),
        pl.BlockSpec((None, N_KH, tm, LANES), headmaj),
        pl.BlockSpec((tm, 2 * D_MODEL), row),
        pl.BlockSpec((None, 512, tm), feat3),
        pl.BlockSpec((None, 256, tm), feat3),
        pl.BlockSpec((None, 1024, tm), feat3),
        pl.BlockSpec((None, FOX_HEADS, tm), feat3),
        pl.BlockSpec((None, NSA_HEADS, HEAD_DIM, tm), featmaj),
        pl.BlockSpec((None, 4, VT_ROWS, tm), featmaj),
        pl.BlockSpec((None, FOX_HEADS, HEAD_DIM, tm), featmaj),
        pl.BlockSpec((None, FOX_HEADS, VT_ROWS, tm), featmaj),
        pl.BlockSpec((n_ga, tm), lambda i: (0, i)),
    ]
    return pl.pallas_call(
        _in_proj_kernel,
        grid=(n // tm,),
        in_specs=[
            pl.BlockSpec((tm, d), row),
            pl.BlockSpec((d, C_TOT), lambda i: (0, 0)),
            pl.BlockSpec((1, C_TOT), lambda i: (0, 0)),
            pl.BlockSpec((R_TOT, d), lambda i: (0, 0)),
            pl.BlockSpec((R_TOT, 1), lambda i: (0, 0)),
        ],
        out_specs=out_specs,
        out_shape=out_shape,
        compiler_params=_cparams(("parallel",)),
        name="in_proj",
    )(h, w, b, wt, bt)


def _cmp_parts(k_ref, v_ref, wr_ref, n_rows):
    n_chunk = n_rows // CMP_STRIDE

    def kind(ref, w):
        x = jnp.concatenate([ref[pl.ds(r, n_chunk, stride=CMP_STRIDE), :] for r in range(CMP_STRIDE)], axis=1)
        return _dot(x.astype(BF16), w)

    return jnp.concatenate([kind(k_ref, wr_ref[0]), kind(v_ref, wr_ref[1])], axis=1)


def _cmp_finish(parts, bias8, w2_ref, out_ref, vt_ref=None):
    n = parts.shape[0]
    for s in range(4):
        kind = s // 2
        pa = parts[:, 256 * s:256 * s + CMP_HIDDEN]
        pb = parts[:, 256 * s + CMP_HIDDEN:256 * (s + 1)]
        pb = pltpu.roll(pb, n - 1, axis=0)
        bias = bias8[kind:kind + 1, CMP_HIDDEN * kind:CMP_HIDDEN * (kind + 1)]
        hid = bias + pa + pb
        c = _dot(_gelu_tanh(hid).astype(BF16), w2_ref[kind])
        out_ref[s] = c.astype(BF16)
        if vt_ref is not None and kind == 1:
            vt = jnp.concatenate([c.T[:HEAD_DIM], _ones_row_block(n)], axis=0)
            vt_ref[s - 2] = vt.astype(BF16)


def _cmp_bias(pe_ref, w1_ref):
    return _dot(pe_ref[...], w1_ref[...])


def _compress_prompt_kernel(k_ref, v_ref, wr_ref, pe_ref, w1_ref, w2_ref, out_ref, vt_ref, *, t):
    parts = _cmp_parts(k_ref, v_ref, wr_ref, t)
    _cmp_finish(parts, _cmp_bias(pe_ref, w1_ref), w2_ref, out_ref, vt_ref)


def _compress_prompt(nsa_rows, cw, nb, t):
    n_chunk = t // CMP_STRIDE
    return pl.pallas_call(
        functools.partial(_compress_prompt_kernel, t=t),
        grid=(nb,),
        in_specs=[
            pl.BlockSpec((t, LANES), lambda b: (b, 0)),
            pl.BlockSpec((t, LANES), lambda b: (b, 1)),
            pl.BlockSpec((2, CMP_STRIDE * LANES, 512), lambda b: (0, 0, 0)),
            pl.BlockSpec((16, 2048), lambda b: (0, 0)),
            pl.BlockSpec((2048, 256), lambda b: (0, 0)),
            pl.BlockSpec((2, CMP_HIDDEN, LANES), lambda b: (0, 0, 0)),
        ],
        out_specs=[
            pl.BlockSpec((None, 4, n_chunk, LANES), lambda b: (b, 0, 0, 0)),
            pl.BlockSpec((None, 2, VT_ROWS, n_chunk), lambda b: (b, 0, 0, 0)),
        ],
        out_shape=[
            jax.ShapeDtypeStruct((nb, 4, n_chunk, LANES), BF16),
            jax.ShapeDtypeStruct((nb, 2, VT_ROWS, n_chunk), BF16),
        ],
        compiler_params=_cparams(("parallel",)),
        name="compress_prompt",
    )(nsa_rows, nsa_rows, cw["wr"], cw["pe8"], cw["w1cat"], cw["w2pad"])


def _compress_sample_kernel(pt_ref, *refs, n_pages, n_steps):
    k_refs = refs[:n_pages]
    v_refs = refs[n_pages:2 * n_pages]
    wr_ref, pe_ref, w1_ref, w2_ref, out_ref, parts_ref, krows_ref, vrows_ref = refs[2 * n_pages:]
    pg = pl.program_id(1)
    rows = n_pages * (PAGE // CMP_STRIDE)
    for j in range(n_pages):
        krows_ref[j * PAGE:(j + 1) * PAGE, :] = k_refs[j][...].T
        vrows_ref[j * PAGE:(j + 1) * PAGE, :] = v_refs[j][...].T
    parts = _cmp_parts(krows_ref, vrows_ref, wr_ref, n_pages * PAGE)
    parts_ref[pl.ds(pl.multiple_of(pg * rows, rows), rows), :] = parts

    @pl.when(pg == n_steps - 1)
    def _():
        _cmp_finish(parts_ref[...], _cmp_bias(pe_ref, w1_ref), w2_ref, out_ref)


def _compress_sample(pool_t, page_table, cw, n_pages):
    nb, pages_per_seq = page_table.shape
    n_steps = pages_per_seq // n_pages
    n_chunk = pages_per_seq * (PAGE // CMP_STRIDE)
    page_specs = [
        pl.BlockSpec((None, None, LANES, PAGE), lambda b, pg, pt, j=j, c=c: (pt[b, pg * n_pages + j], c, 0, 0))
        for c in range(2) for j in range(n_pages)
    ]
    grid_spec = pltpu.PrefetchScalarGridSpec(
        num_scalar_prefetch=1,
        grid=(nb, n_steps),
        in_specs=page_specs + [
            pl.BlockSpec((2, CMP_STRIDE * LANES, 512), lambda b, pg, pt: (0, 0, 0)),
            pl.BlockSpec((16, 2048), lambda b, pg, pt: (0, 0)),
            pl.BlockSpec((2048, 256), lambda b, pg, pt: (0, 0)),
            pl.BlockSpec((2, CMP_HIDDEN, LANES), lambda b, pg, pt: (0, 0, 0)),
        ],
        out_specs=pl.BlockSpec((None, 4, n_chunk, LANES), lambda b, pg, pt: (b, 0, 0, 0)),
        scratch_shapes=[pltpu.VMEM((n_chunk, 1024), F32),
                        pltpu.VMEM((n_pages * PAGE, LANES), F32), pltpu.VMEM((n_pages * PAGE, LANES), F32)],
    )
    return pl.pallas_call(
        functools.partial(_compress_sample_kernel, n_pages=n_pages, n_steps=n_steps),
        grid_spec=grid_spec,
        out_shape=jax.ShapeDtypeStruct((nb, 4, n_chunk, LANES), BF16),
        compiler_params=_cparams(("parallel", "arbitrary")),
        name="compress_sample",
    )(page_table, *([pool_t] * (2 * n_pages)), cw["wr"], cw["pe8"], cw["w1cat"], cw["w2pad"])


def _softmax_update(s, v, m_ref, l_ref, acc_ref):
    m_prev = m_ref[...]
    m_new = jnp.maximum(m_prev, jnp.max(s, axis=1, keepdims=True))
    a = jnp.exp(m_prev - m_new)
    p = jnp.exp(s - m_new)
    l_ref[...] = a * l_ref[...] + jnp.sum(p, axis=1, keepdims=True)
    acc_ref[...] = a * acc_ref[...] + _dot(p.astype(BF16), v)
    m_ref[...] = m_new


def _softmax_update_t(s, vt, m_ref, acc_ref):
    m_prev = m_ref[...]
    m_new = jnp.maximum(m_prev, jnp.max(s, axis=0, keepdims=True))
    a = jnp.exp(m_prev - m_new)
    p = jnp.exp(s - m_new).astype(BF16)
    acc_ref[...] = a * acc_ref[...] + _dot(vt, p)
    m_ref[...] = m_new


def _softmax_reset(m_ref, acc_ref):
    m_ref[...] = jnp.full_like(m_ref, NEG)
    acc_ref[...] = jnp.zeros_like(acc_ref)


def _softmax_result(acc_ref):
    acc = acc_ref[...]
    return acc[:HEAD_DIM] / acc[HEAD_DIM:HEAD_DIM + 1]


def _nsa_prompt_kernel(q_ref, selk_ref, selvt_ref, wink_ref, winvt_ref, kc_ref, vct_ref, gat_ref,
                       kaug_sel_ref, kaug_win_ref, kaug_cmp_ref, ovlt_ref,
                       o_ref, ksel_s, kwin_s, m_ref, acc_ref, *, tq, tk, n_cmp, n_slc):
    g = pl.program_id(1)
    qt = pl.program_id(2)
    cols = NSA_GROUP * tq
    q0 = qt * tq

    @pl.when(qt == 0)
    def _():
        ksel_s[...] = selk_ref[...] + kaug_sel_ref[...]
        kwin_s[...] = wink_ref[...] + kaug_win_ref[...]

    qt4 = jnp.concatenate([q_ref[r] for r in range(NSA_GROUP)], axis=1)

    def head_and_pos(shape):
        lane = lax.broadcasted_iota(jnp.int32, shape, 1)
        return lane >> (tq.bit_length() - 1), q0 + (lane & (tq - 1))

    sub = lax.broadcasted_iota(jnp.int32, (16, cols), 0)
    r, t = head_and_pos((16, cols))
    sl0 = jnp.where(r == 0, 0.5, jnp.where(r == 1, 0.25, jnp.where(r == 2, 0.125, 0.0625))).astype(F32)
    slope = jnp.where(g == 0, sl0, sl0 * 0.0625)
    t_hi = ((t >> 7) << 7).astype(F32)
    t_lo = (t & 127).astype(F32)
    al = jnp.where(sub == 0, -slope * t_hi,
                   jnp.where(sub == 1, -slope * t_lo,
                             jnp.where((sub == 2) | (sub == 3), slope, 0.0))).astype(BF16)
    zeros16 = jnp.zeros((16, cols), BF16)

    def query_cols(selbits):
        return jnp.concatenate([qt4, selbits, al, zeros16], axis=0)

    qa = query_cols(jnp.zeros((n_slc, cols), BF16))

    n_chunk = kc_ref.shape[0]
    kc = kc_ref[...] + kaug_cmp_ref[...]
    s = _dot(kc, qa)
    nidx = lax.broadcasted_iota(jnp.int32, (n_chunk, cols), 0)
    _, t_c = head_and_pos((n_chunk, cols))
    valid = (t_c >= nidx * CMP_STRIDE + (CMP_BLOCK - 1)) & (nidx < n_cmp)
    s = jnp.where(valid, s, NEG)
    m = jnp.max(s, axis=0, keepdims=True)
    p = jnp.where(valid, jnp.exp(s - m), 0.0)
    den = jnp.sum(p, axis=0, keepdims=True)
    p = p / jnp.where(den > 0.0, den, 1.0)
    o_cmp = _dot(vct_ref[...], p.astype(BF16))[:HEAD_DIM]

    imp = p[:, 0:tq] + p[:, tq:2 * tq] + p[:, 2 * tq:3 * tq] + p[:, 3 * tq:4 * tq]
    hi, mid, lo = _split3(imp)
    ovlt = ovlt_ref[...]
    imp_slc = _dot(ovlt, hi) + _dot(ovlt, mid) + _dot(ovlt, lo)
    blk = lax.broadcasted_iota(jnp.int32, (n_slc, tq), 0)
    t1 = q0 + lax.broadcasted_iota(jnp.int32, (n_slc, tq), 1)
    cur = t1 >> 6
    forced = (blk == 0) | (blk == cur) | (blk == cur - 1)
    score = jnp.where(forced, FORCE, jnp.where(blk * SEL_BLOCK <= t1, imp_slc, -FORCE))
    rank = jnp.zeros((n_slc, tq), F32)
    for i in range(n_slc):
        row = score[i:i + 1, :]
        beats = (row > score) | ((row == score) & (blk > i))
        rank = rank + jnp.where(beats, 1.0, 0.0)
    notsel = jnp.where(rank >= float(N_SEL), 1.0, 0.0).astype(BF16)
    qs = query_cols(jnp.concatenate([notsel] * NSA_GROUP, axis=1))

    kpos_iota = lax.broadcasted_iota(jnp.int32, (tk, cols), 0)
    _, t_row = head_and_pos((1, cols))
    n_full = q0 // tk

    def branch(qx, k_s, vt_ref, kt_lo, window):
        _softmax_reset(m_ref, acc_ref)

        def tile(k0, causal):
            sc = _dot(k_s[pl.ds(k0, tk), :], qx)
            rel = t_row - k0
            if causal:
                sc = jnp.where(kpos_iota <= rel, sc, NEG)
            elif window:
                sc = jnp.where(kpos_iota > rel - WINDOW, sc, NEG)
            _softmax_update_t(sc, vt_ref[:, pl.ds(k0, tk)], m_ref, acc_ref)

        def body(kt, carry):
            tile(pl.multiple_of(kt * tk, tk), False)
            return carry

        lax.fori_loop(kt_lo, n_full, body, 0)
        for j in range(tq // tk):
            tile(pl.multiple_of(q0 + j * tk, tk), True)
        return _softmax_result(acc_ref)

    o_sel = branch(qs, ksel_s, selvt_ref, 0, False)
    o_win = branch(qa, kwin_s, winvt_ref, jnp.maximum(q0 - WINDOW, 0) // tk, True)

    outs = []
    for rr in range(NSA_GROUP):
        def gate(br):
            return gat_ref[pl.ds(br * NSA_HEADS + g * NSA_GROUP + rr, 1), :]
        sl = slice(rr * tq, (rr + 1) * tq)
        outs.append(gate(0) * o_cmp[:, sl] + gate(1) * o_sel[:, sl] + gate(2) * o_win[:, sl])
    o_ref[...] = jnp.concatenate(outs, axis=0).T.astype(BF16)


def _nsa_prompt(qat, kh, nvt, kcvc, vct, gat, consts, nb, t, tq, tk):
    n_cmp = (t - CMP_BLOCK) // CMP_STRIDE + 1
    n_slc = -(-t // SEL_BLOCK)
    n_chunk = t // CMP_STRIDE
    assert n_slc == 32 and tq % tk == 0 and tq <= WINDOW
    q5 = qat.reshape(nb, NSA_KV_HEADS, NSA_GROUP, HEAD_DIM, t)
    k_spec = lambda off: pl.BlockSpec((None, None, t, LANES), lambda b, g, qt: (b, off + g, 0, 0))
    vt_spec = lambda off: pl.BlockSpec((None, None, VT_ROWS, t), lambda b, g, qt: (b, off + g, 0, 0))
    const2 = lambda shape: pl.BlockSpec(shape, lambda b, g, qt: (0, 0))
    cols = NSA_GROUP * tq
    return pl.pallas_call(
        functools.partial(_nsa_prompt_kernel, tq=tq, tk=tk, n_cmp=n_cmp, n_slc=n_slc),
        grid=(nb, NSA_KV_HEADS, t // tq),
        in_specs=[
            pl.BlockSpec((None, None, NSA_GROUP, HEAD_DIM, tq), lambda b, g, qt: (b, g, 0, 0, qt)),
            k_spec(0), vt_spec(0), k_spec(2), vt_spec(2),
            pl.BlockSpec((None, None, n_chunk, LANES), lambda b, g, qt: (b, g, 0, 0)),
            pl.BlockSpec((None, None, VT_ROWS, n_chunk), lambda b, g, qt: (b, g, 0, 0)),
            pl.BlockSpec((R_FB - R_GA, tq), lambda b, g, qt: (0, b * (t // tq) + qt)),
            const2((t, LANES)), const2((t, LANES)), const2((n_chunk, LANES)), const2((n_slc, n_chunk)),
        ],
        out_specs=pl.BlockSpec((None, tq, NSA_GROUP * HEAD_DIM), lambda b, g, qt: (b, qt, g)),
        out_shape=jax.ShapeDtypeStruct((nb, t, NSA_HEADS * HEAD_DIM), BF16),
        scratch_shapes=[
            pltpu.VMEM((t, LANES), BF16), pltpu.VMEM((t, LANES), BF16),
            pltpu.VMEM((1, cols), F32), pltpu.VMEM((VT_ROWS, cols), F32),
        ],
        compiler_params=_cparams(("parallel", "parallel", "arbitrary")),
        name="nsa_prompt",
    )(q5, kh, nvt, kh, nvt, kcvc, vct, gat,
      consts["kaug_sel"], consts["kaug_win"], consts["kaug_cmp"], consts["ovlt"])


FOX_BIAS0 = 64
FOX_PAIR = 2


def _fox_prompt_kernel(q_ref, k_ref, vt_ref, lf_ref, tril_ref, o_ref, c_s, kaug_s, m_ref, acc_ref, *, tq, tk):
    hp = pl.program_id(1)
    qt = pl.program_id(2)
    q0 = qt * tq
    t = k_ref.shape[1]

    @pl.when((hp == 0) & (qt == 0))
    def _():
        tb = tril_ref.shape[0]
        tril = tril_ref[...]
        carry = jnp.zeros((1, FOX_HEADS), F32)
        cs = []
        for i in range(t // tb):
            hi, mid, lo = _split3(lf_ref[i * tb:(i + 1) * tb, :])
            c = _dot(tril, hi) + _dot(tril, mid) + _dot(tril, lo) + carry
            cs.append(c)
            carry = c[tb - 1:tb, :]
        for j, piece in enumerate(_split3(-jnp.concatenate(cs, axis=0))):
            c_s[j] = piece

    @pl.when(qt == 0)
    def _():
        hi, mid, lo = c_s[0], c_s[1], c_s[2]
        hrow = lax.broadcasted_iota(jnp.int32, (FOX_HEADS, LANES), 0)
        lane = lax.broadcasted_iota(jnp.int32, (FOX_HEADS, LANES), 1)
        for i in range(FOX_PAIR):
            head = FOX_PAIR * hp + i

            def place(j):
                return jnp.where((hrow == head) & (lane == FOX_BIAS0 + j), 1.0, 0.0).astype(BF16)

            aug = _dot(hi, place(0)) + _dot(mid, place(1)) + _dot(lo, place(2))
            kaug_s[i] = k_ref[i] + aug.astype(BF16)

    sub = lax.broadcasted_iota(jnp.int32, (16, tq), 0)
    ones3 = jnp.where(sub < 3, 1.0, 0.0).astype(BF16)
    zeros = jnp.zeros((LANES - HEAD_DIM - 16, tq), BF16)
    qx = [jnp.concatenate([q_ref[i], ones3, zeros], axis=0) for i in range(FOX_PAIR)]

    for i in range(FOX_PAIR):
        _softmax_reset(m_ref.at[i], acc_ref.at[i])

    def tile(k0, masked, c0=0):
        for i in range(FOX_PAIR):
            sc = _dot(kaug_s[i, pl.ds(k0, tk), :], qx[i][:, c0:])
            if masked:
                kpos = lax.broadcasted_iota(jnp.int32, (tk, tq - c0), 0)
                rel = (q0 + c0 - k0) + lax.broadcasted_iota(jnp.int32, (1, tq - c0), 1)
                sc = jnp.where(kpos <= rel, sc, NEG)
            _softmax_update_t(sc, vt_ref[i, :, pl.ds(k0, tk)],
                              m_ref.at[i, :, pl.ds(c0, tq - c0)], acc_ref.at[i, :, pl.ds(c0, tq - c0)])

    def body(kt, carry):
        tile(pl.multiple_of(kt * tk, tk), False)
        return carry

    lax.fori_loop(0, q0 // tk, body, 0)
    for j in range(tq // tk):
        tile(pl.multiple_of(q0 + j * tk, tk), True, c0=j * tk)
    o_t = jnp.concatenate([_softmax_result(acc_ref.at[i]) for i in range(FOX_PAIR)], axis=0)
    o_ref[...] = o_t.T.astype(BF16)


def _fox_prompt(fqt, kh, fvt, logf, tril, nb, t, tq, tk):
    n_pair = FOX_HEADS // FOX_PAIR
    q5 = fqt.reshape(nb, n_pair, FOX_PAIR, HEAD_DIM, t)
    k5 = kh.reshape(nb, N_KH // FOX_PAIR, FOX_PAIR, t, LANES)
    fox_pair0 = (N_KH - FOX_HEADS) // FOX_PAIR
    v5 = fvt.reshape(nb, n_pair, FOX_PAIR, VT_ROWS, t)
    return pl.pallas_call(
        functools.partial(_fox_prompt_kernel, tq=tq, tk=tk),
        grid=(nb, n_pair, t // tq),
        in_specs=[
            pl.BlockSpec((None, None, FOX_PAIR, HEAD_DIM, tq), lambda b, hp, qt: (b, hp, 0, 0, qt)),
            pl.BlockSpec((None, None, FOX_PAIR, t, LANES), lambda b, hp, qt: (b, fox_pair0 + hp, 0, 0, 0)),
            pl.BlockSpec((None, None, FOX_PAIR, VT_ROWS, t), lambda b, hp, qt: (b, hp, 0, 0, 0)),
            pl.BlockSpec((t, FOX_HEADS), lambda b, hp, qt: (b, 0)),
            pl.BlockSpec(tril.shape, lambda b, hp, qt: (0, 0)),
        ],
        out_specs=pl.BlockSpec((None, tq, FOX_PAIR * HEAD_DIM), lambda b, hp, qt: (b, qt, hp)),
        out_shape=jax.ShapeDtypeStruct((nb, t, FOX_HEADS * HEAD_DIM), BF16),
        scratch_shapes=[
            pltpu.VMEM((3, t, FOX_HEADS), BF16),
            pltpu.VMEM((FOX_PAIR, t, LANES), BF16),
            pltpu.VMEM((FOX_PAIR, 1, tq), F32), pltpu.VMEM((FOX_PAIR, VT_ROWS, tq), F32),
        ],
        compiler_params=_cparams(("parallel", "arbitrary", "arbitrary")),
        name="fox_prompt",
    )(q5, k5, v5, logf, tril)


def _merge_ln_kernel(oa_ref, ob_ref, gm_ref, h_ref, wa_ref, wb_ref, wo_ref, g_ref, b_ref, o_ref, *, alpha):
    pa = _dot(oa_ref[...], wa_ref[...])
    pb = _dot(ob_ref[...], wb_ref[...])
    merged = gm_ref[:, :D_MODEL] * pa + gm_ref[:, D_MODEL:] * pb
    y = alpha * h_ref[...] + _dot(merged.astype(BF16), wo_ref[...])
    mu = jnp.mean(y, axis=-1, keepdims=True)
    d = y - mu
    var = jnp.mean(d * d, axis=-1, keepdims=True)
    o_ref[...] = d * lax.rsqrt(var + LN_EPS) * g_ref[...] + b_ref[...]


def _merge_ln(oa, ob, gm, h, wa, wb, wo, g, b, alpha, tm):
    n, d = h.shape
    row = lambda i: (i, 0)
    fixed = lambda i: (0, 0)
    return pl.pallas_call(
        functools.partial(_merge_ln_kernel, alpha=alpha),
        grid=(n // tm,),
        in_specs=[
            pl.BlockSpec((tm, 512), row), pl.BlockSpec((tm, 512), row),
            pl.BlockSpec((tm, 2 * d), row), pl.BlockSpec((tm, d), row),
            pl.BlockSpec((512, d), fixed), pl.BlockSpec((512, d), fixed), pl.BlockSpec((d, d), fixed),
            pl.BlockSpec((1, d), fixed), pl.BlockSpec((1, d), fixed),
        ],
        out_specs=pl.BlockSpec((tm, d), row),
        out_shape=jax.ShapeDtypeStruct((n, d), F32),
        compiler_params=_cparams(("parallel",)),
        name="merge_ln",
    )(oa, ob, gm, h, wa, wb, wo, g, b)


SAMPLE_ROWS = 32


def _row_slopes(rows, width, head0=0):
    row = lax.broadcasted_iota(jnp.int32, (rows, width), 0)
    head = head0 + (row >> 2)
    slope = lax.bitcast_convert_type((126 - head) << 23, F32)
    return slope, row & 3


def _sample_select_kernel(q_ref, kc_ref, vc_ref, ovl_ref, ocmp_ref, sel_ref, *, q_off, n_cmp, n_slc, t_new):
    n_key = kc_ref.shape[1]
    hrows = SAMPLE_ROWS // NSA_KV_HEADS
    lane = lax.broadcasted_iota(jnp.int32, (hrows, n_key), 1)
    ps, os_ = [], []
    for g in range(NSA_KV_HEADS):
        slope, tok = _row_slopes(hrows, n_key, head0=g * NSA_GROUP)
        dist = (q_off + tok) - (lane * CMP_STRIDE + (CMP_BLOCK - 1))
        valid = (dist >= 0) & (lane < n_cmp)
        s = _dot_nt(q_ref[g * hrows:(g + 1) * hrows], kc_ref[g]) - slope * dist.astype(F32)
        s = jnp.where(valid, s, NEG)
        m = jnp.max(s, axis=1, keepdims=True)
        p = jnp.where(valid, jnp.exp(s - m), 0.0)
        den = jnp.sum(p, axis=1, keepdims=True)
        p = p / jnp.where(den > 0.0, den, 1.0)
        ps.append(p)
        os_.append(_dot(p.astype(BF16), vc_ref[g]))
    ocmp_ref[...] = jnp.concatenate(os_, axis=0)

    p_all = jnp.concatenate(ps, axis=0)
    rr = lax.broadcasted_iota(jnp.int32, (2 * 8, SAMPLE_ROWS), 0)
    cc = lax.broadcasted_iota(jnp.int32, (2 * 8, SAMPLE_ROWS), 1)
    gather = jnp.where(((rr >> 3) == (cc >> 4)) & ((rr & 7) == (cc & 3)), 1.0, 0.0).astype(BF16)
    hi, mid, lo = _split3(p_all)
    imp = _dot(gather, hi) + _dot(gather, mid) + _dot(gather, lo)
    imp_slc = _dot3(imp, ovl_ref[...])
    nb_pad = imp_slc.shape[1]
    blk = lax.broadcasted_iota(jnp.int32, (16, nb_pad), 1)
    blk_f = blk.astype(F32)
    tpos = q_off + (lax.broadcasted_iota(jnp.int32, (16, nb_pad), 0) & 7)
    cur = tpos >> 6
    forced = (blk == 0) | (blk == cur) | (blk == cur - 1)
    score = jnp.where(forced, FORCE, jnp.where(blk * SEL_BLOCK <= tpos, imp_slc, -FORCE))
    score = jnp.where(blk < n_slc, score, -3.0 * FORCE)
    sel = jnp.zeros((16, nb_pad), F32)
    for _ in range(min(N_SEL, n_slc)):
        mx = jnp.max(score, axis=1, keepdims=True)
        first = jnp.min(jnp.where(score == mx, blk_f, float(nb_pad)), axis=1, keepdims=True)
        hit = blk_f == first
        sel = jnp.where(hit, 1.0, sel)
        score = jnp.where(hit, -4.0 * FORCE, score)
    sel_ref[...] = sel


def _sample_select(q32, kcvc, ovl, q_off, n_cmp, n_slc, t_new):
    nb = q32.shape[0]
    n_key = kcvc.shape[2]
    nb_pad = ovl.shape[1]
    return pl.pallas_call(
        functools.partial(_sample_select_kernel, q_off=q_off, n_cmp=n_cmp, n_slc=n_slc, t_new=t_new),
        grid=(nb,),
        in_specs=[
            pl.BlockSpec((None, SAMPLE_ROWS, LANES), lambda b: (b, 0, 0)),
            pl.BlockSpec((None, None, 2, n_key, LANES), lambda b: (b, 0, 0, 0, 0)),
            pl.BlockSpec((None, None, 2, n_key, LANES), lambda b: (b, 1, 0, 0, 0)),
            pl.BlockSpec((n_key, nb_pad), lambda b: (0, 0)),
        ],
        out_specs=[
            pl.BlockSpec((None, SAMPLE_ROWS, LANES), lambda b: (b, 0, 0)),
            pl.BlockSpec((None, 16, nb_pad), lambda b: (b, 0, 0)),
        ],
        out_shape=[
            jax.ShapeDtypeStruct((nb, SAMPLE_ROWS, LANES), F32),
            jax.ShapeDtypeStruct((nb, 16, nb_pad), F32),
        ],
        compiler_params=_cparams(("parallel",)),
        name="sample_select",
    )(q32, kcvc.reshape(nb, 2, 2, n_key, LANES), kcvc.reshape(nb, 2, 2, n_key, LANES), ovl)


TAIL = 128


def _softmax_update_pages(s, vts, m_ref, l_ref, acc_ref):
    m_prev = m_ref[...]
    m_new = jnp.maximum(m_prev, jnp.max(s, axis=1, keepdims=True))
    a = jnp.exp(m_prev - m_new)
    p = jnp.exp(s - m_new)
    l_ref[...] = a * l_ref[...] + jnp.sum(p, axis=1, keepdims=True)
    pb = p.astype(BF16)
    pv = None
    for j, vt in enumerate(vts):
        d = _dot_nt(pb[:, j * PAGE:(j + 1) * PAGE], vt)
        pv = d if pv is None else pv + d
    acc_ref[...] = a * acc_ref[...] + pv
    m_ref[...] = m_new


def _tail_scores(q, knew_t, t_new):
    s = _dot(q, knew_t)
    slope, tok = _row_slopes(SAMPLE_ROWS, TAIL)
    j = lax.broadcasted_iota(jnp.int32, s.shape, 1)
    d = tok - j
    s = s - slope * d.astype(F32)
    return jnp.where((d >= 0) & (j < t_new), s, NEG)


def _sample_nsa_kernel(pt_ref, *refs, n_pages, n_steps, q_off, t_new):
    k_refs = refs[:n_pages]
    v_refs = refs[n_pages:2 * n_pages]
    (q_ref, selx_ref, exp_ref, knew_ref, vnew_ref, wk_ref, wv_ref, wknew_ref, wvnew_ref, ocmp_ref, gate_ref,
     o_ref, m_ref, l_ref, acc_ref) = refs[2 * n_pages:]
    pg = pl.program_id(1)
    n_key = n_pages * PAGE
    q = q_ref[...]
    slope, tok = _row_slopes(SAMPLE_ROWS, n_key)

    @pl.when(pg == 0)
    def _():
        m_ref[...] = jnp.full_like(m_ref, NEG)
        l_ref[...] = jnp.zeros_like(l_ref)
        acc_ref[...] = jnp.zeros_like(acc_ref)

    s = jnp.concatenate([_dot(q, r[...].astype(BF16)) for r in k_refs], axis=1)
    kpos = pg * n_key + lax.broadcasted_iota(jnp.int32, (SAMPLE_ROWS, n_key), 1)
    dist = (q_off + tok) - kpos
    s = s - slope * dist.astype(F32)
    chosen = _dot(selx_ref[...], exp_ref[...]) > 0.5
    s = jnp.where(chosen & (dist >= 0), s, NEG)
    _softmax_update_pages(s, [r[...].astype(BF16) for r in v_refs], m_ref, l_ref, acc_ref)

    @pl.when(pg == n_steps - 1)
    def _():
        lane = lax.broadcasted_iota(jnp.int32, (SAMPLE_ROWS, LANES), 1)
        row = lax.broadcasted_iota(jnp.int32, (SAMPLE_ROWS, LANES), 0)

        def own_group(x):
            return jnp.where(lane < HEAD_DIM, jnp.where(row < 16, x, pltpu.roll(x, HEAD_DIM, axis=1)), 0.0)

        st = _tail_scores(q, knew_ref[...], t_new)
        _softmax_update(st, vnew_ref[...], m_ref, l_ref, acc_ref)
        o_sel = own_group(acc_ref[...] / l_ref[...])

        sw = _dot(q, wk_ref[...].astype(BF16))
        wslope, wtok = _row_slopes(SAMPLE_ROWS, WINDOW)
        wpos = (q_off - WINDOW) + lax.broadcasted_iota(jnp.int32, (SAMPLE_ROWS, WINDOW), 1)
        wd = (q_off + wtok) - wpos
        sw = sw - wslope * wd.astype(F32)
        sw = jnp.where((wd >= 0) & (wd < WINDOW) & (wpos >= 0), sw, NEG)
        swt = _tail_scores(q, wknew_ref[...], t_new)
        mw = jnp.maximum(jnp.max(sw, axis=1, keepdims=True), jnp.max(swt, axis=1, keepdims=True))
        pw = jnp.exp(sw - mw)
        pwt = jnp.exp(swt - mw)
        lw = jnp.sum(pw, axis=1, keepdims=True) + jnp.sum(pwt, axis=1, keepdims=True)
        ow = _dot_nt(pw.astype(BF16), wv_ref[...].astype(BF16)) + _dot(pwt.astype(BF16), wvnew_ref[...])
        o_win = own_group(ow / lw)

        gate = gate_ref[...]
        o_ref[...] = gate[:, 0:1] * ocmp_ref[...] + gate[:, 1:2] * o_sel + gate[:, 2:3] * o_win


def _sample_nsa(pool_t, page_table, q32, selx, expand, knew_t, vnew, wbuf_t, wknew_t, wvnew, ocmp, gate32,
                n_pages, q_off, t_new):
    nb, pages_per_seq = page_table.shape
    n_steps = pages_per_seq // n_pages
    page_specs = [
        pl.BlockSpec((None, None, LANES, PAGE), lambda b, pg, pt, j=j, c=c: (pt[b, pg * n_pages + j], c, 0, 0))
        for c in (2, 3) for j in range(n_pages)
    ]
    per_b = lambda shape: pl.BlockSpec((None,) + shape, lambda b, pg, pt: (b,) + (0,) * len(shape))
    win_spec = lambda c: pl.BlockSpec((None, None, LANES, WINDOW), lambda b, pg, pt: (b, c, 0, 0))
    grid_spec = pltpu.PrefetchScalarGridSpec(
        num_scalar_prefetch=1,
        grid=(nb, n_steps),
        in_specs=page_specs + [
            per_b((SAMPLE_ROWS, LANES)),
            pl.BlockSpec((None, None, SAMPLE_ROWS, LANES), lambda b, pg, pt: (b, pg, 0, 0)),
            pl.BlockSpec((LANES, n_pages * PAGE), lambda b, pg, pt: (0, 0)),
            per_b((LANES, TAIL)), per_b((TAIL, LANES)),
            win_spec(0), win_spec(1),
            per_b((LANES, TAIL)), per_b((TAIL, LANES)),
            per_b((SAMPLE_ROWS, LANES)), per_b((SAMPLE_ROWS, LANES)),
        ],
        out_specs=per_b((SAMPLE_ROWS, LANES)),
        scratch_shapes=[
            pltpu.VMEM((SAMPLE_ROWS, 1), F32), pltpu.VMEM((SAMPLE_ROWS, 1), F32),
            pltpu.VMEM((SAMPLE_ROWS, LANES), F32),
        ],
    )
    return pl.pallas_call(
        functools.partial(_sample_nsa_kernel, n_pages=n_pages, n_steps=n_steps, q_off=q_off, t_new=t_new),
        grid_spec=grid_spec,
        out_shape=jax.ShapeDtypeStruct((nb, SAMPLE_ROWS, LANES), F32),
        compiler_params=_cparams(("parallel", "arbitrary")),
        name="sample_nsa",
    )(page_table, *([pool_t] * (2 * n_pages)), q32, selx, expand, knew_t, vnew, wbuf_t, wbuf_t, wknew_t, wvnew,
      ocmp, gate32)


def _sample_fox_kernel(pt_ref, *refs, n_pages, n_steps, t_new):
    k_refs = refs[:n_pages]
    v_refs = refs[n_pages:2 * n_pages]
    lf_refs = refs[2 * n_pages:3 * n_pages]
    (q_ref, triu_ref, knew_ref, vnew_ref, lfnew_ref,
     o_ref, m_ref, l_ref, acc_ref, carry_ref) = refs[3 * n_pages:]
    pg = pl.program_id(1)
    q = q_ref[...]

    @pl.when(pg == 0)
    def _():
        m_ref[...] = jnp.full_like(m_ref, NEG)
        l_ref[...] = jnp.zeros_like(l_ref)
        acc_ref[...] = jnp.zeros_like(acc_ref)
        carry_ref[...] = jnp.zeros_like(carry_ref)

    triu = triu_ref[...]

    def per_token(c):
        return jnp.concatenate([c] * (SAMPLE_ROWS // FOX_HEADS), axis=0)

    lf_all = jnp.concatenate([r[...] for r in lf_refs], axis=0)
    prefix = _dot3(lf_all, triu)
    totals = jnp.sum(lf_all, axis=1, keepdims=True)
    carry = carry_ref[...]
    scores = []
    for j in range(n_pages):
        rows = slice(j * FOX_HEADS, (j + 1) * FOX_HEADS)
        bias = per_token(prefix[rows] + carry)
        carry = carry + totals[rows]
        scores.append(_dot(q, k_refs[j][...].astype(BF16)) - bias)
    carry_ref[...] = carry
    _softmax_update_pages(jnp.concatenate(scores, axis=1), [r[...].astype(BF16) for r in v_refs],
                          m_ref, l_ref, acc_ref)

    @pl.when(pg == n_steps - 1)
    def _():
        bias_n = per_token(_dot3(lfnew_ref[...], triu) + carry)
        sn = _dot(q, knew_ref[...]) - bias_n
        j = lax.broadcasted_iota(jnp.int32, sn.shape, 1)
        tok = lax.broadcasted_iota(jnp.int32, sn.shape, 0) >> 3
        sn = jnp.where((j <= tok) & (j < t_new), sn, NEG)
        _softmax_update(sn, vnew_ref[...], m_ref, l_ref, acc_ref)
        o_ref[...] = acc_ref[...] / l_ref[...]


def _sample_fox(pool_t, lf_t, page_table, qbd, triu, knew_t, vnew, lfnew_t, n_pages, t_new):
    nb, pages_per_seq = page_table.shape
    n_steps = pages_per_seq // n_pages
    width = FOX_HEADS * HEAD_DIM
    kv_specs = [
        pl.BlockSpec((None, None, width, PAGE), lambda b, pg, pt, j=j, c=c: (pt[b, pg * n_pages + j], c, 0, 0))
        for c in range(2) for j in range(n_pages)
    ]
    lf_specs = [
        pl.BlockSpec((None, FOX_HEADS, PAGE), lambda b, pg, pt, j=j: (pt[b, pg * n_pages + j], 0, 0))
        for j in range(n_pages)
    ]
    per_b = lambda shape: pl.BlockSpec((None,) + shape, lambda b, pg, pt: (b,) + (0,) * len(shape))
    grid_spec = pltpu.PrefetchScalarGridSpec(
        num_scalar_prefetch=1,
        grid=(nb, n_steps),
        in_specs=kv_specs + lf_specs + [
            per_b((SAMPLE_ROWS, width)),
            pl.BlockSpec((PAGE, PAGE), lambda b, pg, pt: (0, 0)),
            per_b((width, TAIL)), per_b((TAIL, width)), per_b((FOX_HEADS, TAIL)),
        ],
        out_specs=per_b((SAMPLE_ROWS, width)),
        scratch_shapes=[
            pltpu.VMEM((SAMPLE_ROWS, 1), F32), pltpu.VMEM((SAMPLE_ROWS, 1), F32),
            pltpu.VMEM((SAMPLE_ROWS, width), F32), pltpu.VMEM((FOX_HEADS, 1), F32),
        ],
    )
    return pl.pallas_call(
        functools.partial(_sample_fox_kernel, n_pages=n_pages, n_steps=n_steps, t_new=t_new),
        grid_spec=grid_spec,
        out_shape=jax.ShapeDtypeStruct((nb, SAMPLE_ROWS, width), F32),
        compiler_params=_cparams(("parallel", "arbitrary")),
        name="sample_fox",
    )(page_table, *([pool_t] * (2 * n_pages)), *([lf_t] * n_pages), qbd, triu, knew_t, vnew, lfnew_t)


def _alibi_key_cols(pos):
    tab = np.zeros((len(pos), LANES), np.float32)
    tab[:, ALIBI0] = 1.0
    tab[:, ALIBI0 + 1] = 1.0
    tab[:, ALIBI0 + 2] = (pos // 128) * 128
    tab[:, ALIBI0 + 3] = pos % 128
    return tab


def _prompt_consts(t):
    pos = np.arange(t)
    kaug_win = _alibi_key_cols(pos)
    kaug_sel = kaug_win.copy()
    kaug_sel[pos, SELBIT0 + pos // SEL_BLOCK] = -MASK_BIG
    n_chunk = t // CMP_STRIDE
    kaug_cmp = _alibi_key_cols(np.arange(n_chunk) * CMP_STRIDE + CMP_BLOCK - 1)
    n_cmp = (t - CMP_BLOCK) // CMP_STRIDE + 1
    n_slc = -(-t // SEL_BLOCK)
    ovlt = np.zeros((n_slc, n_chunk), np.float32)
    cs = np.arange(n_cmp)[None, :] * CMP_STRIDE
    ss = np.arange(n_slc)[:, None] * SEL_BLOCK
    ovlt[:, :n_cmp] = (cs < ss + SEL_BLOCK) & (cs + CMP_BLOCK > ss)
    tril = np.tril(np.ones((512, 512), np.float32))
    as_bf = lambda a: jnp.asarray(a, BF16)
    return dict(kaug_sel=as_bf(kaug_sel), kaug_win=as_bf(kaug_win), kaug_cmp=as_bf(kaug_cmp),
                ovlt=as_bf(ovlt), tril=as_bf(tril))


def _sample_ovl(n_key, n_cmp, n_slc, nb_pad):
    ovl = np.zeros((n_key, nb_pad), np.float32)
    cs = np.arange(n_cmp)[:, None] * CMP_STRIDE
    ss = np.arange(n_slc)[None, :] * SEL_BLOCK
    ovl[:n_cmp, :n_slc] = (cs < ss + SEL_BLOCK) & (cs + CMP_BLOCK > ss)
    return jnp.asarray(ovl, BF16)


def _perm_in_proj(w_in, b_in):
    o_qa, o_kva, o_ga, o_fox, o_fb, o_gm = 0, 512, 1280, 1304, 2840, 2848
    d = w_in.shape[0]

    def cols(x, zeros):
        return jnp.concatenate([x[..., o_gm:], x[..., o_fb:o_gm], zeros(LANES - FOX_HEADS)], axis=-1)

    def rows(x, zeros):
        return jnp.concatenate([x[..., o_qa:o_ga], x[..., o_fox:o_fb],
                                x[..., o_ga:o_fox], zeros(R_FB - R_GA - 3 * NSA_HEADS),
                                x[..., o_fb:o_gm], zeros(R_TOT - R_FB - FOX_HEADS)], axis=-1)

    w = cols(w_in, lambda k: jnp.zeros((d, k), w_in.dtype)).astype(BF16)
    b = cols(b_in, lambda k: jnp.zeros((k,), b_in.dtype)).reshape(1, C_TOT)
    wt = rows(w_in, lambda k: jnp.zeros((d, k), w_in.dtype)).T.astype(BF16)
    bt = rows(b_in, lambda k: jnp.zeros((k,), b_in.dtype)).reshape(R_TOT, 1)
    return w, b, wt, bt


def _compress_weights(pe_k, w1_k, w2_k, pe_v, w1_v, w2_v):
    n_sub = CMP_BLOCK // CMP_STRIDE

    def per_row(w1):
        w = w1.reshape(n_sub, CMP_STRIDE, HEAD_DIM, CMP_HIDDEN)
        return w.transpose(1, 2, 0, 3).reshape(CMP_STRIDE, HEAD_DIM, n_sub * CMP_HIDDEN)

    def per_kind(w1):
        blk = per_row(w1)
        zero = jnp.zeros_like(blk)
        w = jnp.stack([jnp.concatenate([blk, zero], axis=-1), jnp.concatenate([zero, blk], axis=-1)], axis=1)
        return w.reshape(CMP_STRIDE * NSA_KV_HEADS * HEAD_DIM, NSA_KV_HEADS * n_sub * CMP_HIDDEN)

    wr = jnp.stack([per_kind(w1_k), per_kind(w1_v)]).astype(BF16)
    pe8 = jnp.concatenate([pe_k.reshape(1, -1), pe_v.reshape(1, -1), jnp.zeros((14, CMP_BLOCK * HEAD_DIM), F32)], axis=0)
    w1cat = jnp.concatenate([w1_k, w1_v], axis=1).astype(BF16)
    pad = jnp.zeros((CMP_HIDDEN, LANES - HEAD_DIM), F32)
    w2pad = jnp.stack([jnp.concatenate([w2_k, pad], axis=1), jnp.concatenate([w2_v, pad], axis=1)]).astype(BF16)
    return dict(wr=wr, pe8=pe8.astype(BF16), w1cat=w1cat, w2pad=w2pad)


def _prompt_mixer(h1, w, nb, t):
    cmp_rows, logf, kh, gm, nsat, wint, foxt, logft, qat, nvt, fqt, fvt, gat = _in_proj(
        h1, w["w_in"], w["b_in"], w["wt"], w["bt"], nb, t, tm=TM_PROJ)
    consts = _prompt_consts(t)
    kcvc, vct = _compress_prompt(cmp_rows, w["cmp"], nb, t)
    o_a = _nsa_prompt(qat, kh, nvt, kcvc, vct, gat, consts, nb, t, tq=NSA_TQ, tk=NSA_TK)
    o_b = _fox_prompt(fqt, kh, fvt, logf, consts["tril"], nb, t, tq=FOX_TQ, tk=FOX_TK)
    n = nb * t
    state = (nsat.reshape(nb, 4, NSA_KV_HEADS, HEAD_DIM, t).transpose(0, 4, 1, 2, 3),
             wint.reshape(nb, 2, NSA_KV_HEADS, HEAD_DIM, t).transpose(0, 4, 1, 2, 3),
             foxt.reshape(nb, 2, FOX_HEADS, HEAD_DIM, t).transpose(0, 4, 1, 2, 3),
             logft.transpose(0, 2, 1))
    return o_a.reshape(n, -1), o_b.reshape(n, -1), gm, state


def _sample_mixer(h1, w, nb, t_new, nsa_pool, fox_pool, logf_pool, win_buf, page_table):
    n = nb * t_new
    past = page_table.shape[1] * PAGE
    _, _, _, gm, nsat, wint, foxt, logft, qat, _, fqt, _, gat = _in_proj(
        h1, w["w_in"], w["b_in"], w["wt"], w["bt"], 1, n, tm=n)
    nsa, win, fox, logf = nsat[0].T, wint[0].T, foxt[0].T, logft[0].T
    ga = gat[:3 * NSA_HEADS].T
    n_pool = nsa_pool.shape[0]
    nsa_t = nsa_pool.transpose(0, 2, 3, 4, 1).reshape(n_pool, 4, NSA_KV_HEADS * HEAD_DIM, PAGE)
    fox_t = fox_pool.transpose(0, 2, 3, 4, 1).reshape(n_pool, 2, FOX_HEADS * HEAD_DIM, PAGE)
    lf_t = logf_pool.transpose(0, 2, 1)
    wbuf_t = win_buf.transpose(0, 2, 3, 4, 1).reshape(nb, 2, NSA_KV_HEADS * HEAD_DIM, win_buf.shape[1])
    assert win_buf.shape[1] == WINDOW and t_new == 4 and past % SEL_BLOCK == 0

    seq_len = past + t_new
    n_cmp = (seq_len - CMP_BLOCK) // CMP_STRIDE + 1
    n_slc = -(-seq_len // SEL_BLOCK)
    n_chunk = past // CMP_STRIDE
    assert n_cmp <= n_chunk
    kcvc = _compress_sample(nsa_t, page_table, w["cmp"], n_pages=CMP_PAGES)

    q = qat[0].reshape(NSA_HEADS, HEAD_DIM, nb, t_new).transpose(2, 0, 3, 1).reshape(nb, SAMPLE_ROWS, HEAD_DIM)
    q = jnp.pad(q, ((0, 0), (0, 0), (0, LANES - HEAD_DIM)))
    nb_pad = -(-n_slc // LANES) * LANES
    ocmp, sel = _sample_select(q, kcvc, _sample_ovl(n_chunk, n_cmp, n_slc, nb_pad), past, n_cmp, n_slc, t_new)

    n_pages = NSA_PAGES
    n_steps = past // (n_pages * PAGE)
    blk_per_step = n_pages * PAGE // SEL_BLOCK
    selg = sel.reshape(nb, 2, 8, nb_pad)[:, :, :t_new, :n_steps * blk_per_step]
    selg = selg.reshape(nb, 2, 1, t_new, n_steps, blk_per_step)
    selx = jnp.broadcast_to(selg, (nb, 2, NSA_GROUP, t_new, n_steps, blk_per_step))
    selx = selx.transpose(0, 4, 1, 2, 3, 5).reshape(nb, n_steps, SAMPLE_ROWS, blk_per_step)
    selx = jnp.pad(selx, ((0, 0), (0, 0), (0, 0), (0, LANES - blk_per_step))).astype(BF16)
    expand = np.zeros((LANES, n_pages * PAGE), np.float32)
    expand[np.arange(n_pages * PAGE) // SEL_BLOCK, np.arange(n_pages * PAGE)] = 1.0
    q_sel = jnp.concatenate([q[:, :16], jnp.roll(q[:, 16:], HEAD_DIM, axis=-1)], axis=1)

    def new_rows(x, c0, width=LANES):
        r = x.reshape(nb, t_new, -1)[:, :, c0:c0 + width]
        return jnp.pad(r, ((0, 0), (0, TAIL - t_new), (0, 0))).astype(BF16)

    def new_cols(x, c0, width=LANES):
        return new_rows(x, c0, width).transpose(0, 2, 1)

    gate32 = ga.reshape(nb, t_new, 3, NSA_HEADS).transpose(0, 3, 1, 2).reshape(nb, SAMPLE_ROWS, 3)
    gate32 = jnp.pad(gate32, ((0, 0), (0, 0), (0, LANES - 3)))
    o_a32 = _sample_nsa(nsa_t, page_table, q_sel, selx, jnp.asarray(expand, BF16),
                        new_cols(nsa, 256), new_rows(nsa, 384), wbuf_t, new_cols(win, 0), new_rows(win, 128),
                        ocmp, gate32, n_pages, past, t_new)
    o_a = o_a32[:, :, :HEAD_DIM].reshape(nb, NSA_HEADS, t_new, HEAD_DIM).transpose(0, 2, 1, 3)
    o_a = o_a.reshape(n, NSA_HEADS * HEAD_DIM).astype(BF16)

    width = FOX_HEADS * HEAD_DIM
    qf = fqt[0].reshape(FOX_HEADS, HEAD_DIM, nb, t_new).transpose(2, 3, 0, 1)
    eye = jnp.eye(FOX_HEADS, dtype=qf.dtype)
    qbd = (qf[:, :, :, None, :] * eye[None, None, :, :, None]).reshape(nb, SAMPLE_ROWS, width)
    triu = jnp.asarray(np.triu(np.ones((PAGE, PAGE), np.float32)), BF16)
    lfnew_t = jnp.pad(logf.reshape(nb, t_new, FOX_HEADS), ((0, 0), (0, TAIL - t_new), (0, 0))).transpose(0, 2, 1)
    o_b32 = _sample_fox(fox_t, lf_t, page_table, qbd, triu, new_cols(fox, 0, width), new_rows(fox, width, width),
                        lfnew_t, FOX_PAGES, t_new)
    o_b = o_b32.reshape(nb, t_new, FOX_HEADS, FOX_HEADS, HEAD_DIM)
    o_b = jnp.einsum("bthgd,hg->bthd", o_b, jnp.eye(FOX_HEADS, dtype=o_b.dtype))
    o_b = o_b.reshape(n, width).astype(BF16)
    state = (nsa.reshape(nb, t_new, 4, NSA_KV_HEADS, HEAD_DIM), win.reshape(nb, t_new, 2, NSA_KV_HEADS, HEAD_DIM),
             fox.reshape(nb, t_new, 2, FOX_HEADS, HEAD_DIM), logf.reshape(nb, t_new, FOX_HEADS))
    return o_a, o_b, gm, state


def _layer(x, w, alpha, mixer, tm):
    h1 = _ffn_ln(x, w["ffn1_wg"], w["ffn1_wu"], w["ffn1_wd"], w["ln1_g"], w["ln1_b"], alpha, tm)
    o_a, o_b, gm, state = mixer(h1, w)
    h2 = _merge_ln(o_a, o_b, gm, h1, w["w_proj_a"], w["w_proj_b"], w["w_out"], w["ln2_g"], w["ln2_b"], alpha, tm)
    y = _ffn_ln(h2, w["ffn2_wg"], w["ffn2_wu"], w["ffn2_wd"], w["ln3_g"], w["ln3_b"], alpha, tm)
    return y, state


def kernel(x_prompt, x_sample, cache_nsa_kv, cache_fox_kv, cache_fox_logf, state_win_kv, page_table,
           ln1_g, ln1_b, ffn1_w_gate, ffn1_w_up, ffn1_w_down, w_in, b_in,
           cmp_pe_k, cmp_w1_k, cmp_w2_k, cmp_pe_v, cmp_w1_v, cmp_w2_v,
           w_proj_a, w_proj_b, w_out, ln2_g, ln2_b,
           ffn2_w_gate, ffn2_w_up, ffn2_w_down, ln3_g, ln3_b):
    depth = w_in.shape[0]
    nb_p, t_p, d = x_prompt.shape
    nb_s, t_s, _ = x_sample.shape
    alpha = float((2.0 * depth) ** 0.25)
    h_p = x_prompt.reshape(nb_p * t_p, d)
    h_s = x_sample.reshape(nb_s * t_s, d)
    st_p, st_s = [], []
    for l in range(depth):
        wi, bi, wt, bt = _perm_in_proj(w_in[l], b_in[l])
        vec = lambda a: a[l].reshape(1, d)
        w = dict(
            ln1_g=vec(ln1_g), ln1_b=vec(ln1_b), ln2_g=vec(ln2_g), ln2_b=vec(ln2_b), ln3_g=vec(ln3_g), ln3_b=vec(ln3_b),
            ffn1_wg=ffn1_w_gate[l].astype(BF16), ffn1_wu=ffn1_w_up[l].astype(BF16), ffn1_wd=ffn1_w_down[l].astype(BF16),
            ffn2_wg=ffn2_w_gate[l].astype(BF16), ffn2_wu=ffn2_w_up[l].astype(BF16), ffn2_wd=ffn2_w_down[l].astype(BF16),
            w_in=wi, b_in=bi, wt=wt, bt=bt,
            cmp=_compress_weights(cmp_pe_k[l], cmp_w1_k[l], cmp_w2_k[l], cmp_pe_v[l], cmp_w1_v[l], cmp_w2_v[l]),
            w_proj_a=w_proj_a[l].astype(BF16), w_proj_b=w_proj_b[l].astype(BF16), w_out=w_out[l].astype(BF16),
        )
        h_p, s_p = _layer(h_p, w, alpha, functools.partial(_prompt_mixer, nb=nb_p, t=t_p), tm=TM_FFN)
        h_s, s_s = _layer(h_s, w, alpha, functools.partial(
            _sample_mixer, nb=nb_s, t_new=t_s, nsa_pool=cache_nsa_kv[l], fox_pool=cache_fox_kv[l],
            logf_pool=cache_fox_logf[l], win_buf=state_win_kv[l], page_table=page_table), tm=nb_s * t_s)
        st_p.append(s_p)
        st_s.append(s_s)

    def states(st, nb, t, win_prev):
        nsa = jnp.stack([s[0] for s in st])
        fox = jnp.stack([s[2] for s in st])
        logf = jnp.stack([s[3] for s in st])
        wins = []
        for l, s in enumerate(st):
            wr = s[1]
            if win_prev is None:
                wins.append(wr[:, t - min(WINDOW, t):])
            else:
                wins.append(jnp.concatenate([win_prev[l], wr], axis=1)[:, t:])
        return nsa, fox, logf, jnp.stack(wins)

    nsa_p, fox_p, logf_p, win_p = states(st_p, nb_p, t_p, None)
    nsa_s, fox_s, logf_s, win_s = states(st_s, nb_s, t_s, state_win_kv)
    return (h_p.reshape(nb_p, t_p, d), h_s.reshape(nb_s, t_s, d), nsa_p, fox_p, logf_p, win_p,
            nsa_s, fox_s, logf_s, win_s)
```
